```python
import functools
import jax
import jax.numpy as jnp
from jax import lax
import numpy as np

D_MODEL = 1024
BATCH = 4
SEQ = 4096
DEPTH = 1
DEC_BATCH = 32
DEC_SEQ = 64
PAST_LEN = 4096

CHUNK = 64
A_HEADS = 8
A_DK = 128
A_DV = D_MODEL // A_HEADS
B_HEADS = 8
B_KV_HEADS = 4
HEAD_DIM = 128
IDX_HEADS = 8
IDX_DIM = 64
IDX_TOPK_MAX = 256
Q_BLOCK = 128
ROPE_THETA = 10000.0
N_EXPERTS = 32
TOP_K = 4
D_FF = D_MODEL
SWIGLU_LIMIT = 7.0
SWIGLU_ALPHA = 1.702
EPS = 1e-6

A_QK = A_HEADS * A_DK
A_V = A_HEADS * A_DV
B_Q = B_HEADS * HEAD_DIM
B_KV = B_KV_HEADS * HEAD_DIM
IQ = IDX_HEADS * IDX_DIM
IN_SIZES = (A_QK, A_QK, A_V, A_V, B_Q, B_KV, B_KV, IQ, IDX_DIM, IDX_HEADS, D_MODEL, D_MODEL)
N_IN = 2 * A_QK + 2 * A_V + B_Q + 2 * B_KV + IQ + IDX_DIM + IDX_HEADS + 2 * D_MODEL

kernel_name = 'hybrid_hgrn2_dsa_moe_stream'


def split_in(z):
    offs = np.cumsum(np.array(IN_SIZES))[:-1].tolist()
    return jnp.split(z, offs, axis=-1)


def rmsnorm(x, g):
    x32 = x.astype(jnp.float32)
    y = x32 * lax.rsqrt(jnp.mean(x32 * x32, axis=-1, keepdims=True) + EPS)
    return (y * g.astype(jnp.float32)).astype(x.dtype)


def rope(x, pos):
    d = x.shape[-1]
    inv = 1.0 / (ROPE_THETA ** (jnp.arange(0, d, 2, dtype=jnp.float32) / d))
    ang = pos.astype(jnp.float32)[:, None] * inv[None, :]
    cos = jnp.cos(ang)[:, None, :]
    sin = jnp.sin(ang)[:, None, :]
    x32 = x.astype(jnp.float32)
    x1, x2 = jnp.split(x32, 2, axis=-1)
    out = jnp.concatenate([x1 * cos - x2 * sin, x2 * cos + x1 * sin], axis=-1)
    return out.astype(x.dtype)


def hgrn2_chunkwise(q, k, v, log_f, s0):
    b_, L = q.shape[0], q.shape[1]
    c = min(CHUNK, L)
    n = L // c

    def to_chunks(a):
        return jnp.moveaxis(a.reshape((b_, n, c) + a.shape[2:]), 1, 0)

    causal = jnp.tril(jnp.ones((c, c), dtype=bool))

    def step(S, xs):
        qc, kc, vc, lc = xs
        cum = jnp.cumsum(lc, axis=1)
        last = cum[:, -1]
        q_dec = qc * jnp.exp(cum)
        k_dec = kc * jnp.exp(-cum)
        att = jnp.where(causal, jnp.einsum('bthk,bshk->bhts', q_dec, k_dec), 0.0)
        o = jnp.einsum('bhts,bshv->bthv', att, vc) + jnp.einsum('bthk,bhkv->bthv', q_dec, S)
        k_tail = kc * jnp.exp(last[:, None] - cum)
        S = jnp.exp(last)[..., None] * S + jnp.einsum('bshk,bshv->bhkv', k_tail, vc)
        return S, o

    S, o = lax.scan(step, s0, (to_chunks(q), to_chunks(k), to_chunks(v), to_chunks(log_f)))
    o = jnp.moveaxis(o, 0, 1).reshape((b_, L) + o.shape[3:])
    return o, S


def dsa_one(q, qi, wi, pos_q, k, v, ki, pos_k, topk):
    f32 = jnp.float32
    s_h = jnp.einsum('thd,sd->ths', qi.astype(f32), ki.astype(f32))
    score = jnp.einsum('ths,th->ts', jax.nn.relu(s_h), wi.astype(f32))
    cq = pos_q // CHUNK
    adm = (pos_k[None, :] // CHUNK) <= cq[:, None]
    score = jnp.where(adm, score, -jnp.inf)
    _, idx = lax.top_k(score, topk)
    ok = (pos_k[idx] // CHUNK) <= cq[:, None]
    k_sel = k[idx].astype(f32)
    v_sel = v[idx].astype(f32)
    t = q.shape[0]
    qg = q.reshape(t, B_KV_HEADS, B_HEADS // B_KV_HEADS, HEAD_DIM).astype(f32)
    logits = jnp.einsum('tgrd,tkgd->tgrk', qg, k_sel) * (HEAD_DIM ** -0.5)
    logits = jnp.where(ok[:, None, None, :], logits, -jnp.inf)
    p = jax.nn.softmax(logits, axis=-1)
    o = jnp.einsum('tgrk,tkgd->tgrd', p, v_sel)
    return o.reshape(t, B_Q).astype(q.dtype)


def dsa_prompt(q, qi, wi, pos, k, v, ki):
    bsz, t = q.shape[0], q.shape[1]
    nb = t // Q_BLOCK
    topk = min(IDX_TOPK_MAX, t // 4)
    attend = jax.vmap(functools.partial(dsa_one, topk=topk), in_axes=(0, 0, 0, None, 0, 0, 0, None))

    def blocks(a):
        return jnp.moveaxis(a.reshape((bsz, nb, Q_BLOCK) + a.shape[2:]), 1, 0)

    def one_block(xs):
        qb, qib, wib, pb = xs
        return attend(qb, qib, wib, pb, k, v, ki, pos)

    out = lax.map(one_block, (blocks(q), blocks(qi), blocks(wi), pos.reshape(nb, Q_BLOCK)))
    return jnp.moveaxis(out, 0, 1).reshape(bsz, t, B_Q)


def dsa_step(q, qi, wi, pos_q, k_all, v_all, ki_all, pos_k):
    topk = min(IDX_TOPK_MAX, k_all.shape[1] // 4)
    attend = functools.partial(dsa_one, topk=topk)
    return lax.map(lambda xs: attend(xs[0], xs[1], xs[2], pos_q, xs[3], xs[4], xs[5], pos_k),
                   (q, qi, wi, k_all, v_all, ki_all))


def token_mixers(h, pos_q, k_past, v_past, ki_past, s0, lb, w_in, hgrn_norm_g, q_norm_g, k_norm_g,
                 w_branch_a, w_branch_b, w_out):
    bsz, t = h.shape[0], h.shape[1]
    f32 = jnp.float32
    aq, af, ai, ag, bq, bk, bv, iq, ik, iw, ga, gb = split_in(h @ w_in)
    lb = lb.astype(f32).reshape(A_HEADS, A_DK)
    forget = lb + (1.0 - lb) * jax.nn.sigmoid(af.astype(f32).reshape(bsz, t, A_HEADS, A_DK))
    qa = jax.nn.silu(aq.astype(f32)).reshape(bsz, t, A_HEADS, A_DK)
    va = ai.astype(f32).reshape(bsz, t, A_HEADS, A_DV)
    oa, s_new = hgrn2_chunkwise(qa, 1.0 - forget, va, jnp.log(forget), s0.astype(f32))
    oa = rmsnorm(oa, hgrn_norm_g) * jax.nn.silu(ag.astype(f32).reshape(bsz, t, A_HEADS, A_DV))
    y_a = oa.reshape(bsz, t, A_V).astype(h.dtype) @ w_branch_a
    q = rope(rmsnorm(bq.reshape(bsz, t, B_HEADS, HEAD_DIM), q_norm_g), pos_q)
    k_new = rope(rmsnorm(bk.reshape(bsz, t, B_KV_HEADS, HEAD_DIM), k_norm_g), pos_q)
    v_new = bv.reshape(bsz, t, B_KV_HEADS, HEAD_DIM)
    qi = rope(iq.reshape(bsz, t, IDX_HEADS, IDX_DIM), pos_q)
    ki_new = rope(ik[:, :, None, :], pos_q)[:, :, 0, :]
    wi = iw * (IDX_HEADS ** -0.5 * IDX_DIM ** -0.5)
    if k_past is None:
        o_b = dsa_prompt(q, qi, wi, pos_q, k_new, v_new, ki_new)
    else:
        past = k_past.shape[1]
        pos_k = jnp.arange(past + t, dtype=jnp.int32)
        o_b = dsa_step(q, qi, wi, pos_q,
                       jnp.concatenate([k_past.astype(k_new.dtype), k_new], axis=1),
                       jnp.concatenate([v_past.astype(v_new.dtype), v_new], axis=1),
                       jnp.concatenate([ki_past.astype(ki_new.dtype), ki_new], axis=1), pos_k)
    y_b = o_b @ w_branch_b
    merged = jax.nn.sigmoid(ga) * y_a + jax.nn.sigmoid(gb) * y_b
    return merged @ w_out, k_new, v_new, ki_new, s_new


def moe(x, w_router, b_router, w_gate_up, b_gate_up, w_down, b_down):
    f32 = jnp.float32
    logits = (x @ w_router + b_router).astype(f32)
    top_val, top_idx = lax.top_k(logits, TOP_K)
    probs = jax.nn.softmax(top_val, axis=-1)
    combine = jnp.einsum('nk,nke->ne', probs, jax.nn.one_hot(top_idx, N_EXPERTS, dtype=f32))
    out = jnp.zeros(x.shape, f32)
    for e in range(N_EXPERTS):
        gate, up = jnp.split(x @ w_gate_up[e] + b_gate_up[e], 2, axis=-1)
        gate = jnp.minimum(gate, SWIGLU_LIMIT)
        up = jnp.clip(up, -SWIGLU_LIMIT, SWIGLU_LIMIT)
        act = (up + 1.0) * gate * jax.nn.sigmoid(SWIGLU_ALPHA * gate)
        out = out + combine[:, e:e + 1] * (act @ w_down[e] + b_down[e]).astype(f32)
    return out.astype(x.dtype)


def setup_inputs(seed: int = 0) -> dict:
    key = jax.random.key(seed)
    ks = jax.random.split(key, 24)
    f32 = jnp.float32

    def nrm(k, shape, scale):
        return jax.random.normal(k, shape, f32) * scale

    return {
        'x_prompt': nrm(ks[0], (BATCH, SEQ, D_MODEL), 1.0),
        'x_sample': nrm(ks[1], (DEC_BATCH, DEC_SEQ, D_MODEL), 1.0),
        'cache_k': nrm(ks[2], (DEPTH, DEC_BATCH, PAST_LEN, B_KV_HEADS, HEAD_DIM), 1.0),
        'cache_v': nrm(ks[3], (DEPTH, DEC_BATCH, PAST_LEN, B_KV_HEADS, HEAD_DIM), 1.0),
        'cache_kidx': nrm(ks[4], (DEPTH, DEC_BATCH, PAST_LEN, IDX_DIM), 1.0),
        'state_hgrn': nrm(ks[5], (DEPTH, DEC_BATCH, A_HEADS, A_DK, A_DV), 0.5),
        'norm1_g': 1.0 + nrm(ks[6], (DEPTH, D_MODEL), 0.02),
        'w_in': nrm(ks[7], (DEPTH, D_MODEL, N_IN), D_MODEL ** -0.5),
        'lower_bounds': 1.0 + nrm(ks[8], (DEPTH + 1, A_QK), 0.1),
        'hgrn_norm_g': 1.0 + nrm(ks[9], (DEPTH, A_DV), 0.02),
        'q_norm_g': 1.0 + nrm(ks[10], (DEPTH, HEAD_DIM), 0.02),
        'k_norm_g': 1.0 + nrm(ks[11], (DEPTH, HEAD_DIM), 0.02),
        'w_branch_a': nrm(ks[12], (DEPTH, A_V, D_MODEL), A_V ** -0.5),
        'w_branch_b': nrm(ks[13], (DEPTH, B_Q, D_MODEL), B_Q ** -0.5),
        'w_out': nrm(ks[14], (DEPTH, D_MODEL, D_MODEL), D_MODEL ** -0.5),
        'norm2_g': 1.0 + nrm(ks[15], (DEPTH, D_MODEL), 0.02),
        'w_router': nrm(ks[16], (DEPTH, D_MODEL, N_EXPERTS), D_MODEL ** -0.5),
        'b_router': nrm(ks[17], (DEPTH, N_EXPERTS), 0.01),
        'w_gate_up': nrm(ks[18], (DEPTH, N_EXPERTS, D_MODEL, 2 * D_FF), D_MODEL ** -0.5),
        'b_gate_up': nrm(ks[19], (DEPTH, N_EXPERTS, 2 * D_FF), 0.01),
        'w_down': nrm(ks[20], (DEPTH, N_EXPERTS, D_FF, D_MODEL), D_FF ** -0.5),
        'b_down': nrm(ks[21], (DEPTH, N_EXPERTS, D_MODEL), 0.01),
    }


def reference(x_prompt, x_sample, cache_k, cache_v, cache_kidx, state_hgrn, norm1_g, w_in, lower_bounds,
              hgrn_norm_g, q_norm_g, k_norm_g, w_branch_a, w_branch_b, w_out, norm2_g, w_router, b_router,
              w_gate_up, b_gate_up, w_down, b_down):
    xp, xs = x_prompt, x_sample
    bp, tp = xp.shape[0], xp.shape[1]
    bs, ts = xs.shape[0], xs.shape[1]
    past = cache_k.shape[2]
    pos_p = jnp.arange(tp, dtype=jnp.int32)
    pos_s = past + jnp.arange(ts, dtype=jnp.int32)
    lbs = jnp.cumsum(jax.nn.softmax(lower_bounds.astype(jnp.float32), axis=0), axis=0)
    kp_l, vp_l, kip_l, sp_l, ks_l, vs_l, kis_l, ss_l = [], [], [], [], [], [], [], []
    for l in range(DEPTH):
        s0_p = jnp.zeros((bp, A_HEADS, A_DK, A_DV), jnp.float32)
        mp, kp, vp, kip, sp = token_mixers(rmsnorm(xp, norm1_g[l]), pos_p, None, None, None, s0_p, lbs[l],
                                           w_in[l], hgrn_norm_g[l], q_norm_g[l], k_norm_g[l],
                                           w_branch_a[l], w_branch_b[l], w_out[l])
        ms, ks, vs, kis, ss = token_mixers(rmsnorm(xs, norm1_g[l]), pos_s, cache_k[l], cache_v[l],
                                           cache_kidx[l], state_hgrn[l], lbs[l], w_in[l], hgrn_norm_g[l],
                                           q_norm_g[l], k_norm_g[l], w_branch_a[l], w_branch_b[l], w_out[l])
        xp = xp + mp
        xs = xs + ms
        tokens = jnp.concatenate([rmsnorm(xp, norm2_g[l]).reshape(bp * tp, -1),
                                  rmsnorm(xs, norm2_g[l]).reshape(bs * ts, -1)], axis=0)
        ffn = moe(tokens, w_router[l], b_router[l], w_gate_up[l], b_gate_up[l], w_down[l], b_down[l])
        xp = xp + ffn[:bp * tp].reshape(xp.shape)
        xs = xs + ffn[bp * tp:].reshape(xs.shape)
        kp_l.append(kp); vp_l.append(vp); kip_l.append(kip); sp_l.append(sp.astype(x_prompt.dtype))
        ks_l.append(ks); vs_l.append(vs); kis_l.append(kis); ss_l.append(ss.astype(state_hgrn.dtype))
    return (xp, xs, jnp.stack(kp_l), jnp.stack(vp_l), jnp.stack(kip_l), jnp.stack(sp_l),
            jnp.stack(ks_l), jnp.stack(vs_l), jnp.stack(kis_l), jnp.stack(ss_l))
```

```python
import functools

import jax
import jax.numpy as jnp
from jax import lax
from jax.experimental import pallas as pl
from jax.experimental.pallas import tpu as pltpu

F32 = jnp.float32
BF16 = jnp.bfloat16
I32 = jnp.int32

CHUNK = 64
A_HEADS = 8
A_DK = 128
A_DV = 128
B_HEADS = 8
B_KV_HEADS = 4
HEAD_DIM = 128
IDX_HEADS = 8
IDX_DIM = 64
IDX_TOPK_MAX = 256
ROPE_THETA = 10000.0
N_EXPERTS = 32
TOP_K = 4
SWIGLU_LIMIT = 7.0
SWIGLU_ALPHA = 1.702
EPS = 1e-6

LANES = 128
INT_MIN = -(2 ** 31)
NEG = -1e30
VMEM_LIMIT = 56 * 1024 * 1024


def _rms(x):
    return x * lax.rsqrt(jnp.mean(x * x, axis=-1, keepdims=True) + EPS)


def _silu(x):
    return x * jax.nn.sigmoid(x)


def _dot(a, b):
    return jnp.dot(a, b, preferred_element_type=F32)


def _dot_nt(a, b):
    return lax.dot_general(a, b, (((1,), (1,)), ((), ())), preferred_element_type=F32)


def _dot_tn(a, b):
    return lax.dot_general(a, b, (((0,), (0,)), ((), ())), preferred_element_type=F32)


def _const_spec(shape):
    zeros = (0,) * len(shape)
    return pl.BlockSpec(shape, lambda *_: zeros, pipeline_mode=pl.Buffered(1))


def _rope_tables(pos, d, reps):
    inv = 1.0 / (ROPE_THETA ** (jnp.arange(0, d, 2, dtype=F32) / d))
    ang = pos.astype(F32)[:, None] * inv[None, :]
    cos = jnp.cos(ang)
    sin = jnp.sin(ang)
    cos_t = jnp.concatenate([cos, cos] * reps, axis=-1)
    sin_t = jnp.concatenate([-sin, sin] * reps, axis=-1)
    return cos_t, sin_t


def _inproj_body(x_ref, g1_ref, wm_ref, ws_ref, wg_ref, qg_ref, kg_ref, ca_ref, sa_ref, cb_ref,
                 sb_ref, aq_o, af_o, ai_o, ag_o, q_o, kf_o, kb_o, vf_o, vb_o, qim_o, kif_o, ki2_o,
                 wi_o, ga_o, gb_o):
    x = x_ref[...]
    tm = x.shape[0]
    hb = (_rms(x) * g1_ref[...]).astype(BF16)
    a_qk = A_HEADS * A_DK
    a_v = A_HEADS * A_DV
    b_q = B_HEADS * HEAD_DIM
    b_kv = B_KV_HEADS * HEAD_DIM
    iq_w = IDX_HEADS * IDX_DIM
    o = 0
    aq_o[...] = _dot(hb, wm_ref[:, o:o + a_qk]).astype(BF16)
    o += a_qk
    af_o[...] = _dot(hb, wm_ref[:, o:o + a_qk])
    o += a_qk
    ai_o[...] = _dot(hb, wm_ref[:, o:o + a_v]).astype(BF16)
    o += a_v
    ag_o[...] = _dot(hb, wm_ref[:, o:o + a_v]).astype(BF16)
    o += a_v

    ca = ca_ref[...]
    sa = sa_ref[...]

    def rope_head(y):
        return y * ca + pltpu.roll(y, HEAD_DIM // 2, 1) * sa

    zq = _dot(hb, wm_ref[:, o:o + b_q])
    o += b_q
    for h in range(B_HEADS):
        sl = slice(h * HEAD_DIM, (h + 1) * HEAD_DIM)
        q_o[:, sl] = rope_head(_rms(zq[:, sl]) * qg_ref[...]).astype(BF16)
    zk = _dot(hb, wm_ref[:, o:o + b_kv])
    o += b_kv
    for h in range(B_KV_HEADS):
        sl = slice(h * HEAD_DIM, (h + 1) * HEAD_DIM)
        y = rope_head(_rms(zk[:, sl]) * kg_ref[...])
        kf_o[:, sl] = y
        kb_o[:, sl] = y.astype(BF16)
    zv = _dot(hb, wm_ref[:, o:o + b_kv])
    o += b_kv
    vf_o[...] = zv
    vb_o[...] = zv.astype(BF16)

    cb = cb_ref[...]
    sb = sb_ref[...]
    lane = lax.broadcasted_iota(I32, (tm, LANES), 1)
    first_half = (lane & (IDX_DIM - 1)) < (IDX_DIM // 2)

    def rope_idx(y):
        partner = jnp.where(first_half, pltpu.roll(y, LANES - IDX_DIM // 2, 1),
                            pltpu.roll(y, IDX_DIM // 2, 1))
        return y * cb + partner * sb

    zi = _dot(hb, wm_ref[:, o:o + iq_w])
    for p in range(iq_w // LANES):
        y = rope_idx(zi[:, p * LANES:(p + 1) * LANES])
        qim_o[:, (2 * p) * LANES:(2 * p + 1) * LANES] = jnp.where(lane < IDX_DIM, y, 0.0).astype(BF16)
        qim_o[:, (2 * p + 1) * LANES:(2 * p + 2) * LANES] = jnp.where(lane >= IDX_DIM, y, 0.0).astype(BF16)
    zs = _dot(hb, ws_ref[...])
    y = rope_idx(zs[:, :LANES])
    ki2_o[...] = y.astype(BF16)
    kif_o[...] = y[:, :IDX_DIM]
    wi_o[...] = zs[:, LANES:] * (IDX_HEADS ** -0.5 * IDX_DIM ** -0.5)
    zg = _dot(hb, wg_ref[...])
    d = zg.shape[1] // 2
    ga_o[...] = zg[:, :d].astype(BF16)
    gb_o[...] = zg[:, d:].astype(BF16)


def _inproj(x2, g1, wm, ws, wg, qg, kg, tabs, tm):
    r, d = x2.shape
    ca, sa, cb, sb = tabs
    npos = ca.shape[0] // tm
    row = lambda w: pl.BlockSpec((tm, w), lambda i: (i, 0))
    tab = pl.BlockSpec((tm, LANES), lambda i: (i % npos, 0))
    a_qk = A_HEADS * A_DK
    b_q = B_HEADS * HEAD_DIM
    b_kv = B_KV_HEADS * HEAD_DIM
    outs = [
        (a_qk, BF16), (a_qk, F32), (a_qk, BF16), (a_qk, BF16),
        (b_q, BF16), (b_kv, F32), (b_kv, BF16), (b_kv, F32), (b_kv, BF16),
        (IDX_HEADS * LANES, BF16), (IDX_DIM, F32), (LANES, BF16), (LANES, F32),
        (d, BF16), (d, BF16),
    ]
    return pl.pallas_call(
        _inproj_body,
        grid=(r // tm,),
        in_specs=[row(d), _const_spec(g1.shape), _const_spec(wm.shape), _const_spec(ws.shape),
                  _const_spec(wg.shape), _const_spec(qg.shape), _const_spec(kg.shape),
                  tab, tab, tab, tab],
        out_specs=[row(w) for w, _ in outs],
        out_shape=[jax.ShapeDtypeStruct((r, w), dt) for w, dt in outs],
        compiler_params=pltpu.CompilerParams(dimension_semantics=("arbitrary",),
                                             vmem_limit_bytes=VMEM_LIMIT),
        name="inproj",
    )(x2, g1, wm, ws, wg, qg, kg, ca, sa, cb, sb)


def _hgrn_body(aq_ref, af_ref, ai_ref, ag_ref, s0_ref, lbp_ref, ng_ref, oa_ref, sn_ref, st_ref,
               *, layer):
    c = pl.program_id(1)
    nc = pl.num_programs(1)

    @pl.when(c == 0)
    def _():
        for h in range(A_HEADS):
            st_ref[h] = s0_ref[0, h].T

    lbp = lbp_ref[...]
    e = jnp.exp(lbp - jnp.max(lbp, axis=0, keepdims=True))
    sm = e / jnp.sum(e, axis=0, keepdims=True)
    lb = jnp.sum(sm[:layer + 1], axis=0, keepdims=True)

    f = lb + (1.0 - lb) * jax.nn.sigmoid(af_ref[...])
    lf = jnp.log(f)
    n = lf.shape[0]
    r_i = lax.broadcasted_iota(I32, (n, n), 0)
    c_i = lax.broadcasted_iota(I32, (n, n), 1)
    causal = r_i >= c_i
    tri = jnp.where(causal, 1.0, 0.0).astype(BF16)
    hi = lf.astype(BF16)
    r1 = lf - hi.astype(F32)
    mid = r1.astype(BF16)
    lo = (r1 - mid.astype(F32)).astype(BF16)
    cum = _dot(tri, hi) + _dot(tri, mid) + _dot(tri, lo)
    last = cum[n - 1:n, :]
    qd = _silu(aq_ref[...].astype(F32)) * jnp.exp(cum)
    k = 1.0 - f
    kd = k * jnp.exp(-cum)
    kt = k * jnp.exp(last - cum)
    el = jnp.exp(last)
    gate = _silu(ag_ref[...].astype(F32))
    for h in range(A_HEADS):
        sl = slice(h * A_DK, (h + 1) * A_DK)
        qd_h = qd[:, sl].astype(BF16)
        v_h = ai_ref[:, sl]
        att = jnp.where(causal, _dot_nt(qd_h, kd[:, sl].astype(BF16)), 0.0)
        st = st_ref[h]
        o = _dot(att.astype(BF16), v_h) + _dot_nt(qd_h, st.astype(BF16))
        st_ref[h] = st * el[:, sl] + _dot_tn(v_h, kt[:, sl].astype(BF16))
        oa_ref[:, sl] = (_rms(o) * ng_ref[...] * gate[:, sl]).astype(BF16)

    @pl.when(c == nc - 1)
    def _():
        for h in range(A_HEADS):
            sn_ref[0, h] = st_ref[h].T


def _hgrn(aq, af, ai, ag, s0, lbp, ng, bsz, t, layer):
    d = aq.shape[1]
    nc = t // CHUNK
    row = pl.BlockSpec((CHUNK, d), lambda b, c: (b * nc + c, 0))
    st = pl.BlockSpec((1, A_HEADS, A_DK, A_DV), lambda b, c: (b, 0, 0, 0))
    return pl.pallas_call(
        functools.partial(_hgrn_body, layer=layer),
        grid=(bsz, nc),
        in_specs=[row, row, row, row, st, _const_spec(lbp.shape), _const_spec(ng.shape)],
        out_specs=[row, st],
        out_shape=[jax.ShapeDtypeStruct((bsz * t, d), BF16),
                   jax.ShapeDtypeStruct((bsz, A_HEADS, A_DK, A_DV), F32)],
        scratch_shapes=[pltpu.VMEM((A_HEADS, A_DV, A_DK), F32)],
        compiler_params=pltpu.CompilerParams(dimension_semantics=("arbitrary", "arbitrary"),
                                             vmem_limit_bytes=VMEM_LIMIT),
        name="hgrn",
    )(aq, af, ai, ag, s0, lbp, ng)


def _chunk_of(pos):
    return jnp.right_shift(pos, CHUNK.bit_length() - 1)


def _bcast_cols(x, tk):
    return jnp.concatenate([x] * (tk // LANES), axis=1)


def _score_tile(qim_ref, wb_ref, ki2_tile, tq, tk):
    acc = jnp.zeros((tq, tk), F32)
    for h in range(IDX_HEADS):
        s = _dot_nt(qim_ref[:, h * LANES:(h + 1) * LANES], ki2_tile)
        acc = acc + jnp.maximum(s, 0.0) * _bcast_cols(wb_ref[h], tk)
    return acc


def _sort_key(score, adm):
    score = jnp.where(score == 0.0, 0.0, score)
    bits = lax.bitcast_convert_type(score, I32)
    key = jnp.where(bits < 0, bits ^ 0x7FFFFFFF, bits)
    if adm is None:
        return key
    return jnp.where(adm, key, INT_MIN)


def _select(sc_ref, bias_ref, cut_ref, nkt, krow, tq, tk, nbits):
    ncol = tk // LANES

    def count(pred_fn):
        def body(j, cnt):
            t = sc_ref[j]
            for cc in range(ncol):
                cnt = cnt + pred_fn(t[:, cc * LANES:(cc + 1) * LANES], j, cc)
            return cnt
        cnt = lax.fori_loop(0, nkt, body, jnp.zeros((tq, LANES), F32))
        return jnp.sum(cnt, axis=1, keepdims=True)

    def count_ge(cand):
        return count(lambda t, j, cc: jnp.where(t >= cand, 1.0, 0.0))

    def bit_body(i, tu):
        cand_u = tu | jnp.left_shift(jnp.int32(1), 31 - i)
        ok = count_ge(cand_u ^ INT_MIN) >= krow
        return jnp.where(ok, cand_u, tu)

    tu = lax.fori_loop(0, 32, bit_body, jnp.zeros((tq, LANES), I32))
    thr = tu ^ INT_MIN
    n_ge = count_ge(thr)
    n_gt = count_ge(thr + 1)
    need = krow - n_gt

    lane = lax.broadcasted_iota(I32, (tq, LANES), 1)
    cut_ref[...] = jnp.full((tq, LANES), 2 ** 31 - 1, I32)
    extra = jnp.max(jnp.where(n_ge > krow, 1.0, 0.0))

    @pl.when(extra > 0.0)
    def _():
        def idx_body(i, c):
            cand = c | jnp.left_shift(jnp.int32(1), nbits - 1 - i)
            below = count(lambda t, j, cc: jnp.where(
                t == thr, jnp.where(j * tk + cc * LANES + lane < cand, 1.0, 0.0), 0.0))
            return jnp.where(below < need, cand, c)
        cut_ref[...] = lax.fori_loop(0, nbits, idx_body, jnp.zeros((tq, LANES), I32))

    cut = cut_ref[...]

    def write(j, carry):
        t = sc_ref[j]
        for cc in range(ncol):
            tc = t[:, cc * LANES:(cc + 1) * LANES]
            idx = j * tk + cc * LANES + lane
            tie = jnp.where(tc == thr, jnp.where(idx <= cut, 0.0, NEG), NEG)
            bias_ref[j, :, cc * LANES:(cc + 1) * LANES] = jnp.where(tc > thr, 0.0, tie)
        return carry

    lax.fori_loop(0, nkt, write, 0)


def _attn_update(qg, kt, vt, bias, m, l, acc, scale):
    s = _dot_nt(qg, kt) * scale + jnp.concatenate([bias, bias], axis=0)
    m_new = jnp.maximum(m, jnp.max(s, axis=1, keepdims=True))
    alpha = jnp.exp(m - m_new)
    p = jnp.exp(s - m_new)
    l_new = alpha * l + jnp.sum(p, axis=1, keepdims=True)
    acc_new = alpha * acc + _dot(p.astype(BF16), vt)
    return m_new, l_new, acc_new


def _stack_heads(q_ref, g):
    rep = B_HEADS // B_KV_HEADS
    return jnp.concatenate(
        [q_ref[:, (g * rep + r) * HEAD_DIM:(g * rep + r + 1) * HEAD_DIM] for r in range(rep)], axis=0)


def _store_heads(o_ref, g, o, tq):
    rep = B_HEADS // B_KV_HEADS
    for r in range(rep):
        h = g * rep + r
        o_ref[:, h * HEAD_DIM:(h + 1) * HEAD_DIM] = o[r * tq:(r + 1) * tq].astype(o_ref.dtype)


def _wi_broadcast(wi_ref, wb_ref, tq):
    wi = wi_ref[...]
    for h in range(IDX_HEADS):
        wb_ref[h] = jnp.broadcast_to(wi[:, h:h + 1], (tq, LANES))


def _dsa_prompt_body(q_ref, qim_ref, wi_ref, kb_ref, vb_ref, ki2_ref, o_ref, wb_ref, sc_ref,
                     bias_ref, cut_ref, *, tq, tk, topk, nbits, scale):
    q0 = pl.program_id(1) * tq
    nkt = (q0 + tq + tk - 1) // tk
    rep = B_HEADS // B_KV_HEADS
    _wi_broadcast(wi_ref, wb_ref, tq)
    qchunk = _chunk_of(q0 + lax.broadcasted_iota(I32, (tq, tk), 0))

    def score_body(j, carry):
        ks = pl.multiple_of(j * tk, tk)
        score = _score_tile(qim_ref, wb_ref, ki2_ref[pl.ds(ks, tk), :], tq, tk)
        kpos = ks + lax.broadcasted_iota(I32, (tq, tk), 1)
        sc_ref[j] = _sort_key(score, _chunk_of(kpos) <= qchunk)
        return carry

    lax.fori_loop(0, nkt, score_body, 0)

    n_adm = (qchunk[:, :1] + 1) * CHUNK
    krow = jnp.minimum(n_adm, topk).astype(F32)
    _select(sc_ref, bias_ref, cut_ref, nkt, krow, tq, tk, nbits)

    for g in range(B_KV_HEADS):
        qg = _stack_heads(q_ref, g)
        gs = slice(g * HEAD_DIM, (g + 1) * HEAD_DIM)

        def att_body(j, carry):
            ks = pl.multiple_of(j * tk, tk)
            return _attn_update(qg, kb_ref[pl.ds(ks, tk), gs], vb_ref[pl.ds(ks, tk), gs],
                                bias_ref[j], *carry, scale)

        init = (jnp.full((rep * tq, 1), NEG, F32), jnp.zeros((rep * tq, 1), F32),
                jnp.zeros((rep * tq, HEAD_DIM), F32))
        _, l, acc = lax.fori_loop(0, nkt, att_body, init)
        _store_heads(o_ref, g, acc / l, tq)


def _dsa_prompt(q, qim, wi, kb, vb, ki2, bsz, t, tq, tk):
    nq = t // tq
    topk = min(IDX_TOPK_MAX, t // 4)
    nbits = max(1, (t - 1).bit_length())
    qrow = lambda w: pl.BlockSpec((tq, w), lambda b, i: (b * nq + i, 0))
    seq = lambda w: pl.BlockSpec((t, w), lambda b, i: (b, 0))
    nt = t // tk
    return pl.pallas_call(
        functools.partial(_dsa_prompt_body, tq=tq, tk=tk, topk=topk, nbits=nbits,
                          scale=HEAD_DIM ** -0.5),
        grid=(bsz, nq),
        in_specs=[qrow(q.shape[1]), qrow(qim.shape[1]), qrow(LANES),
                  seq(kb.shape[1]), seq(vb.shape[1]), seq(LANES)],
        out_specs=qrow(q.shape[1]),
        out_shape=jax.ShapeDtypeStruct(q.shape, BF16),
        scratch_shapes=[pltpu.VMEM((IDX_HEADS, tq, LANES), F32),
                        pltpu.VMEM((nt, tq, tk), I32),
                        pltpu.VMEM((nt, tq, tk), F32),
                        pltpu.VMEM((tq, LANES), I32)],
        compiler_params=pltpu.CompilerParams(dimension_semantics=("arbitrary", "arbitrary"),
                                             vmem_limit_bytes=VMEM_LIMIT),
        name="dsa_prompt",
    )(q, qim, wi, kb, vb, ki2)


def _dsa_sample_body(q_ref, qim_ref, wi_ref, kn_ref, vn_ref, ki2n_ref, kidx2_ref, ck_ref, cv_ref,
                     o_ref, wb_ref, sc_ref, bias_ref, cut_ref, m_ref, l_ref, acc_ref,
                     *, tq, tk, nsub, npt, past, topk, nbits, scale):
    j = pl.program_id(1)
    nj = pl.num_programs(1)
    rep = B_HEADS // B_KV_HEADS
    pad = jnp.zeros((tk - tq, LANES), BF16)

    @pl.when(j == 0)
    def _():
        _wi_broadcast(wi_ref, wb_ref, tq)

        def score_body(jt, carry):
            ks = pl.multiple_of(jt * tk, tk)
            sc_ref[jt] = _sort_key(_score_tile(qim_ref, wb_ref, kidx2_ref[pl.ds(ks, tk), :], tq, tk),
                                   None)
            return carry

        lax.fori_loop(0, npt, score_body, 0)
        ki2n = jnp.concatenate([ki2n_ref[...], pad], axis=0)
        col = lax.broadcasted_iota(I32, (tq, tk), 1)
        row = lax.broadcasted_iota(I32, (tq, tk), 0)
        adm = jnp.where(col < tq, _chunk_of(past + col), 2 ** 30) <= _chunk_of(past + row)
        sc_ref[npt] = _sort_key(_score_tile(qim_ref, wb_ref, ki2n, tq, tk), adm)
        n_adm = (_chunk_of(past + row[:, :1]) + 1) * CHUNK
        krow = jnp.minimum(n_adm, topk).astype(F32)
        _select(sc_ref, bias_ref, cut_ref, npt + 1, krow, tq, tk, nbits)
        m_ref[...] = jnp.full(m_ref.shape, NEG, F32)
        l_ref[...] = jnp.zeros(l_ref.shape, F32)
        acc_ref[...] = jnp.zeros(acc_ref.shape, F32)

    def update(g, kt, vt, bias):
        qg = _stack_heads(q_ref, g)
        m, l, acc = _attn_update(qg, kt, vt, bias, m_ref[g], l_ref[g], acc_ref[g], scale)
        m_ref[g] = m
        l_ref[g] = l
        acc_ref[g] = acc

    for su in range(nsub):
        rs = slice(su * tk, (su + 1) * tk)
        bias = bias_ref[j * nsub + su]
        for g in range(B_KV_HEADS):
            gs = slice(g * HEAD_DIM, (g + 1) * HEAD_DIM)
            update(g, ck_ref[rs, gs].astype(BF16), cv_ref[rs, gs].astype(BF16), bias)

    @pl.when(j == nj - 1)
    def _():
        bias = bias_ref[npt]
        for g in range(B_KV_HEADS):
            gs = slice(g * HEAD_DIM, (g + 1) * HEAD_DIM)
            kt = jnp.concatenate([kn_ref[:, gs], pad], axis=0)
            vt = jnp.concatenate([vn_ref[:, gs], pad], axis=0)
            update(g, kt, vt, bias)
            _store_heads(o_ref, g, acc_ref[g] / l_ref[g], tq)


def _dsa_sample(q, qim, wi, kn, vn, ki2n, kidx2, ck, cv, bsz, t, past, tk, nsub):
    tq = t
    npt = past // tk
    nj = npt // nsub
    topk = min(IDX_TOPK_MAX, (past + t) // 4)
    nbits = max(1, (past + t - 1).bit_length())
    rep = B_HEADS // B_KV_HEADS
    qrow = lambda w: pl.BlockSpec((tq, w), lambda b, j: (b, 0))
    kvw = ck.shape[1]
    cache = pl.BlockSpec((nsub * tk, kvw), lambda b, j: (b * nj + j, 0))
    return pl.pallas_call(
        functools.partial(_dsa_sample_body, tq=tq, tk=tk, nsub=nsub, npt=npt, past=past,
                          topk=topk, nbits=nbits, scale=HEAD_DIM ** -0.5),
        grid=(bsz, nj),
        in_specs=[qrow(q.shape[1]), qrow(qim.shape[1]), qrow(LANES), qrow(kvw), qrow(kvw),
                  qrow(LANES), pl.BlockSpec((past, LANES), lambda b, j: (b, 0)), cache, cache],
        out_specs=qrow(q.shape[1]),
        out_shape=jax.ShapeDtypeStruct(q.shape, BF16),
        scratch_shapes=[pltpu.VMEM((IDX_HEADS, tq, LANES), F32),
                        pltpu.VMEM((npt + 1, tq, tk), I32),
                        pltpu.VMEM((npt + 1, tq, tk), F32),
                        pltpu.VMEM((tq, LANES), I32),
                        pltpu.VMEM((B_KV_HEADS, rep * tq, 1), F32),
                        pltpu.VMEM((B_KV_HEADS, rep * tq, 1), F32),
                        pltpu.VMEM((B_KV_HEADS, rep * tq, HEAD_DIM), F32)],
        compiler_params=pltpu.CompilerParams(dimension_semantics=("arbitrary", "arbitrary"),
                                             vmem_limit_bytes=VMEM_LIMIT),
        name="dsa_sample",
    )(q, qim, wi, kn, vn, ki2n, kidx2, ck, cv)


def _post_body(oa_ref, ob_ref, ga_ref, gb_ref, x_ref, wa_ref, wb_ref, wo_ref, g2_ref, wr_ref,
               br_ref, x1_o, t_o, comb_o):
    ya = _dot(oa_ref[...], wa_ref[...])
    yb = _dot(ob_ref[...], wb_ref[...])
    merged = (jax.nn.sigmoid(ga_ref[...].astype(F32)) * ya
              + jax.nn.sigmoid(gb_ref[...].astype(F32)) * yb)
    x1 = x_ref[...] + _dot(merged.astype(BF16), wo_ref[...])
    x1_o[...] = x1
    tok = _rms(x1) * g2_ref[...]
    tb = tok.astype(BF16)
    t_o[...] = tb
    logits = _dot(tb, wr_ref[...]) + br_ref[...]
    tm = logits.shape[0]
    lane = lax.broadcasted_iota(I32, (tm, LANES), 1).astype(F32)
    cur = jnp.where(lane < N_EXPERTS, logits, -jnp.inf)
    top = None
    den = jnp.zeros((tm, 1), F32)
    comb = jnp.zeros((tm, LANES), F32)
    for k in range(TOP_K):
        mx = jnp.max(cur, axis=1, keepdims=True)
        first = jnp.min(jnp.where(cur == mx, lane, float(LANES)), axis=1, keepdims=True)
        hit = lane == first
        if top is None:
            top = mx
        e = jnp.exp(mx - top)
        den = den + e
        comb = comb + jnp.where(hit, e, 0.0)
        cur = jnp.where(hit, -jnp.inf, cur)
    comb_o[...] = comb / den


def _post(oa, ob, ga, gb, x2, wa, wb, wo, g2, wr, br, tm):
    r, d = x2.shape
    row = lambda w: pl.BlockSpec((tm, w), lambda i: (i, 0))
    return pl.pallas_call(
        _post_body,
        grid=(r // tm,),
        in_specs=[row(d), row(d), row(d), row(d), row(d), _const_spec(wa.shape),
                  _const_spec(wb.shape), _const_spec(wo.shape), _const_spec(g2.shape),
                  _const_spec(wr.shape), _const_spec(br.shape)],
        out_specs=[row(d), row(d), row(LANES)],
        out_shape=[jax.ShapeDtypeStruct((r, d), F32), jax.ShapeDtypeStruct((r, d), BF16),
                   jax.ShapeDtypeStruct((r, LANES), F32)],
        compiler_params=pltpu.CompilerParams(dimension_semantics=("arbitrary",),
                                             vmem_limit_bytes=VMEM_LIMIT),
        name="post",
    )(oa, ob, ga, gb, x2, wa, wb, wo, g2, wr, br)


def _moe_body(t_ref, comb_ref, x1_ref, wgu_ref, bgu_ref, wd_ref, bd_ref, out_ref):
    e = pl.program_id(1)

    @pl.when(e == 0)
    def _():
        out_ref[...] = jnp.zeros(out_ref.shape, F32)

    gu = _dot(t_ref[...], wgu_ref[0]) + bgu_ref[0]
    dff = gu.shape[1] // 2
    gate = jnp.minimum(gu[:, :dff], SWIGLU_LIMIT)
    up = jnp.clip(gu[:, dff:], -SWIGLU_LIMIT, SWIGLU_LIMIT)
    act = (up + 1.0) * gate * jax.nn.sigmoid(SWIGLU_ALPHA * gate)
    y = _dot(act.astype(BF16), wd_ref[0]) + bd_ref[0]
    comb = comb_ref[...]
    lane = lax.broadcasted_iota(I32, comb.shape, 1)
    c = jnp.sum(jnp.where(lane == e, comb, 0.0), axis=1, keepdims=True)
    out_ref[...] += c * y

    @pl.when(e == pl.num_programs(1) - 1)
    def _():
        out_ref[...] += x1_ref[...]


def _moe(tok, comb, x1, wgu, bgu, wd, bd, tm):
    r, d = tok.shape
    ne, _, dff2 = wgu.shape
    row = lambda w: pl.BlockSpec((tm, w), lambda i, e: (i, 0))
    return pl.pallas_call(
        _moe_body,
        grid=(r // tm, ne),
        in_specs=[row(d), row(LANES), row(d),
                  pl.BlockSpec((1, d, dff2), lambda i, e: (e, 0, 0)),
                  pl.BlockSpec((1, 1, dff2), lambda i, e: (e, 0, 0)),
                  pl.BlockSpec((1, dff2 // 2, d), lambda i, e: (e, 0, 0)),
                  pl.BlockSpec((1, 1, d), lambda i, e: (e, 0, 0))],
        out_specs=row(d),
        out_shape=jax.ShapeDtypeStruct((r, d), F32),
        compiler_params=pltpu.CompilerParams(dimension_semantics=("arbitrary", "arbitrary"),
                                             vmem_limit_bytes=VMEM_LIMIT),
        name="moe",
    )(tok, comb, x1, wgu, bgu, wd, bd)


def _row_tile(r, want):
    tm = min(r, want)
    assert r % tm == 0, (r, tm)
    return tm


def _mixers(x2, bsz, t, pos, s0, caches, wts, layer):
    (g1, wm, ws, wg, qg, kg, lbp, ng) = wts
    r = x2.shape[0]
    tm = _row_tile(r, 256)
    assert t % tm == 0 or tm % t == 0
    tabs_a = _rope_tables(pos, HEAD_DIM, 1)
    tabs_b = _rope_tables(pos, IDX_DIM, LANES // IDX_DIM)
    tabs = tabs_a + tabs_b
    if tm > t:
        tabs = tuple(jnp.tile(tb, (tm // t, 1)) for tb in tabs)
    (aq, af, ai, ag, q, kf, kb, vf, vb, qim, kif, ki2, wi, ga, gb) = _inproj(
        x2, g1, wm, ws, wg, qg, kg, tabs, tm)
    oa, s_new = _hgrn(aq, af, ai, ag, s0, lbp, ng, bsz, t, layer)
    if caches is None:
        tq = _row_tile(t, 128)
        tk = _row_tile(t, 512)
        ob = _dsa_prompt(q, qim, wi, kb, vb, ki2, bsz, t, tq, tk)
    else:
        ck, cv, kidx2, past = caches
        tk = _row_tile(past, 512)
        nsub = 2 if (past // tk) % 2 == 0 else 1
        ob = _dsa_sample(q, qim, wi, kb, vb, ki2, kidx2, ck, cv, bsz, t, past, tk, nsub)
    return oa, ob, ga, gb, kf, vf, kif, s_new


def kernel(x_prompt, x_sample, cache_k, cache_v, cache_kidx, state_hgrn, norm1_g, w_in, lower_bounds, hgrn_norm_g, q_norm_g, k_norm_g, w_branch_a, w_branch_b, w_out, norm2_g, w_router, b_router, w_gate_up, b_gate_up, w_down, b_down):
    bp, tp, d = x_prompt.shape
    bs, ts, _ = x_sample.shape
    depth = w_in.shape[0]
    past = cache_k.shape[2]
    kvw = B_KV_HEADS * HEAD_DIM
    pos_p = jnp.arange(tp, dtype=I32)
    pos_s = past + jnp.arange(ts, dtype=I32)
    xp = x_prompt.reshape(bp * tp, d)
    xs = x_sample.reshape(bs * ts, d)
    n_main = 2 * A_HEADS * A_DK + 2 * A_HEADS * A_DV + B_HEADS * HEAD_DIM + 2 * kvw + IDX_HEADS * IDX_DIM
    n_small = n_main + IDX_DIM + IDX_HEADS
    outs = [[] for _ in range(8)]
    for l in range(depth):
        w = w_in[l]
        wm = w[:, :n_main].astype(BF16)
        w_ik = w[:, n_main:n_main + IDX_DIM]
        w_iw = w[:, n_main + IDX_DIM:n_small]
        ws = jnp.concatenate(
            [w_ik, w_ik, w_iw, jnp.zeros((d, LANES - IDX_HEADS), w.dtype)], axis=1).astype(BF16)
        wg = w[:, n_small:].astype(BF16)
        wts = (norm1_g[l][None], wm, ws, wg, q_norm_g[l][None], k_norm_g[l][None],
               lower_bounds, hgrn_norm_g[l][None])
        s0_p = jnp.zeros((bp, A_HEADS, A_DK, A_DV), F32)
        oa_p, ob_p, ga_p, gb_p, kp, vp, kip, sp = _mixers(xp, bp, tp, pos_p, s0_p, None, wts, l)
        kidx2 = jnp.concatenate([cache_kidx[l], cache_kidx[l]], axis=-1).astype(BF16)
        caches = (cache_k[l].reshape(bs * past, kvw), cache_v[l].reshape(bs * past, kvw),
                  kidx2.reshape(bs * past, LANES), past)
        oa_s, ob_s, ga_s, gb_s, ks, vs, kis, ss = _mixers(
            xs, bs, ts, pos_s, state_hgrn[l], caches, wts, l)

        wa = w_branch_a[l].astype(BF16)
        wb = w_branch_b[l].astype(BF16)
        wo = w_out[l].astype(BF16)
        wr = jnp.pad(w_router[l], ((0, 0), (0, LANES - N_EXPERTS))).astype(BF16)
        br = jnp.pad(b_router[l], (0, LANES - N_EXPERTS))[None]
        g2 = norm2_g[l][None]
        wgu = w_gate_up[l].astype(BF16)
        wd = w_down[l].astype(BF16)
        bgu = b_gate_up[l][:, None, :]
        bd = b_down[l][:, None, :]
        new = []
        for (x2, oa, ob, ga, gb) in ((xp, oa_p, ob_p, ga_p, gb_p), (xs, oa_s, ob_s, ga_s, gb_s)):
            r = x2.shape[0]
            x1, tok, comb = _post(oa, ob, ga, gb, x2, wa, wb, wo, g2, wr, br, _row_tile(r, 256))
            new.append(_moe(tok, comb, x1, wgu, bgu, wd, bd, _row_tile(r, 512)))
        xp, xs = new
        for lst, v in zip(outs, (kp.reshape(bp, tp, B_KV_HEADS, HEAD_DIM),
                                 vp.reshape(bp, tp, B_KV_HEADS, HEAD_DIM),
                                 kip.reshape(bp, tp, IDX_DIM), sp,
                                 ks.reshape(bs, ts, B_KV_HEADS, HEAD_DIM),
                                 vs.reshape(bs, ts, B_KV_HEADS, HEAD_DIM),
                                 kis.reshape(bs, ts, IDX_DIM), ss)):
            lst.append(v)
    return (xp.reshape(bp, tp, d), xs.reshape(bs, ts, d)) + tuple(jnp.stack(o) for o in outs)
```

```python
import functools

import jax
import jax.numpy as jnp
from jax import lax
from jax.experimental import pallas as pl
from jax.experimental.pallas import tpu as pltpu

F32 = jnp.float32
BF16 = jnp.bfloat16
I32 = jnp.int32

CHUNK = 64
A_HEADS = 8
A_DK = 128
A_DV = 128
B_HEADS = 8
B_KV_HEADS = 4
HEAD_DIM = 128
IDX_HEADS = 8
IDX_DIM = 64
IDX_TOPK_MAX = 256
ROPE_THETA = 10000.0
N_EXPERTS = 32
TOP_K = 4
SWIGLU_LIMIT = 7.0
SWIGLU_ALPHA = 1.702
EPS = 1e-6

LANES = 128
INT_MIN = -(2 ** 31)
NEG = -1e30
VMEM_LIMIT = 56 * 1024 * 1024


def _rms(x):
    return x * lax.rsqrt(jnp.mean(x * x, axis=-1, keepdims=True) + EPS)


def _silu(x):
    return x * jax.nn.sigmoid(x)


def _dot(a, b):
    return jnp.dot(a, b, preferred_element_type=F32)


def _dot_nt(a, b):
    return lax.dot_general(a, b, (((1,), (1,)), ((), ())), preferred_element_type=F32)


def _dot_tn(a, b):
    return lax.dot_general(a, b, (((0,), (0,)), ((), ())), preferred_element_type=F32)


def _const_spec(shape):
    zeros = (0,) * len(shape)
    return pl.BlockSpec(shape, lambda *_: zeros, pipeline_mode=pl.Buffered(1))


def _rope_tables(pos, d, reps):
    inv = 1.0 / (ROPE_THETA ** (jnp.arange(0, d, 2, dtype=F32) / d))
    ang = pos.astype(F32)[:, None] * inv[None, :]
    cos = jnp.cos(ang)
    sin = jnp.sin(ang)
    cos_t = jnp.concatenate([cos, cos] * reps, axis=-1)
    sin_t = jnp.concatenate([-sin, sin] * reps, axis=-1)
    return cos_t, sin_t


def _inproj_body(x_ref, g1_ref, wm_ref, ws_ref, wg_ref, qg_ref, kg_ref, ca_ref, sa_ref, cb_ref,
                 sb_ref, aq_o, af_o, ai_o, ag_o, q_o, kf_o, kb_o, vf_o, vb_o, qim_o, kif_o, ki2_o,
                 wi_o, ga_o, gb_o):
    x = x_ref[...]
    tm = x.shape[0]
    hb = (_rms(x) * g1_ref[...]).astype(BF16)
    a_qk = A_HEADS * A_DK
    a_v = A_HEADS * A_DV
    b_q = B_HEADS * HEAD_DIM
    b_kv = B_KV_HEADS * HEAD_DIM
    iq_w = IDX_HEADS * IDX_DIM
    o = 0
    aq_o[...] = _dot(hb, wm_ref[:, o:o + a_qk]).astype(BF16)
    o += a_qk
    af_o[...] = _dot(hb, wm_ref[:, o:o + a_qk])
    o += a_qk
    ai_o[...] = _dot(hb, wm_ref[:, o:o + a_v]).astype(BF16)
    o += a_v
    ag_o[...] = _dot(hb, wm_ref[:, o:o + a_v]).astype(BF16)
    o += a_v

    ca = ca_ref[...]
    sa = sa_ref[...]

    def rope_head(y):
        return y * ca + pltpu.roll(y, HEAD_DIM // 2, 1) * sa

    zq = _dot(hb, wm_ref[:, o:o + b_q])
    o += b_q
    for h in range(B_HEADS):
        sl = slice(h * HEAD_DIM, (h + 1) * HEAD_DIM)
        q_o[:, sl] = rope_head(_rms(zq[:, sl]) * qg_ref[...]).astype(BF16)
    zk = _dot(hb, wm_ref[:, o:o + b_kv])
    o += b_kv
    for h in range(B_KV_HEADS):
        sl = slice(h * HEAD_DIM, (h + 1) * HEAD_DIM)
        y = rope_head(_rms(zk[:, sl]) * kg_ref[...])
        kf_o[:, sl] = y
        kb_o[:, sl] = y.astype(BF16)
    zv = _dot(hb, wm_ref[:, o:o + b_kv])
    o += b_kv
    vf_o[...] = zv
    vb_o[...] = zv.astype(BF16)

    cb = cb_ref[...]
    sb = sb_ref[...]
    lane = lax.broadcasted_iota(I32, (tm, LANES), 1)
    first_half = (lane & (IDX_DIM - 1)) < (IDX_DIM // 2)

    def rope_idx(y):
        partner = jnp.where(first_half, pltpu.roll(y, LANES - IDX_DIM // 2, 1),
                            pltpu.roll(y, IDX_DIM // 2, 1))
        return y * cb + partner * sb

    zi = _dot(hb, wm_ref[:, o:o + iq_w])
    for p in range(iq_w // LANES):
        y = rope_idx(zi[:, p * LANES:(p + 1) * LANES])
        qim_o[:, (2 * p) * LANES:(2 * p + 1) * LANES] = jnp.where(lane < IDX_DIM, y, 0.0).astype(BF16)
        qim_o[:, (2 * p + 1) * LANES:(2 * p + 2) * LANES] = jnp.where(lane >= IDX_DIM, y, 0.0).astype(BF16)
    zs = _dot(hb, ws_ref[...])
    y = rope_idx(zs[:, :LANES])
    ki2_o[...] = y.astype(BF16)
    kif_o[...] = y[:, :IDX_DIM]
    wi_o[...] = zs[:, LANES:] * (IDX_HEADS ** -0.5 * IDX_DIM ** -0.5)
    zg = _dot(hb, wg_ref[...])
    d = zg.shape[1] // 2
    ga_o[...] = zg[:, :d].astype(BF16)
    gb_o[...] = zg[:, d:].astype(BF16)


def _inproj(x2, g1, wm, ws, wg, qg, kg, tabs, tm):
    r, d = x2.shape
    ca, sa, cb, sb = tabs
    npos = ca.shape[0] // tm
    row = lambda w: pl.BlockSpec((tm, w), lambda i: (i, 0))
    tab = pl.BlockSpec((tm, LANES), lambda i: (i % npos, 0))
    a_qk = A_HEADS * A_DK
    b_q = B_HEADS * HEAD_DIM
    b_kv = B_KV_HEADS * HEAD_DIM
    outs = [
        (a_qk, BF16), (a_qk, F32), (a_qk, BF16), (a_qk, BF16),
        (b_q, BF16), (b_kv, F32), (b_kv, BF16), (b_kv, F32), (b_kv, BF16),
        (IDX_HEADS * LANES, BF16), (IDX_DIM, F32), (LANES, BF16), (LANES, F32),
        (d, BF16), (d, BF16),
    ]
    return pl.pallas_call(
        _inproj_body,
        grid=(r // tm,),
        in_specs=[row(d), _const_spec(g1.shape), _const_spec(wm.shape), _const_spec(ws.shape),
                  _const_spec(wg.shape), _const_spec(qg.shape), _const_spec(kg.shape),
                  tab, tab, tab, tab],
        out_specs=[row(w) for w, _ in outs],
        out_shape=[jax.ShapeDtypeStruct((r, w), dt) for w, dt in outs],
        compiler_params=pltpu.CompilerParams(dimension_semantics=("arbitrary",),
                                             vmem_limit_bytes=VMEM_LIMIT),
        name="inproj",
    )(x2, g1, wm, ws, wg, qg, kg, ca, sa, cb, sb)


def _hgrn_body(aq_ref, af_ref, ai_ref, ag_ref, s0_ref, lbp_ref, ng_ref, oa_ref, sn_ref, st_ref,
               *, layer):
    c = pl.program_id(1)
    nc = pl.num_programs(1)

    @pl.when(c == 0)
    def _():
        for h in range(A_HEADS):
            st_ref[h] = s0_ref[0, h].T

    lbp = lbp_ref[...]
    e = jnp.exp(lbp - jnp.max(lbp, axis=0, keepdims=True))
    sm = e / jnp.sum(e, axis=0, keepdims=True)
    lb = jnp.sum(sm[:layer + 1], axis=0, keepdims=True)

    f = lb + (1.0 - lb) * jax.nn.sigmoid(af_ref[...])
    lf = jnp.log(f)
    n = lf.shape[0]
    r_i = lax.broadcasted_iota(I32, (n, n), 0)
    c_i = lax.broadcasted_iota(I32, (n, n), 1)
    causal = r_i >= c_i
    tri = jnp.where(causal, 1.0, 0.0).astype(BF16)
    hi = lf.astype(BF16)
    r1 = lf - hi.astype(F32)
    mid = r1.astype(BF16)
    lo = (r1 - mid.astype(F32)).astype(BF16)
    cum = _dot(tri, hi) + _dot(tri, mid) + _dot(tri, lo)
    last = cum[n - 1:n, :]
    qd = _silu(aq_ref[...].astype(F32)) * jnp.exp(cum)
    k = 1.0 - f
    kd = k * jnp.exp(-cum)
    kt = k * jnp.exp(last - cum)
    el = jnp.exp(last)
    gate = _silu(ag_ref[...].astype(F32))
    for h in range(A_HEADS):
        sl = slice(h * A_DK, (h + 1) * A_DK)
        qd_h = qd[:, sl].astype(BF16)
        v_h = ai_ref[:, sl]
        att = jnp.where(causal, _dot_nt(qd_h, kd[:, sl].astype(BF16)), 0.0)
        st = st_ref[h]
        o = _dot(att.astype(BF16), v_h) + _dot_nt(qd_h, st.astype(BF16))
        st_ref[h] = st * el[:, sl] + _dot_tn(v_h, kt[:, sl].astype(BF16))
        oa_ref[:, sl] = (_rms(o) * ng_ref[...] * gate[:, sl]).astype(BF16)

    @pl.when(c == nc - 1)
    def _():
        for h in range(A_HEADS):
            sn_ref[0, h] = st_ref[h].T


def _hgrn(aq, af, ai, ag, s0, lbp, ng, bsz, t, layer):
    d = aq.shape[1]
    nc = t // CHUNK
    row = pl.BlockSpec((CHUNK, d), lambda b, c: (b * nc + c, 0))
    st = pl.BlockSpec((1, A_HEADS, A_DK, A_DV), lambda b, c: (b, 0, 0, 0))
    return pl.pallas_call(
        functools.partial(_hgrn_body, layer=layer),
        grid=(bsz, nc),
        in_specs=[row, row, row, row, st, _const_spec(lbp.shape), _const_spec(ng.shape)],
        out_specs=[row, st],
        out_shape=[jax.ShapeDtypeStruct((bsz * t, d), BF16),
                   jax.ShapeDtypeStruct((bsz, A_HEADS, A_DK, A_DV), F32)],
        scratch_shapes=[pltpu.VMEM((A_HEADS, A_DV, A_DK), F32)],
        compiler_params=pltpu.CompilerParams(dimension_semantics=("arbitrary", "arbitrary"),
                                             vmem_limit_bytes=VMEM_LIMIT),
        name="hgrn",
    )(aq, af, ai, ag, s0, lbp, ng)


def _chunk_of(pos):
    return jnp.right_shift(pos, CHUNK.bit_length() - 1)


def _bcast_cols(x, tk):
    return jnp.concatenate([x] * (tk // LANES), axis=1)


def _score_tile(qim_ref, wb_ref, ki2_tile, tq, tk):
    acc = jnp.zeros((tq, tk), F32)
    for h in range(IDX_HEADS):
        s = _dot_nt(qim_ref[:, h * LANES:(h + 1) * LANES], ki2_tile)
        acc = acc + jnp.maximum(s, 0.0) * _bcast_cols(wb_ref[h], tk)
    return acc


def _sort_key(score, adm):
    score = jnp.where(score == 0.0, 0.0, score)
    bits = lax.bitcast_convert_type(score, I32)
    key = jnp.where(bits < 0, bits ^ 0x7FFFFFFF, bits)
    if adm is None:
        return key
    return jnp.where(adm, key, INT_MIN)


def _select(sc_ref, bias_ref, cut_ref, nkt, krow, tq, tk, nbits):
    ncol = tk // LANES

    def count(pred_fn):
        def body(j, cnt):
            t = sc_ref[j]
            for cc in range(ncol):
                cnt = cnt + pred_fn(t[:, cc * LANES:(cc + 1) * LANES], j, cc)
            return cnt
        cnt = lax.fori_loop(0, nkt, body, jnp.zeros((tq, LANES), F32))
        return jnp.sum(cnt, axis=1, keepdims=True)

    def count_ge(cand):
        return count(lambda t, j, cc: jnp.where(t >= cand, 1.0, 0.0))

    def bit_body(i, tu):
        cand_u = tu | jnp.left_shift(jnp.int32(1), 31 - i)
        ok = count_ge(cand_u ^ INT_MIN) >= krow
        return jnp.where(ok, cand_u, tu)

    tu = lax.fori_loop(0, 32, bit_body, jnp.zeros((tq, LANES), I32))
    thr = tu ^ INT_MIN
    n_ge = count_ge(thr)
    n_gt = count_ge(thr + 1)
    need = krow - n_gt

    lane = lax.broadcasted_iota(I32, (tq, LANES), 1)
    cut_ref[...] = jnp.full((tq, LANES), 2 ** 31 - 1, I32)
    extra = jnp.max(jnp.where(n_ge > krow, 1.0, 0.0))

    @pl.when(extra > 0.0)
    def _():
        def idx_body(i, c):
            cand = c | jnp.left_shift(jnp.int32(1), nbits - 1 - i)
            below = count(lambda t, j, cc: jnp.where(
                t == thr, jnp.where(j * tk + cc * LANES + lane < cand, 1.0, 0.0), 0.0))
            return jnp.where(below < need, cand, c)
        cut_ref[...] = lax.fori_loop(0, nbits, idx_body, jnp.zeros((tq, LANES), I32))

    cut = cut_ref[...]

    def write(j, carry):
        t = sc_ref[j]
        for cc in range(ncol):
            tc = t[:, cc * LANES:(cc + 1) * LANES]
            idx = j * tk + cc * LANES + lane
            tie = jnp.where(tc == thr, jnp.where(idx <= cut, 0.0, NEG), NEG)
            bias_ref[j, :, cc * LANES:(cc + 1) * LANES] = jnp.where(tc > thr, 0.0, tie)
        return carry

    lax.fori_loop(0, nkt, write, 0)


def _attn_update(qg, kt, vt, bias, m, l, acc, scale):
    s = _dot_nt(qg, kt) * scale + jnp.concatenate([bias, bias], axis=0)
    m_new = jnp.maximum(m, jnp.max(s, axis=1, keepdims=True))
    alpha = jnp.exp(m - m_new)
    p = jnp.exp(s - m_new)
    l_new = alpha * l + jnp.sum(p, axis=1, keepdims=True)
    acc_new = alpha * acc + _dot(p.astype(BF16), vt)
    return m_new, l_new, acc_new


def _stack_heads(q_ref, g):
    rep = B_HEADS // B_KV_HEADS
    return jnp.concatenate(
        [q_ref[:, (g * rep + r) * HEAD_DIM:(g * rep + r + 1) * HEAD_DIM] for r in range(rep)], axis=0)


def _store_heads(o_ref, g, o, tq):
    rep = B_HEADS // B_KV_HEADS
    for r in range(rep):
        h = g * rep + r
        o_ref[:, h * HEAD_DIM:(h + 1) * HEAD_DIM] = o[r * tq:(r + 1) * tq].astype(o_ref.dtype)


def _wi_broadcast(wi_ref, wb_ref, tq):
    wi = wi_ref[...]
    for h in range(IDX_HEADS):
        wb_ref[h] = jnp.broadcast_to(wi[:, h:h + 1], (tq, LANES))


def _dsa_prompt_body(q_ref, qim_ref, wi_ref, kb_ref, vb_ref, ki2_ref, o_ref, wb_ref, sc_ref,
                     bias_ref, cut_ref, *, tq, tk, topk, nbits, scale):
    q0 = pl.program_id(1) * tq
    nkt = (q0 + tq + tk - 1) // tk
    rep = B_HEADS // B_KV_HEADS
    _wi_broadcast(wi_ref, wb_ref, tq)
    qchunk = _chunk_of(q0 + lax.broadcasted_iota(I32, (tq, tk), 0))

    def score_body(j, carry):
        ks = pl.multiple_of(j * tk, tk)
        score = _score_tile(qim_ref, wb_ref, ki2_ref[pl.ds(ks, tk), :], tq, tk)
        kpos = ks + lax.broadcasted_iota(I32, (tq, tk), 1)
        sc_ref[j] = _sort_key(score, _chunk_of(kpos) <= qchunk)
        return carry

    lax.fori_loop(0, nkt, score_body, 0)

    n_adm = (qchunk[:, :1] + 1) * CHUNK
    krow = jnp.minimum(n_adm, topk).astype(F32)
    _select(sc_ref, bias_ref, cut_ref, nkt, krow, tq, tk, nbits)

    for g in range(B_KV_HEADS):
        qg = _stack_heads(q_ref, g)
        gs = slice(g * HEAD_DIM, (g + 1) * HEAD_DIM)

        def att_body(j, carry):
            ks = pl.multiple_of(j * tk, tk)
            return _attn_update(qg, kb_ref[pl.ds(ks, tk), gs], vb_ref[pl.ds(ks, tk), gs],
                                bias_ref[j], *carry, scale)

        init = (jnp.full((rep * tq, 1), NEG, F32), jnp.zeros((rep * tq, 1), F32),
                jnp.zeros((rep * tq, HEAD_DIM), F32))
        _, l, acc = lax.fori_loop(0, nkt, att_body, init)
        _store_heads(o_ref, g, acc / l, tq)


def _dsa_prompt(q, qim, wi, kb, vb, ki2, bsz, t, tq, tk):
    nq = t // tq
    topk = min(IDX_TOPK_MAX, t // 4)
    nbits = max(1, (t - 1).bit_length())
    qrow = lambda w: pl.BlockSpec((tq, w), lambda b, i: (b * nq + i, 0))
    seq = lambda w: pl.BlockSpec((t, w), lambda b, i: (b, 0))
    nt = t // tk
    return pl.pallas_call(
        functools.partial(_dsa_prompt_body, tq=tq, tk=tk, topk=topk, nbits=nbits,
                          scale=HEAD_DIM ** -0.5),
        grid=(bsz, nq),
        in_specs=[qrow(q.shape[1]), qrow(qim.shape[1]), qrow(LANES),
                  seq(kb.shape[1]), seq(vb.shape[1]), seq(LANES)],
        out_specs=qrow(q.shape[1]),
        out_shape=jax.ShapeDtypeStruct(q.shape, BF16),
        scratch_shapes=[pltpu.VMEM((IDX_HEADS, tq, LANES), F32),
                        pltpu.VMEM((nt, tq, tk), I32),
                        pltpu.VMEM((nt, tq, tk), F32),
                        pltpu.VMEM((tq, LANES), I32)],
        compiler_params=pltpu.CompilerParams(dimension_semantics=("arbitrary", "arbitrary"),
                                             vmem_limit_bytes=VMEM_LIMIT),
        name="dsa_prompt",
    )(q, qim, wi, kb, vb, ki2)


def _dsa_sample_body(q_ref, qim_ref, wi_ref, kn_ref, vn_ref, ki2n_ref, kidx2_ref, ck_ref, cv_ref,
                     o_ref, wb_ref, sc_ref, bias_ref, cut_ref, m_ref, l_ref, acc_ref,
                     *, tq, tk, nsub, npt, past, topk, nbits, scale):
    j = pl.program_id(1)
    nj = pl.num_programs(1)
    rep = B_HEADS // B_KV_HEADS
    pad = jnp.zeros((tk - tq, LANES), BF16)

    @pl.when(j == 0)
    def _():
        _wi_broadcast(wi_ref, wb_ref, tq)

        def score_body(jt, carry):
            ks = pl.multiple_of(jt * tk, tk)
            sc_ref[jt] = _sort_key(_score_tile(qim_ref, wb_ref, kidx2_ref[pl.ds(ks, tk), :], tq, tk),
                                   None)
            return carry

        lax.fori_loop(0, npt, score_body, 0)
        ki2n = jnp.concatenate([ki2n_ref[...], pad], axis=0)
        col = lax.broadcasted_iota(I32, (tq, tk), 1)
        row = lax.broadcasted_iota(I32, (tq, tk), 0)
        adm = jnp.where(col < tq, _chunk_of(past + col), 2 ** 30) <= _chunk_of(past + row)
        sc_ref[npt] = _sort_key(_score_tile(qim_ref, wb_ref, ki2n, tq, tk), adm)
        n_adm = (_chunk_of(past + row[:, :1]) + 1) * CHUNK
        krow = jnp.minimum(n_adm, topk).astype(F32)
        _select(sc_ref, bias_ref, cut_ref, npt + 1, krow, tq, tk, nbits)
        m_ref[...] = jnp.full(m_ref.shape, NEG, F32)
        l_ref[...] = jnp.zeros(l_ref.shape, F32)
        acc_ref[...] = jnp.zeros(acc_ref.shape, F32)

    def update(g, kt, vt, bias):
        qg = _stack_heads(q_ref, g)
        m, l, acc = _attn_update(qg, kt, vt, bias, m_ref[g], l_ref[g], acc_ref[g], scale)
        m_ref[g] = m
        l_ref[g] = l
        acc_ref[g] = acc

    for su in range(nsub):
        rs = slice(su * tk, (su + 1) * tk)
        bias = bias_ref[j * nsub + su]
        for g in range(B_KV_HEADS):
            gs = slice(g * HEAD_DIM, (g + 1) * HEAD_DIM)
            update(g, ck_ref[rs, gs].astype(BF16), cv_ref[rs, gs].astype(BF16), bias)

    @pl.when(j == nj - 1)
    def _():
        bias = bias_ref[npt]
        for g in range(B_KV_HEADS):
            gs = slice(g * HEAD_DIM, (g + 1) * HEAD_DIM)
            kt = jnp.concatenate([kn_ref[:, gs], pad], axis=0)
            vt = jnp.concatenate([vn_ref[:, gs], pad], axis=0)
            update(g, kt, vt, bias)
            _store_heads(o_ref, g, acc_ref[g] / l_ref[g], tq)


def _dsa_sample(q, qim, wi, kn, vn, ki2n, kidx2, ck, cv, bsz, t, past, tk, nsub):
    tq = t
    npt = past // tk
    nj = npt // nsub
    topk = min(IDX_TOPK_MAX, (past + t) // 4)
    nbits = max(1, (past + t - 1).bit_length())
    rep = B_HEADS // B_KV_HEADS
    qrow = lambda w: pl.BlockSpec((tq, w), lambda b, j: (b, 0))
    kvw = ck.shape[1]
    cache = pl.BlockSpec((nsub * tk, kvw), lambda b, j: (b * nj + j, 0))
    return pl.pallas_call(
        functools.partial(_dsa_sample_body, tq=tq, tk=tk, nsub=nsub, npt=npt, past=past,
                          topk=topk, nbits=nbits, scale=HEAD_DIM ** -0.5),
        grid=(bsz, nj),
        in_specs=[qrow(q.shape[1]), qrow(qim.shape[1]), qrow(LANES), qrow(kvw), qrow(kvw),
                  qrow(LANES), pl.BlockSpec((past, LANES), lambda b, j: (b, 0)), cache, cache],
        out_specs=qrow(q.shape[1]),
        out_shape=jax.ShapeDtypeStruct(q.shape, BF16),
        scratch_shapes=[pltpu.VMEM((IDX_HEADS, tq, LANES), F32),
                        pltpu.VMEM((npt + 1, tq, tk), I32),
                        pltpu.VMEM((npt + 1, tq, tk), F32),
                        pltpu.VMEM((tq, LANES), I32),
                        pltpu.VMEM((B_KV_HEADS, rep * tq, 1), F32),
                        pltpu.VMEM((B_KV_HEADS, rep * tq, 1), F32),
                        pltpu.VMEM((B_KV_HEADS, rep * tq, HEAD_DIM), F32)],
        compiler_params=pltpu.CompilerParams(dimension_semantics=("arbitrary", "arbitrary"),
                                             vmem_limit_bytes=VMEM_LIMIT),
        name="dsa_sample",
    )(q, qim, wi, kn, vn, ki2n, kidx2, ck, cv)


def _post_body(oa_ref, ob_ref, ga_ref, gb_ref, x_ref, wa_ref, wb_ref, wo_ref, g2_ref, wr_ref,
               br_ref, x1_o, t_o, eidx_o, prob_o):
    ya = _dot(oa_ref[...], wa_ref[...])
    yb = _dot(ob_ref[...], wb_ref[...])
    merged = (jax.nn.sigmoid(ga_ref[...].astype(F32)) * ya
              + jax.nn.sigmoid(gb_ref[...].astype(F32)) * yb)
    x1 = x_ref[...] + _dot(merged.astype(BF16), wo_ref[...])
    x1_o[...] = x1
    tok = _rms(x1) * g2_ref[...]
    t_o[...] = tok
    logits = _dot(tok.astype(BF16), wr_ref[...]) + br_ref[...]
    tm = logits.shape[0]
    lane = lax.broadcasted_iota(I32, (tm, LANES), 1).astype(F32)
    cur = jnp.where(lane < N_EXPERTS, logits, -jnp.inf)
    top = None
    den = jnp.zeros((tm, 1), F32)
    eidx = jnp.zeros((tm, LANES), F32)
    prob = jnp.zeros((tm, LANES), F32)
    for k in range(TOP_K):
        mx = jnp.max(cur, axis=1, keepdims=True)
        first = jnp.min(jnp.where(cur == mx, lane, float(LANES)), axis=1, keepdims=True)
        if top is None:
            top = mx
        e = jnp.exp(mx - top)
        den = den + e
        eidx = jnp.where(lane == k, first, eidx)
        prob = jnp.where(lane == k, e, prob)
        cur = jnp.where(lane == first, -jnp.inf, cur)
    eidx_o[...] = eidx.astype(I32)
    prob_o[...] = prob / den


def _post(oa, ob, ga, gb, x2, wa, wb, wo, g2, wr, br, tm):
    r, d = x2.shape
    row = lambda w: pl.BlockSpec((tm, w), lambda i: (i, 0))
    return pl.pallas_call(
        _post_body,
        grid=(r // tm,),
        in_specs=[row(d), row(d), row(d), row(d), row(d), _const_spec(wa.shape),
                  _const_spec(wb.shape), _const_spec(wo.shape), _const_spec(g2.shape),
                  _const_spec(wr.shape), _const_spec(br.shape)],
        out_specs=[row(d), row(d), row(LANES), row(LANES)],
        out_shape=[jax.ShapeDtypeStruct((r, d), F32), jax.ShapeDtypeStruct((r, d), F32),
                   jax.ShapeDtypeStruct((r, LANES), I32), jax.ShapeDtypeStruct((r, LANES), F32)],
        compiler_params=pltpu.CompilerParams(dimension_semantics=("arbitrary",),
                                             vmem_limit_bytes=VMEM_LIMIT),
        name="post",
    )(oa, ob, ga, gb, x2, wa, wb, wo, g2, wr, br)


MOE_TILE = 256


def _rank_body(eidx_ref, cin_ref, rank_o, cout_o, carry_ref):
    @pl.when(pl.program_id(0) == 0)
    def _():
        carry_ref[...] = cin_ref[...]

    eidx = eidx_ref[...]
    tm = eidx.shape[0]
    lane = lax.broadcasted_iota(I32, (tm, LANES), 1)
    hits = [lane == eidx[:, k:k + 1] for k in range(TOP_K)]
    onehot = jnp.zeros((tm, LANES), F32)
    for hit in hits:
        onehot = onehot + jnp.where(hit, 1.0, 0.0)
    r_i = lax.broadcasted_iota(I32, (tm, tm), 0)
    c_i = lax.broadcasted_iota(I32, (tm, tm), 1)
    before = jnp.where(c_i < r_i, 1.0, 0.0).astype(BF16)
    base = carry_ref[...] + _dot(before, onehot.astype(BF16))
    rank = jnp.zeros((tm, LANES), F32)
    for k, hit in enumerate(hits):
        rk = jnp.sum(jnp.where(hit, base, 0.0), axis=1, keepdims=True)
        rank = jnp.where(lane == k, rk, rank)
    rank_o[...] = rank.astype(I32)
    carry_ref[...] += jnp.sum(onehot, axis=0, keepdims=True)
    cout_o[...] = carry_ref[...]


def _rank(eidx, cin, tm):
    r = eidx.shape[0]
    return pl.pallas_call(
        _rank_body,
        grid=(r // tm,),
        in_specs=[pl.BlockSpec((tm, LANES), lambda i: (i, 0)), _const_spec(cin.shape)],
        out_specs=[pl.BlockSpec((tm, LANES), lambda i: (i, 0)),
                   pl.BlockSpec((1, LANES), lambda i: (0, 0))],
        out_shape=[jax.ShapeDtypeStruct((r, LANES), I32), jax.ShapeDtypeStruct((1, LANES), F32)],
        scratch_shapes=[pltpu.VMEM((1, LANES), F32)],
        compiler_params=pltpu.CompilerParams(dimension_semantics=("arbitrary",)),
        name="moe_rank",
    )(eidx, cin)


def _wait_rows(ref, sem):
    pltpu.make_async_copy(ref, ref, sem).wait()


def _dispatch_body(pos_ref, tok_ref, xs_in, xs_out, sem):
    del xs_in
    tm = tok_ref.shape[0]

    def issue(i, carry):
        for k in range(TOP_K):
            p = pos_ref[i * TOP_K + k]
            pltpu.make_async_copy(tok_ref.at[pl.ds(i, 1)], xs_out.at[pl.ds(p, 1)], sem).start()
        return carry

    lax.fori_loop(0, tm, issue, 0)
    for _ in range(TOP_K):
        _wait_rows(tok_ref, sem)


def _dispatch(pos, tok, xs, tm):
    r, d = tok.shape
    return pl.pallas_call(
        _dispatch_body,
        grid=(r // tm,),
        in_specs=[pl.BlockSpec((tm * TOP_K,), lambda i: (i,), memory_space=pltpu.SMEM),
                  pl.BlockSpec((tm, d), lambda i: (i, 0)),
                  pl.BlockSpec(memory_space=pl.ANY)],
        out_specs=pl.BlockSpec(memory_space=pl.ANY),
        out_shape=jax.ShapeDtypeStruct(xs.shape, xs.dtype),
        scratch_shapes=[pltpu.SemaphoreType.DMA],
        input_output_aliases={2: 0},
        compiler_params=pltpu.CompilerParams(dimension_semantics=("arbitrary",),
                                             has_side_effects=True),
        name="moe_dispatch",
    )(pos, tok, xs)


def _experts_body(te_ref, na_ref, x_ref, wgu_ref, bgu_ref, wd_ref, bd_ref, y_ref, wgu_s, wd_s):
    r = pl.program_id(0)
    e = te_ref[r]
    prev = te_ref[jnp.maximum(r - 1, 0)]

    @pl.when((r == 0) | (e != prev))
    def _():
        rows = 128
        for c in range(wgu_s.shape[0] // rows):
            wgu_s[c * rows:(c + 1) * rows, :] = wgu_ref[0, c * rows:(c + 1) * rows, :].astype(BF16)
        for c in range(wd_s.shape[0] // rows):
            wd_s[c * rows:(c + 1) * rows, :] = wd_ref[0, c * rows:(c + 1) * rows, :].astype(BF16)

    @pl.when(r < na_ref[0])
    def _():
        gu = _dot(x_ref[...].astype(BF16), wgu_s[...]) + bgu_ref[0]
        dff = gu.shape[1] // 2
        gate = jnp.minimum(gu[:, :dff], SWIGLU_LIMIT)
        up = jnp.clip(gu[:, dff:], -SWIGLU_LIMIT, SWIGLU_LIMIT)
        act = (up + 1.0) * gate * jax.nn.sigmoid(SWIGLU_ALPHA * gate)
        y_ref[...] = _dot(act.astype(BF16), wd_s[...]) + bd_ref[0]

    @pl.when(r >= na_ref[0])
    def _():
        y_ref[...] = jnp.zeros(y_ref.shape, F32)


def _experts(tile_expert, n_active, xs, wgu, bgu, wd, bd):
    p, d = xs.shape
    _, _, dff2 = wgu.shape
    tm = MOE_TILE
    grid_spec = pltpu.PrefetchScalarGridSpec(
        num_scalar_prefetch=2,
        grid=(p // tm,),
        in_specs=[pl.BlockSpec((tm, d), lambda r, te, na: (r, 0)),
                  pl.BlockSpec((1, d, dff2), lambda r, te, na: (te[r], 0, 0)),
                  pl.BlockSpec((1, 1, dff2), lambda r, te, na: (te[r], 0, 0)),
                  pl.BlockSpec((1, dff2 // 2, d), lambda r, te, na: (te[r], 0, 0)),
                  pl.BlockSpec((1, 1, d), lambda r, te, na: (te[r], 0, 0))],
        out_specs=pl.BlockSpec((tm, d), lambda r, te, na: (r, 0)),
        scratch_shapes=[pltpu.VMEM((d, dff2), BF16), pltpu.VMEM((dff2 // 2, d), BF16)],
    )
    return pl.pallas_call(
        _experts_body,
        grid_spec=grid_spec,
        out_shape=jax.ShapeDtypeStruct((p, d), F32),
        compiler_params=pltpu.CompilerParams(dimension_semantics=("arbitrary",),
                                             vmem_limit_bytes=VMEM_LIMIT),
        name="moe_experts",
    )(tile_expert, n_active, xs, wgu, bgu, wd, bd)


def _combine_body(pos_ref, prob_ref, x1_ref, y_hbm, out_ref, buf, sem):
    tm = x1_ref.shape[0]

    def issue(i, carry):
        for k in range(TOP_K):
            p = pos_ref[i * TOP_K + k]
            pltpu.make_async_copy(y_hbm.at[pl.ds(p, 1)], buf.at[k, pl.ds(i, 1)], sem).start()
        return carry

    lax.fori_loop(0, tm, issue, 0)
    for k in range(TOP_K):
        _wait_rows(buf.at[k], sem)
    prob = prob_ref[...]
    ffn = prob[:, 0:1] * buf[0]
    for k in range(1, TOP_K):
        ffn = ffn + prob[:, k:k + 1] * buf[k]
    out_ref[...] = x1_ref[...] + ffn


def _combine(pos, prob, x1, ys, tm):
    r, d = x1.shape
    return pl.pallas_call(
        _combine_body,
        grid=(r // tm,),
        in_specs=[pl.BlockSpec((tm * TOP_K,), lambda i: (i,), memory_space=pltpu.SMEM),
                  pl.BlockSpec((tm, LANES), lambda i: (i, 0)),
                  pl.BlockSpec((tm, d), lambda i: (i, 0)),
                  pl.BlockSpec(memory_space=pl.ANY)],
        out_specs=pl.BlockSpec((tm, d), lambda i: (i, 0)),
        out_shape=jax.ShapeDtypeStruct((r, d), F32),
        scratch_shapes=[pltpu.VMEM((TOP_K, tm, d), F32), pltpu.SemaphoreType.DMA],
        compiler_params=pltpu.CompilerParams(dimension_semantics=("arbitrary",),
                                             vmem_limit_bytes=VMEM_LIMIT),
        name="moe_combine",
    )(pos, prob, x1, ys)


def _moe(streams, wgu, bgu, wd, bd):
    tm = MOE_TILE
    counts = jnp.zeros((1, LANES), F32)
    ranks = []
    for (x1, tok, eidx, prob) in streams:
        rank, counts = _rank(eidx, counts, _row_tile(eidx.shape[0], 256))
        ranks.append(rank)
    cnt = counts[0, :N_EXPERTS].astype(I32)
    padded = ((cnt + tm - 1) // tm) * tm
    ends = jnp.cumsum(padded)
    offs = ends - padded
    n_pairs = sum(s[0].shape[0] for s in streams) * TOP_K
    n_tiles = n_pairs // tm + N_EXPERTS
    tile_start = jnp.arange(n_tiles, dtype=I32) * tm
    tile_expert = jnp.minimum(jnp.searchsorted(ends, tile_start, side="right"),
                              N_EXPERTS - 1).astype(I32)
    n_active = (ends[-1:] // tm).astype(I32)
    d = streams[0][0].shape[1]
    xs = jnp.zeros((n_tiles * tm, d), F32)
    poss = []
    for (x1, tok, eidx, prob), rank in zip(streams, ranks):
        pos = (offs[eidx[:, :TOP_K]] + rank[:, :TOP_K]).reshape(-1)
        poss.append(pos)
        xs = _dispatch(pos, tok, xs, _row_tile(tok.shape[0], 256))
    ys = _experts(tile_expert, n_active, xs, wgu, bgu, wd, bd)
    return [_combine(pos, prob, x1, ys, _row_tile(x1.shape[0], 256))
            for (x1, tok, eidx, prob), pos in zip(streams, poss)]


def _row_tile(r, want):
    tm = min(r, want)
    assert r % tm == 0, (r, tm)
    return tm


def _mixers(x2, bsz, t, pos, s0, caches, wts, layer):
    (g1, wm, ws, wg, qg, kg, lbp, ng) = wts
    r = x2.shape[0]
    tm = _row_tile(r, 256)
    assert t % tm == 0 or tm % t == 0
    tabs_a = _rope_tables(pos, HEAD_DIM, 1)
    tabs_b = _rope_tables(pos, IDX_DIM, LANES // IDX_DIM)
    tabs = tabs_a + tabs_b
    if tm > t:
        tabs = tuple(jnp.tile(tb, (tm // t, 1)) for tb in tabs)
    (aq, af, ai, ag, q, kf, kb, vf, vb, qim, kif, ki2, wi, ga, gb) = _inproj(
        x2, g1, wm, ws, wg, qg, kg, tabs, tm)
    oa, s_new = _hgrn(aq, af, ai, ag, s0, lbp, ng, bsz, t, layer)
    if caches is None:
        tq = _row_tile(t, 128)
        tk = _row_tile(t, 512)
        ob = _dsa_prompt(q, qim, wi, kb, vb, ki2, bsz, t, tq, tk)
    else:
        ck, cv, kidx2, past = caches
        tk = _row_tile(past, 512)
        nsub = 2 if (past // tk) % 2 == 0 else 1
        ob = _dsa_sample(q, qim, wi, kb, vb, ki2, kidx2, ck, cv, bsz, t, past, tk, nsub)
    return oa, ob, ga, gb, kf, vf, kif, s_new


def kernel(x_prompt, x_sample, cache_k, cache_v, cache_kidx, state_hgrn, norm1_g, w_in, lower_bounds, hgrn_norm_g, q_norm_g, k_norm_g, w_branch_a, w_branch_b, w_out, norm2_g, w_router, b_router, w_gate_up, b_gate_up, w_down, b_down):
    bp, tp, d = x_prompt.shape
    bs, ts, _ = x_sample.shape
    depth = w_in.shape[0]
    past = cache_k.shape[2]
    kvw = B_KV_HEADS * HEAD_DIM
    pos_p = jnp.arange(tp, dtype=I32)
    pos_s = past + jnp.arange(ts, dtype=I32)
    xp = x_prompt.reshape(bp * tp, d)
    xs = x_sample.reshape(bs * ts, d)
    n_main = 2 * A_HEADS * A_DK + 2 * A_HEADS * A_DV + B_HEADS * HEAD_DIM + 2 * kvw + IDX_HEADS * IDX_DIM
    n_small = n_main + IDX_DIM + IDX_HEADS
    outs = [[] for _ in range(8)]
    for l in range(depth):
        w = w_in[l]
        wm = w[:, :n_main].astype(BF16)
        w_ik = w[:, n_main:n_main + IDX_DIM]
        w_iw = w[:, n_main + IDX_DIM:n_small]
        ws = jnp.concatenate(
            [w_ik, w_ik, w_iw, jnp.zeros((d, LANES - IDX_HEADS), w.dtype)], axis=1).astype(BF16)
        wg = w[:, n_small:].astype(BF16)
        wts = (norm1_g[l][None], wm, ws, wg, q_norm_g[l][None], k_norm_g[l][None],
               lower_bounds, hgrn_norm_g[l][None])
        s0_p = jnp.zeros((bp, A_HEADS, A_DK, A_DV), F32)
        oa_p, ob_p, ga_p, gb_p, kp, vp, kip, sp = _mixers(xp, bp, tp, pos_p, s0_p, None, wts, l)
        kidx2 = jnp.concatenate([cache_kidx[l], cache_kidx[l]], axis=-1).astype(BF16)
        caches = (cache_k[l].reshape(bs * past, kvw), cache_v[l].reshape(bs * past, kvw),
                  kidx2.reshape(bs * past, LANES), past)
        oa_s, ob_s, ga_s, gb_s, ks, vs, kis, ss = _mixers(
            xs, bs, ts, pos_s, state_hgrn[l], caches, wts, l)

        wa = w_branch_a[l].astype(BF16)
        wb = w_branch_b[l].astype(BF16)
        wo = w_out[l].astype(BF16)
        wr = jnp.pad(w_router[l], ((0, 0), (0, LANES - N_EXPERTS))).astype(BF16)
        br = jnp.pad(b_router[l], (0, LANES - N_EXPERTS))[None]
        g2 = norm2_g[l][None]
        bgu = b_gate_up[l][:, None, :]
        bd = b_down[l][:, None, :]
        streams = []
        for (x2, oa, ob, ga, gb) in ((xp, oa_p, ob_p, ga_p, gb_p), (xs, oa_s, ob_s, ga_s, gb_s)):
            r = x2.shape[0]
            streams.append(_post(oa, ob, ga, gb, x2, wa, wb, wo, g2, wr, br, _row_tile(r, 256)))
        xp, xs = _moe(streams, w_gate_up[l], bgu, w_down[l], bd)
        for lst, v in zip(outs, (kp.reshape(bp, tp, B_KV_HEADS, HEAD_DIM),
                                 vp.reshape(bp, tp, B_KV_HEADS, HEAD_DIM),
                                 kip.reshape(bp, tp, IDX_DIM), sp,
                                 ks.reshape(bs, ts, B_KV_HEADS, HEAD_DIM),
                                 vs.reshape(bs, ts, B_KV_HEADS, HEAD_DIM),
                                 kis.reshape(bs, ts, IDX_DIM), ss)):
            lst.append(v)
    return (xp.reshape(bp, tp, d), xs.reshape(bs, ts, d)) + tuple(jnp.stack(o) for o in outs)
```

```python
import functools

import jax
import jax.numpy as jnp
from jax import lax
from jax.experimental import pallas as pl
from jax.experimental.pallas import tpu as pltpu

F32 = jnp.float32
BF16 = jnp.bfloat16
I32 = jnp.int32

CHUNK = 64
A_HEADS = 8
A_DK = 128
A_DV = 128
B_HEADS = 8
B_KV_HEADS = 4
HEAD_DIM = 128
IDX_HEADS = 8
IDX_DIM = 64
IDX_TOPK_MAX = 256
ROPE_THETA = 10000.0
N_EXPERTS = 32
TOP_K = 4
SWIGLU_LIMIT = 7.0
SWIGLU_ALPHA = 1.702
EPS = 1e-6

LANES = 128
INT_MIN = -(2 ** 31)
NEG = -1e30
VMEM_LIMIT = 56 * 1024 * 1024


def _rms(x):
    return x * lax.rsqrt(jnp.mean(x * x, axis=-1, keepdims=True) + EPS)


def _silu(x):
    return x * jax.nn.sigmoid(x)


def _dot(a, b):
    return jnp.dot(a, b, preferred_element_type=F32)


def _dot_nt(a, b):
    return lax.dot_general(a, b, (((1,), (1,)), ((), ())), preferred_element_type=F32)


def _dot_tn(a, b):
    return lax.dot_general(a, b, (((0,), (0,)), ((), ())), preferred_element_type=F32)


def _const_spec(shape):
    zeros = (0,) * len(shape)
    return pl.BlockSpec(shape, lambda *_: zeros, pipeline_mode=pl.Buffered(1))


def _rope_tables(pos, d, reps):
    inv = 1.0 / (ROPE_THETA ** (jnp.arange(0, d, 2, dtype=F32) / d))
    ang = pos.astype(F32)[:, None] * inv[None, :]
    cos = jnp.cos(ang)
    sin = jnp.sin(ang)
    cos_t = jnp.concatenate([cos, cos] * reps, axis=-1)
    sin_t = jnp.concatenate([-sin, sin] * reps, axis=-1)
    return cos_t, sin_t


def _inproj_body(x_ref, g1_ref, wm_ref, ws_ref, wg_ref, qg_ref, kg_ref, ca_ref, sa_ref, cb_ref,
                 sb_ref, aq_o, af_o, ai_o, ag_o, q_o, kf_o, kb_o, vf_o, vb_o, qim_o, kif_o, ki2_o,
                 wi_o, ga_o, gb_o, vt_o):
    x = x_ref[...]
    tm = x.shape[0]
    hb = (_rms(x) * g1_ref[...]).astype(BF16)
    a_qk = A_HEADS * A_DK
    a_v = A_HEADS * A_DV
    b_q = B_HEADS * HEAD_DIM
    b_kv = B_KV_HEADS * HEAD_DIM
    iq_w = IDX_HEADS * IDX_DIM
    o = 0
    aq_o[...] = _dot(hb, wm_ref[:, o:o + a_qk]).astype(BF16)
    o += a_qk
    af_o[...] = _dot(hb, wm_ref[:, o:o + a_qk])
    o += a_qk
    ai_o[...] = _dot(hb, wm_ref[:, o:o + a_v]).astype(BF16)
    o += a_v
    ag_o[...] = _dot(hb, wm_ref[:, o:o + a_v]).astype(BF16)
    o += a_v

    ca = ca_ref[...]
    sa = sa_ref[...]

    def rope_head(y):
        return y * ca + pltpu.roll(y, HEAD_DIM // 2, 1) * sa

    zq = _dot(hb, wm_ref[:, o:o + b_q])
    o += b_q
    for h in range(B_HEADS):
        sl = slice(h * HEAD_DIM, (h + 1) * HEAD_DIM)
        q_o[:, sl] = rope_head(_rms(zq[:, sl]) * qg_ref[...]).astype(BF16)
    zk = _dot(hb, wm_ref[:, o:o + b_kv])
    o += b_kv
    for h in range(B_KV_HEADS):
        sl = slice(h * HEAD_DIM, (h + 1) * HEAD_DIM)
        y = rope_head(_rms(zk[:, sl]) * kg_ref[...])
        kf_o[pl.ds(h, tm, stride=B_KV_HEADS), :] = y
        kb_o[:, sl] = y.astype(BF16)
    zv = _dot(hb, wm_ref[:, o:o + b_kv])
    o += b_kv
    for h in range(B_KV_HEADS):
        vf_o[pl.ds(h, tm, stride=B_KV_HEADS), :] = zv[:, h * HEAD_DIM:(h + 1) * HEAD_DIM]
    vb_o[...] = zv.astype(BF16)
    vt_o[0] = zv.T.astype(BF16)

    cb = cb_ref[...]
    sb = sb_ref[...]
    lane = lax.broadcasted_iota(I32, (tm, LANES), 1)
    first_half = (lane & (IDX_DIM - 1)) < (IDX_DIM // 2)

    def rope_idx(y):
        partner = jnp.where(first_half, pltpu.roll(y, LANES - IDX_DIM // 2, 1),
                            pltpu.roll(y, IDX_DIM // 2, 1))
        return y * cb + partner * sb

    zi = _dot(hb, wm_ref[:, o:o + iq_w])
    for p in range(iq_w // LANES):
        y = rope_idx(zi[:, p * LANES:(p + 1) * LANES])
        qim_o[:, (2 * p) * LANES:(2 * p + 1) * LANES] = jnp.where(lane < IDX_DIM, y, 0.0).astype(BF16)
        qim_o[:, (2 * p + 1) * LANES:(2 * p + 2) * LANES] = jnp.where(lane >= IDX_DIM, y, 0.0).astype(BF16)
    zs = _dot(hb, ws_ref[...])
    y = rope_idx(zs[:, :LANES])
    ki2_o[...] = y.astype(BF16)
    kif_o[...] = y[:, :IDX_DIM]
    wi_o[...] = zs[:, LANES:] * (IDX_HEADS ** -0.5 * IDX_DIM ** -0.5)
    zg = _dot(hb, wg_ref[...])
    d = zg.shape[1] // 2
    ga_o[...] = zg[:, :d].astype(BF16)
    gb_o[...] = zg[:, d:].astype(BF16)


def _inproj(x2, g1, wm, ws, wg, qg, kg, tabs, tm, vt_tile):
    r, d = x2.shape
    ca, sa, cb, sb = tabs
    npos = ca.shape[0] // tm
    per = vt_tile // tm
    row = lambda w: pl.BlockSpec((tm, w), lambda i: (i, 0))
    tab = pl.BlockSpec((tm, LANES), lambda i: (i % npos, 0))
    a_qk = A_HEADS * A_DK
    b_q = B_HEADS * HEAD_DIM
    b_kv = B_KV_HEADS * HEAD_DIM
    kvh = B_KV_HEADS
    outs = [
        (1, a_qk, BF16), (1, a_qk, F32), (1, a_qk, BF16), (1, a_qk, BF16),
        (1, b_q, BF16), (kvh, HEAD_DIM, F32), (1, b_kv, BF16),
        (kvh, HEAD_DIM, F32), (1, b_kv, BF16),
        (1, IDX_HEADS * LANES, BF16), (1, IDX_DIM, F32), (1, LANES, BF16),
        (1, LANES, F32), (1, d, BF16), (1, d, BF16),
    ]
    return pl.pallas_call(
        _inproj_body,
        grid=(r // tm,),
        in_specs=[row(d), _const_spec(g1.shape), _const_spec(wm.shape), _const_spec(ws.shape),
                  _const_spec(wg.shape), _const_spec(qg.shape), _const_spec(kg.shape),
                  tab, tab, tab, tab],
        out_specs=[pl.BlockSpec((tm * k, w), lambda i: (i, 0)) for k, w, _ in outs] + [
            pl.BlockSpec((1, b_kv, tm), lambda i: (i // per, 0, i % per))],
        out_shape=[jax.ShapeDtypeStruct((r * k, w), dt) for k, w, dt in outs] + [
            jax.ShapeDtypeStruct((r // vt_tile, b_kv, vt_tile), BF16)],
        compiler_params=pltpu.CompilerParams(dimension_semantics=("arbitrary",),
                                             vmem_limit_bytes=VMEM_LIMIT),
        name="inproj",
    )(x2, g1, wm, ws, wg, qg, kg, ca, sa, cb, sb)


def _hgrn_body(aq_ref, af_ref, ai_ref, ag_ref, s0_ref, lbp_ref, ng_ref, oa_ref, sn_ref, st_ref,
               *, layer):
    c = pl.program_id(1)
    nc = pl.num_programs(1)

    @pl.when(c == 0)
    def _():
        for h in range(A_HEADS):
            st_ref[h] = s0_ref[0, h].T

    lbp = lbp_ref[...]
    e = jnp.exp(lbp - jnp.max(lbp, axis=0, keepdims=True))
    sm = e / jnp.sum(e, axis=0, keepdims=True)
    lb = jnp.sum(sm[:layer + 1], axis=0, keepdims=True)

    f = lb + (1.0 - lb) * jax.nn.sigmoid(af_ref[...])
    lf = jnp.log(f)
    n = lf.shape[0]
    r_i = lax.broadcasted_iota(I32, (n, n), 0)
    c_i = lax.broadcasted_iota(I32, (n, n), 1)
    causal = r_i >= c_i
    tri = jnp.where(causal, 1.0, 0.0).astype(BF16)
    hi = lf.astype(BF16)
    r1 = lf - hi.astype(F32)
    mid = r1.astype(BF16)
    lo = (r1 - mid.astype(F32)).astype(BF16)
    cum = _dot(tri, hi) + _dot(tri, mid) + _dot(tri, lo)
    last = cum[n - 1:n, :]
    qd = _silu(aq_ref[...].astype(F32)) * jnp.exp(cum)
    k = 1.0 - f
    kd = k * jnp.exp(-cum)
    kt = k * jnp.exp(last - cum)
    el = jnp.exp(last)
    gate = _silu(ag_ref[...].astype(F32))
    for h in range(A_HEADS):
        sl = slice(h * A_DK, (h + 1) * A_DK)
        qd_h = qd[:, sl].astype(BF16)
        v_h = ai_ref[:, sl]
        att = jnp.where(causal, _dot_nt(qd_h, kd[:, sl].astype(BF16)), 0.0)
        st = st_ref[h]
        o = _dot(att.astype(BF16), v_h) + _dot_nt(qd_h, st.astype(BF16))
        st_ref[h] = st * el[:, sl] + _dot_tn(v_h, kt[:, sl].astype(BF16))
        oa_ref[:, sl] = (_rms(o) * ng_ref[...] * gate[:, sl]).astype(BF16)

    @pl.when(c == nc - 1)
    def _():
        for h in range(A_HEADS):
            sn_ref[0, h] = st_ref[h].T


def _hgrn(aq, af, ai, ag, s0, lbp, ng, bsz, t, layer):
    d = aq.shape[1]
    nc = t // CHUNK
    row = pl.BlockSpec((CHUNK, d), lambda b, c: (b * nc + c, 0))
    st = pl.BlockSpec((1, A_HEADS, A_DK, A_DV), lambda b, c: (b, 0, 0, 0))
    return pl.pallas_call(
        functools.partial(_hgrn_body, layer=layer),
        grid=(bsz, nc),
        in_specs=[row, row, row, row, st, _const_spec(lbp.shape), _const_spec(ng.shape)],
        out_specs=[row, st],
        out_shape=[jax.ShapeDtypeStruct((bsz * t, d), BF16),
                   jax.ShapeDtypeStruct((bsz, A_HEADS, A_DK, A_DV), F32)],
        scratch_shapes=[pltpu.VMEM((A_HEADS, A_DV, A_DK), F32)],
        compiler_params=pltpu.CompilerParams(dimension_semantics=("arbitrary", "arbitrary"),
                                             vmem_limit_bytes=VMEM_LIMIT),
        name="hgrn",
    )(aq, af, ai, ag, s0, lbp, ng)


def _chunk_of(pos):
    return jnp.right_shift(pos, CHUNK.bit_length() - 1)


def _transpose_wi(wi, tq):
    if tq % LANES:
        wi = jnp.concatenate([wi, jnp.zeros((LANES - tq % LANES, LANES), F32)], axis=0)
    return wi.T[:IDX_HEADS, :tq]


def _score_tile(ki2_tile, qim_ref, wt):
    acc = None
    for h in range(IDX_HEADS):
        s = _dot_nt(ki2_tile, qim_ref[:, h * LANES:(h + 1) * LANES])
        term = jnp.maximum(s, 0.0) * wt[h:h + 1, :]
        acc = term if acc is None else acc + term
    return acc


def _mask_scores(score, adm):
    score = jnp.where(score == 0.0, 0.0, score)
    return score if adm is None else jnp.where(adm, score, -jnp.inf)


def _sort_key(score):
    bits = lax.bitcast_convert_type(score, I32)
    return jnp.where(bits < 0, bits ^ 0x7FFFFFFF, bits)


def _key_to_score(key):
    return lax.bitcast_convert_type(jnp.where(key < 0, key ^ 0x7FFFFFFF, key), F32)


SUBLANES = 8
COUNT_ACCS = 4
KEY_BITS = 32
PLANE_KEYS = SUBLANES * KEY_BITS


def _store_bit_planes(planes_ref, key, blk0):
    u = key ^ INT_MIN
    tk = u.shape[0]
    for blk in range(tk // PLANE_KEYS):
        a = [u[blk * PLANE_KEYS + i * SUBLANES:blk * PLANE_KEYS + (i + 1) * SUBLANES, :]
             for i in range(KEY_BITS)]
        for j, m in ((16, 0x0000FFFF), (8, 0x00FF00FF), (4, 0x0F0F0F0F), (2, 0x33333333),
                     (1, 0x55555555)):
            k = 0
            while k < KEY_BITS:
                t = (a[k] ^ lax.shift_right_logical(a[k + j], j)) & m
                a[k] = a[k] ^ t
                a[k + j] = a[k + j] ^ lax.shift_left(t, j)
                k = (k + j + 1) & ~j
        row = pl.multiple_of((blk0 + blk) * SUBLANES, SUBLANES)
        for p in range(KEY_BITS):
            planes_ref[p, pl.ds(row, SUBLANES), :] = a[p]


def _select(sc_ref, planes_ref, bias_ref, cut_ref, stat_ref, nkt, krow, tk, tq, nbits):
    nrow = planes_ref.shape[1]
    nblk = nkt * (tk // PLANE_KEYS)
    blk_of_row = jnp.right_shift(lax.broadcasted_iota(I32, (nrow, tq), 0),
                                 SUBLANES.bit_length() - 1)

    def word_count(words):
        pc = lax.population_count(words).reshape(nrow // SUBLANES, SUBLANES, tq)
        return jnp.sum(jnp.sum(pc, axis=0).astype(F32), axis=0, keepdims=True)

    def bit_body(i, carry):
        alive, above, thr_u = carry
        ones = alive & planes_ref[i]
        c1 = word_count(ones)
        take = above + c1 >= krow
        alive = jnp.where(take, ones, alive ^ ones)
        above = jnp.where(take, above, above + c1)
        thr_u = jnp.where(take, thr_u | jnp.left_shift(jnp.int32(1), KEY_BITS - 1 - i), thr_u)
        return alive, above, thr_u

    alive0 = jnp.where(blk_of_row < nblk, -1, 0).astype(I32)
    _, _, thr_u = lax.fori_loop(
        0, KEY_BITS, bit_body, (alive0, jnp.zeros((1, tq), F32), jnp.zeros((1, tq), I32)))

    nchunk = tk // SUBLANES
    sub = lax.broadcasted_iota(I32, (SUBLANES, tq), 0)

    def count(pred_fn):
        def body(j, accs):
            t = sc_ref[j]
            accs = list(accs)
            for c in range(nchunk):
                a = c % COUNT_ACCS
                accs[a] = pred_fn(t[c * SUBLANES:(c + 1) * SUBLANES, :], j, c, accs[a])
            return tuple(accs)
        zero = jnp.zeros((SUBLANES, tq), F32)
        accs = lax.fori_loop(0, nkt, body, (zero,) * COUNT_ACCS)
        tot = accs[0]
        for a in accs[1:]:
            tot = tot + a
        return jnp.sum(tot, axis=0, keepdims=True)

    def count_ge(cand):
        c8 = jnp.broadcast_to(cand, (SUBLANES, tq))
        return count(lambda t, j, c, acc: jnp.where(t >= c8, acc + 1.0, acc))

    def count_gt(cand):
        c8 = jnp.broadcast_to(cand, (SUBLANES, tq))
        return count(lambda t, j, c, acc: jnp.where(t > c8, acc + 1.0, acc))

    def record(thr):
        stat_ref[0:1, :] = thr
        stat_ref[1:2, :] = count_ge(thr)
        stat_ref[2:3, :] = count_gt(thr)

    record(_key_to_score(thr_u ^ INT_MIN))
    good = (stat_ref[2:3, :] < krow) & (stat_ref[1:2, :] >= krow)

    @pl.when(jnp.max(jnp.where(good, 0.0, 1.0)) > 0.0)
    def _():
        def cmp_body(i, tu):
            cand_u = tu | jnp.left_shift(jnp.int32(1), KEY_BITS - 1 - i)
            return jnp.where(count_ge(_key_to_score(cand_u ^ INT_MIN)) >= krow, cand_u, tu)
        tu = lax.fori_loop(0, KEY_BITS, cmp_body, jnp.zeros((1, tq), I32))
        record(_key_to_score(tu ^ INT_MIN))

    thr = stat_ref[0:1, :]
    n_ge = stat_ref[1:2, :]
    need = krow - stat_ref[2:3, :]

    cut_ref[...] = jnp.full(cut_ref.shape, 2 ** 31 - 1, I32)

    @pl.when(jnp.max(jnp.where(n_ge > krow, 1.0, 0.0)) > 0.0)
    def _():
        thr8 = jnp.broadcast_to(thr, (SUBLANES, tq))

        def idx_body(i, cut):
            cand = cut | jnp.left_shift(jnp.int32(1), nbits - 1 - i)
            c8 = jnp.broadcast_to(cand, (SUBLANES, tq))
            below = count(lambda t, j, c, acc: jnp.where(
                t == thr8, jnp.where(j * tk + c * SUBLANES + sub < c8, acc + 1.0, acc), acc))
            return jnp.where(below < need, cand, cut)
        cut_ref[...] = lax.fori_loop(0, nbits, idx_body, jnp.zeros((1, tq), I32))

    cut = cut_ref[...]

    def write(j, carry):
        t = sc_ref[j]
        idx = j * tk + lax.broadcasted_iota(I32, (tk, tq), 0)
        tie = jnp.where(t == thr, jnp.where(idx <= cut, 0.0, NEG), NEG)
        bias_ref[j] = jnp.where(t > thr, 0.0, tie)
        return carry

    lax.fori_loop(0, nkt, write, 0)


def _attn_update(kt, qg, bias, m, l, acc, c1, pv):
    rep = qg.shape[0] // bias.shape[1]
    s = _dot_nt(kt, qg) * c1 + jnp.concatenate([bias] * rep, axis=1)
    m_new = jnp.maximum(m, jnp.max(s, axis=0, keepdims=True))
    alpha = jnp.exp2(m - m_new)
    p = jnp.exp2(s - m_new)
    l_new = alpha * l + jnp.sum(p, axis=0, keepdims=True)
    acc_new = alpha * acc + pv(p.astype(BF16))
    return m_new, l_new, acc_new


def _stack_heads(q_ref, u, stack):
    heads = [q_ref[:, h * HEAD_DIM:(h + 1) * HEAD_DIM] for h in range(u * stack, (u + 1) * stack)]
    return heads[0] if stack == 1 else jnp.concatenate(heads, axis=0)


def _store_heads(o_ref, u, stack, o, tq):
    for r in range(stack):
        h = u * stack + r
        o_ref[:, h * HEAD_DIM:(h + 1) * HEAD_DIM] = o[r * tq:(r + 1) * tq].astype(o_ref.dtype)


LOG2E = 1.4426950408889634


def _attn_reset(m_ref, l_ref, acc_ref):
    m_ref[...] = jnp.full(m_ref.shape, NEG, F32)
    l_ref[...] = jnp.zeros(l_ref.shape, F32)
    acc_ref[...] = jnp.zeros(acc_ref.shape, F32)


def _attn_step(q_ref, u, stack, kt, bias, m_ref, l_ref, acc_ref, c1, pv):
    m, l, acc = _attn_update(kt, _stack_heads(q_ref, u, stack), bias, m_ref[u], l_ref[u],
                             acc_ref[u], c1, pv)
    m_ref[u] = m
    l_ref[u] = l
    acc_ref[u] = acc


def _attn_logits(q_ref, u, stack, kt, bias, m_ref, c1):
    s = _dot_nt(kt, _stack_heads(q_ref, u, stack)) * c1 + jnp.concatenate([bias] * stack, axis=1)
    return s, jnp.maximum(m_ref[u], jnp.max(s, axis=0, keepdims=True))


def _attn_accumulate(u, s, m_new, m_ref, l_ref, acc_ref, pv):
    alpha = jnp.exp2(m_ref[u] - m_new)
    p = jnp.exp2(s - m_new)
    l_ref[u] = alpha * l_ref[u] + jnp.sum(p, axis=0, keepdims=True)
    acc_ref[u] = alpha * acc_ref[u] + pv(p.astype(BF16))
    m_ref[u] = m_new


def _attn_finish(o_ref, u, stack, l_ref, acc_ref, tq):
    _store_heads(o_ref, u, stack, (acc_ref[u] / l_ref[u]).T, tq)


def _dsa_prompt_body(q_ref, qim_ref, wi_ref, kb_ref, vt_ref, ki2_ref, o_ref, sc_ref, planes_ref,
                     bias_ref, cut_ref, stat_ref, m_ref, l_ref, acc_ref,
                     *, tq, tk, topk, nbits, c1):
    q0 = pl.program_id(1) * tq
    nkt = lax.div(q0 + tq + tk - 1, tk)
    wt = _transpose_wi(wi_ref[...], tq)
    qchunk = _chunk_of(q0 + lax.broadcasted_iota(I32, (1, tq), 1))

    @pl.when((pl.program_id(0) == 0) & (pl.program_id(1) == 0))
    def _():
        planes_ref[...] = jnp.zeros(planes_ref.shape, I32)

    def score_body(j, carry):
        ks = pl.multiple_of(j * tk, tk)
        score = _score_tile(ki2_ref[pl.ds(ks, tk), :], qim_ref, wt)
        kpos = ks + lax.broadcasted_iota(I32, (tk, tq), 0)
        score = _mask_scores(score, _chunk_of(kpos) <= qchunk)
        sc_ref[j] = score
        _store_bit_planes(planes_ref, _sort_key(score), j * (tk // PLANE_KEYS))
        return carry

    lax.fori_loop(0, nkt, score_body, 0)

    krow = jnp.minimum((qchunk + 1) * CHUNK, topk).astype(F32)
    _select(sc_ref, planes_ref, bias_ref, cut_ref, stat_ref, nkt, krow, tk, tq, nbits)

    _attn_reset(m_ref, l_ref, acc_ref)

    rep = B_HEADS // B_KV_HEADS

    def att_body(j, carry):
        ks = pl.multiple_of(j * tk, tk)
        bias = bias_ref[j]
        gsl = lambda g: slice(g * HEAD_DIM, (g + 1) * HEAD_DIM)
        logits = lambda g: _attn_logits(q_ref, g, rep, kb_ref[pl.ds(ks, tk), gsl(g)], bias,
                                        m_ref, c1)
        pending = logits(0)
        for g in range(B_KV_HEADS):
            nxt = logits(g + 1) if g + 1 < B_KV_HEADS else None
            vt = vt_ref[j, gsl(g), :]
            _attn_accumulate(g, *pending, m_ref, l_ref, acc_ref, lambda p, vt=vt: _dot(vt, p))
            pending = nxt
        return carry

    lax.fori_loop(0, nkt, att_body, 0)
    for g in range(B_KV_HEADS):
        _attn_finish(o_ref, g, rep, l_ref, acc_ref, tq)


def _dsa_scratch(nt, tk, tq, stack):
    units = B_HEADS // stack
    return [pltpu.VMEM((nt, tk, tq), F32),
            pltpu.VMEM((KEY_BITS, nt * tk // KEY_BITS, tq), I32),
            pltpu.VMEM((nt, tk, tq), F32),
            pltpu.VMEM((1, tq), I32),
            pltpu.VMEM((SUBLANES, tq), F32),
            pltpu.VMEM((units, 1, stack * tq), F32),
            pltpu.VMEM((units, 1, stack * tq), F32),
            pltpu.VMEM((units, HEAD_DIM, stack * tq), F32)]


def _dsa_prompt(q, qim, wi, kb, vt, ki2, bsz, t, tq, tk):
    nq = t // tq
    nt = t // tk
    topk = min(IDX_TOPK_MAX, t // 4)
    nbits = max(1, (t - 1).bit_length())
    qrow = lambda w: pl.BlockSpec((tq, w), lambda b, i: (b * nq + i, 0))
    seq = lambda w: pl.BlockSpec((t, w), lambda b, i: (b, 0))
    return pl.pallas_call(
        functools.partial(_dsa_prompt_body, tq=tq, tk=tk, topk=topk, nbits=nbits,
                          c1=HEAD_DIM ** -0.5 * LOG2E),
        grid=(bsz, nq),
        in_specs=[qrow(q.shape[1]), qrow(qim.shape[1]), qrow(LANES), seq(kb.shape[1]),
                  pl.BlockSpec((nt, vt.shape[1], tk), lambda b, i: (b, 0, 0)), seq(LANES)],
        out_specs=qrow(q.shape[1]),
        out_shape=jax.ShapeDtypeStruct(q.shape, BF16),
        scratch_shapes=_dsa_scratch(nt, tk, tq, B_HEADS // B_KV_HEADS),
        compiler_params=pltpu.CompilerParams(dimension_semantics=("arbitrary", "arbitrary"),
                                             vmem_limit_bytes=VMEM_LIMIT),
        name="dsa_prompt",
    )(q, qim, wi, kb, vt, ki2)


def _dsa_sample_body(q_ref, qim_ref, wi_ref, kn_ref, vn_ref, ki2n_ref, kidx2_ref, ck_ref, cv_ref,
                     o_ref, sc_ref, planes_ref, bias_ref, cut_ref, stat_ref, m_ref, l_ref, acc_ref,
                     *, tq, tk, nsub, npt, past, topk, nbits, c1):
    j = pl.program_id(1)
    nj = pl.num_programs(1)
    pad = jnp.zeros((tk - tq, LANES), BF16)
    nb = tk // PLANE_KEYS

    @pl.when(j == 0)
    def _():
        wt = _transpose_wi(wi_ref[...], tq)

        def score_body(jt, carry):
            ks = pl.multiple_of(jt * tk, tk)
            score = _mask_scores(_score_tile(kidx2_ref[pl.ds(ks, tk), :], qim_ref, wt), None)
            sc_ref[jt] = score
            _store_bit_planes(planes_ref, _sort_key(score), jt * nb)
            return carry

        lax.fori_loop(0, npt, score_body, 0)
        ki2n = jnp.concatenate([ki2n_ref[...], pad], axis=0)
        krow_i = lax.broadcasted_iota(I32, (tk, tq), 0)
        qchunk = _chunk_of(past + lax.broadcasted_iota(I32, (1, tq), 1))
        adm = jnp.where(krow_i < tq, _chunk_of(past + krow_i), 2 ** 30) <= qchunk
        score = _mask_scores(_score_tile(ki2n, qim_ref, wt), adm)
        sc_ref[npt] = score
        _store_bit_planes(planes_ref, _sort_key(score), npt * nb)
        krow = jnp.minimum((qchunk + 1) * CHUNK, topk).astype(F32)
        _select(sc_ref, planes_ref, bias_ref, cut_ref, stat_ref, npt + 1, krow, tk, tq, nbits)
        _attn_reset(m_ref, l_ref, acc_ref)

    rep = B_HEADS // B_KV_HEADS

    def update(g, kt, vt, bias):
        _attn_step(q_ref, g, rep, kt, bias, m_ref, l_ref, acc_ref, c1, lambda p: _dot_tn(vt, p))

    for su in range(nsub):
        bias = bias_ref[j * nsub + su]
        rows = lambda g: pl.ds(su * tk * B_KV_HEADS + g, tk, stride=B_KV_HEADS)
        logits = lambda g: _attn_logits(q_ref, g, rep, ck_ref[rows(g), :].astype(BF16), bias,
                                        m_ref, c1)
        pending = logits(0)
        for g in range(B_KV_HEADS):
            nxt = logits(g + 1) if g + 1 < B_KV_HEADS else None
            vt = cv_ref[rows(g), :].astype(BF16)
            _attn_accumulate(g, *pending, m_ref, l_ref, acc_ref, lambda p, vt=vt: _dot_tn(vt, p))
            pending = nxt

    @pl.when(j == nj - 1)
    def _():
        bias = bias_ref[npt]
        for g in range(B_KV_HEADS):
            gs = slice(g * HEAD_DIM, (g + 1) * HEAD_DIM)
            kt = jnp.concatenate([kn_ref[:, gs], pad], axis=0)
            vt = jnp.concatenate([vn_ref[:, gs], pad], axis=0)
            update(g, kt, vt, bias)
            _attn_finish(o_ref, g, rep, l_ref, acc_ref, tq)


def _dsa_sample(q, qim, wi, kn, vn, ki2n, kidx2, ck, cv, bsz, t, past, tk, nsub, cache_off):
    tq = t
    npt = past // tk
    nj = npt // nsub
    topk = min(IDX_TOPK_MAX, (past + t) // 4)
    nbits = max(1, (past + t - 1).bit_length())
    rep = B_HEADS // B_KV_HEADS
    qrow = lambda w: pl.BlockSpec((tq, w), lambda b, j: (b, 0))
    kvw = kn.shape[1]
    cache = pl.BlockSpec((nsub * tk * B_KV_HEADS, HEAD_DIM),
                         lambda b, j: ((cache_off + b) * nj + j, 0))
    return pl.pallas_call(
        functools.partial(_dsa_sample_body, tq=tq, tk=tk, nsub=nsub, npt=npt, past=past,
                          topk=topk, nbits=nbits, c1=HEAD_DIM ** -0.5 * LOG2E),
        grid=(bsz, nj),
        in_specs=[qrow(q.shape[1]), qrow(qim.shape[1]), qrow(LANES), qrow(kvw), qrow(kvw),
                  qrow(LANES), pl.BlockSpec((past, LANES), lambda b, j: (b, 0)), cache, cache],
        out_specs=qrow(q.shape[1]),
        out_shape=jax.ShapeDtypeStruct(q.shape, BF16),
        scratch_shapes=_dsa_scratch(npt + 1, tk, tq, rep),
        compiler_params=pltpu.CompilerParams(dimension_semantics=("arbitrary", "arbitrary"),
                                             vmem_limit_bytes=VMEM_LIMIT),
        name="dsa_sample",
    )(q, qim, wi, kn, vn, ki2n, kidx2, ck, cv)


def _post_body(oa_ref, ob_ref, ga_ref, gb_ref, x_ref, wa_ref, wb_ref, wo_ref, g2_ref, wr_ref,
               br_ref, x1_o, t_o, eidx_o, prob_o):
    ya = _dot(oa_ref[...], wa_ref[...])
    yb = _dot(ob_ref[...], wb_ref[...])
    merged = (jax.nn.sigmoid(ga_ref[...].astype(F32)) * ya
              + jax.nn.sigmoid(gb_ref[...].astype(F32)) * yb)
    x1 = x_ref[...] + _dot(merged.astype(BF16), wo_ref[...])
    x1_o[...] = x1
    tok = _rms(x1) * g2_ref[...]
    t_o[...] = tok
    logits = _dot(tok.astype(BF16), wr_ref[...]) + br_ref[...]
    tm = logits.shape[0]
    lane = lax.broadcasted_iota(I32, (tm, LANES), 1).astype(F32)
    cur = jnp.where(lane < N_EXPERTS, logits, -jnp.inf)
    top = None
    den = jnp.zeros((tm, 1), F32)
    eidx = jnp.zeros((tm, LANES), F32)
    prob = jnp.zeros((tm, LANES), F32)
    for k in range(TOP_K):
        mx = jnp.max(cur, axis=1, keepdims=True)
        first = jnp.min(jnp.where(cur == mx, lane, float(LANES)), axis=1, keepdims=True)
        if top is None:
            top = mx
        e = jnp.exp(mx - top)
        den = den + e
        eidx = jnp.where(lane == k, first, eidx)
        prob = jnp.where(lane == k, e, prob)
        cur = jnp.where(lane == first, -jnp.inf, cur)
    eidx_o[...] = eidx.astype(I32)
    prob_o[...] = prob / den


def _post(oa, ob, ga, gb, x2, wa, wb, wo, g2, wr, br, tm):
    r, d = x2.shape
    row = lambda w: pl.BlockSpec((tm, w), lambda i: (i, 0))
    return pl.pallas_call(
        _post_body,
        grid=(r // tm,),
        in_specs=[row(d), row(d), row(d), row(d), row(d), _const_spec(wa.shape),
                  _const_spec(wb.shape), _const_spec(wo.shape), _const_spec(g2.shape),
                  _const_spec(wr.shape), _const_spec(br.shape)],
        out_specs=[row(d), row(d), row(LANES), row(LANES)],
        out_shape=[jax.ShapeDtypeStruct((r, d), F32), jax.ShapeDtypeStruct((r, d), F32),
                   jax.ShapeDtypeStruct((r, LANES), I32), jax.ShapeDtypeStruct((r, LANES), F32)],
        compiler_params=pltpu.CompilerParams(dimension_semantics=("arbitrary",),
                                             vmem_limit_bytes=VMEM_LIMIT),
        name="post",
    )(oa, ob, ga, gb, x2, wa, wb, wo, g2, wr, br)


MOE_TILE = 256
DMA_PRIORITIES = 2


def _rank_body(eidx_ref, cin_ref, rank_o, cout_o, carry_ref):
    @pl.when(pl.program_id(0) == 0)
    def _():
        carry_ref[...] = cin_ref[...]

    eidx = eidx_ref[...]
    tm = eidx.shape[0]
    lane = lax.broadcasted_iota(I32, (tm, LANES), 1)
    hits = [lane == eidx[:, k:k + 1] for k in range(TOP_K)]
    onehot = jnp.zeros((tm, LANES), F32)
    for hit in hits:
        onehot = onehot + jnp.where(hit, 1.0, 0.0)
    r_i = lax.broadcasted_iota(I32, (tm, tm), 0)
    c_i = lax.broadcasted_iota(I32, (tm, tm), 1)
    before = jnp.where(c_i < r_i, 1.0, 0.0).astype(BF16)
    base = carry_ref[...] + _dot(before, onehot.astype(BF16))
    rank = jnp.zeros((tm, LANES), F32)
    for k, hit in enumerate(hits):
        rk = jnp.sum(jnp.where(hit, base, 0.0), axis=1, keepdims=True)
        rank = jnp.where(lane == k, rk, rank)
    rank_o[...] = rank.astype(I32)
    carry_ref[...] += jnp.sum(onehot, axis=0, keepdims=True)
    cout_o[...] = carry_ref[...]


def _rank(eidx, cin, tm):
    r = eidx.shape[0]
    return pl.pallas_call(
        _rank_body,
        grid=(r // tm,),
        in_specs=[pl.BlockSpec((tm, LANES), lambda i: (i, 0)), _const_spec(cin.shape)],
        out_specs=[pl.BlockSpec((tm, LANES), lambda i: (i, 0)),
                   pl.BlockSpec((1, LANES), lambda i: (0, 0))],
        out_shape=[jax.ShapeDtypeStruct((r, LANES), I32), jax.ShapeDtypeStruct((1, LANES), F32)],
        scratch_shapes=[pltpu.VMEM((1, LANES), F32)],
        compiler_params=pltpu.CompilerParams(dimension_semantics=("arbitrary",)),
        name="moe_rank",
    )(eidx, cin)


def _wait_rows(ref, sem):
    pltpu.make_async_copy(ref, ref, sem).wait()


def _dispatch_body(pos_ref, tok_ref, xs_in, xs_out, sem):
    del xs_in
    tm = tok_ref.shape[0]

    def issue(i, carry):
        for k in range(TOP_K):
            p = pos_ref[i * TOP_K + k]
            pltpu.make_async_copy(tok_ref.at[pl.ds(i, 1)], xs_out.at[pl.ds(p, 1)], sem).start(
                priority=k % DMA_PRIORITIES)
        return carry

    lax.fori_loop(0, tm, issue, 0)
    for _ in range(TOP_K):
        _wait_rows(tok_ref, sem)


def _dispatch(pos, tok, xs, tm):
    r, d = tok.shape
    return pl.pallas_call(
        _dispatch_body,
        grid=(r // tm,),
        in_specs=[pl.BlockSpec((tm * TOP_K,), lambda i: (i,), memory_space=pltpu.SMEM),
                  pl.BlockSpec((tm, d), lambda i: (i, 0)),
                  pl.BlockSpec(memory_space=pl.ANY)],
        out_specs=pl.BlockSpec(memory_space=pl.ANY),
        out_shape=jax.ShapeDtypeStruct(xs.shape, xs.dtype),
        scratch_shapes=[pltpu.SemaphoreType.DMA],
        input_output_aliases={2: 0},
        compiler_params=pltpu.CompilerParams(dimension_semantics=("arbitrary",),
                                             has_side_effects=True),
        name="moe_dispatch",
    )(pos, tok, xs)


def _experts_body(te_ref, na_ref, x_ref, wgu_ref, bgu_ref, wd_ref, bd_ref, y_ref, wgu_s, wd_s):
    r = pl.program_id(0)
    e = te_ref[r]
    prev = te_ref[jnp.maximum(r - 1, 0)]

    @pl.when((r == 0) | (e != prev))
    def _():
        rows = 128
        for c in range(wgu_s.shape[0] // rows):
            wgu_s[c * rows:(c + 1) * rows, :] = wgu_ref[0, c * rows:(c + 1) * rows, :].astype(BF16)
        for c in range(wd_s.shape[0] // rows):
            wd_s[c * rows:(c + 1) * rows, :] = wd_ref[0, c * rows:(c + 1) * rows, :].astype(BF16)

    @pl.when(r < na_ref[0])
    def _():
        gu = _dot(x_ref[...].astype(BF16), wgu_s[...]) + bgu_ref[0]
        dff = gu.shape[1] // 2
        gate = jnp.minimum(gu[:, :dff], SWIGLU_LIMIT)
        up = jnp.clip(gu[:, dff:], -SWIGLU_LIMIT, SWIGLU_LIMIT)
        act = (up + 1.0) * gate * jax.nn.sigmoid(SWIGLU_ALPHA * gate)
        y_ref[...] = _dot(act.astype(BF16), wd_s[...]) + bd_ref[0]

    @pl.when(r >= na_ref[0])
    def _():
        y_ref[...] = jnp.zeros(y_ref.shape, F32)


def _experts(tile_expert, n_active, xs, wgu, bgu, wd, bd):
    p, d = xs.shape
    _, _, dff2 = wgu.shape
    tm = MOE_TILE
    grid_spec = pltpu.PrefetchScalarGridSpec(
        num_scalar_prefetch=2,
        grid=(p // tm,),
        in_specs=[pl.BlockSpec((tm, d), lambda r, te, na: (r, 0)),
                  pl.BlockSpec((1, d, dff2), lambda r, te, na: (te[r], 0, 0)),
                  pl.BlockSpec((1, 1, dff2), lambda r, te, na: (te[r], 0, 0)),
                  pl.BlockSpec((1, dff2 // 2, d), lambda r, te, na: (te[r], 0, 0)),
                  pl.BlockSpec((1, 1, d), lambda r, te, na: (te[r], 0, 0))],
        out_specs=pl.BlockSpec((tm, d), lambda r, te, na: (r, 0)),
        scratch_shapes=[pltpu.VMEM((d, dff2), BF16), pltpu.VMEM((dff2 // 2, d), BF16)],
    )
    return pl.pallas_call(
        _experts_body,
        grid_spec=grid_spec,
        out_shape=jax.ShapeDtypeStruct((p, d), F32),
        compiler_params=pltpu.CompilerParams(dimension_semantics=("arbitrary",),
                                             vmem_limit_bytes=VMEM_LIMIT),
        name="moe_experts",
    )(tile_expert, n_active, xs, wgu, bgu, wd, bd)


def _combine_body(pos_ref, prob_ref, x1_ref, y_hbm, out_ref, buf, sem):
    tm = x1_ref.shape[0]

    def issue(i, carry):
        for k in range(TOP_K):
            p = pos_ref[i * TOP_K + k]
            pltpu.make_async_copy(y_hbm.at[pl.ds(p, 1)], buf.at[k, pl.ds(i, 1)], sem).start(
                priority=k % DMA_PRIORITIES)
        return carry

    lax.fori_loop(0, tm, issue, 0)
    for k in range(TOP_K):
        _wait_rows(buf.at[k], sem)
    prob = prob_ref[...]
    ffn = prob[:, 0:1] * buf[0]
    for k in range(1, TOP_K):
        ffn = ffn + prob[:, k:k + 1] * buf[k]
    out_ref[...] = x1_ref[...] + ffn


def _combine(pos, prob, x1, ys, tm):
    r, d = x1.shape
    return pl.pallas_call(
        _combine_body,
        grid=(r // tm,),
        in_specs=[pl.BlockSpec((tm * TOP_K,), lambda i: (i,), memory_space=pltpu.SMEM),
                  pl.BlockSpec((tm, LANES), lambda i: (i, 0)),
                  pl.BlockSpec((tm, d), lambda i: (i, 0)),
                  pl.BlockSpec(memory_space=pl.ANY)],
        out_specs=pl.BlockSpec((tm, d), lambda i: (i, 0)),
        out_shape=jax.ShapeDtypeStruct((r, d), F32),
        scratch_shapes=[pltpu.VMEM((TOP_K, tm, d), F32), pltpu.SemaphoreType.DMA],
        compiler_params=pltpu.CompilerParams(dimension_semantics=("arbitrary",),
                                             vmem_limit_bytes=VMEM_LIMIT),
        name="moe_combine",
    )(pos, prob, x1, ys)


def _moe(streams, wgu, bgu, wd, bd):
    tm = MOE_TILE
    counts = jnp.zeros((1, LANES), F32)
    ranks = []
    for (x1, tok, eidx, prob) in streams:
        rank, counts = _rank(eidx, counts, _row_tile(eidx.shape[0], 256))
        ranks.append(rank)
    cnt = counts[0, :N_EXPERTS].astype(I32)
    padded = ((cnt + tm - 1) // tm) * tm
    ends = jnp.cumsum(padded)
    offs = ends - padded
    n_pairs = sum(s[0].shape[0] for s in streams) * TOP_K
    n_tiles = n_pairs // tm + N_EXPERTS
    tile_start = jnp.arange(n_tiles, dtype=I32) * tm
    tile_expert = jnp.minimum(jnp.sum((ends[None, :] <= tile_start[:, None]).astype(I32), axis=1),
                              N_EXPERTS - 1)
    n_active = (ends[-1:] // tm).astype(I32)
    d = streams[0][0].shape[1]
    xs = jnp.zeros((n_tiles * tm, d), F32)
    poss = []
    for (x1, tok, eidx, prob), rank in zip(streams, ranks):
        pos = (offs[eidx[:, :TOP_K]] + rank[:, :TOP_K]).reshape(-1)
        poss.append(pos)
        xs = _dispatch(pos, tok, xs, _row_tile(tok.shape[0], 256))
    ys = _experts(tile_expert, n_active, xs, wgu, bgu, wd, bd)
    return [_combine(pos, prob, x1, ys, _row_tile(x1.shape[0], 256))
            for (x1, tok, eidx, prob), pos in zip(streams, poss)]


def _row_tile(r, want):
    tm = min(r, want)
    assert r % tm == 0, (r, tm)
    return tm


def _mixers(x2, bsz, t, pos, s0, caches, wts, layer):
    (g1, wm, ws, wg, qg, kg, lbp, ng) = wts
    r = x2.shape[0]
    tm = _row_tile(r, 256)
    assert t % tm == 0 or tm % t == 0
    tabs_a = _rope_tables(pos, HEAD_DIM, 1)
    tabs_b = _rope_tables(pos, IDX_DIM, LANES // IDX_DIM)
    tabs = tabs_a + tabs_b
    if tm > t:
        tabs = tuple(jnp.tile(tb, (tm // t, 1)) for tb in tabs)
    tk = max(tm, _row_tile(t, 512)) if caches is None else tm
    (aq, af, ai, ag, q, kf, kb, vf, vb, qim, kif, ki2, wi, ga, gb, vt) = _inproj(
        x2, g1, wm, ws, wg, qg, kg, tabs, tm, tk)
    oa, s_new = _hgrn(aq, af, ai, ag, s0, lbp, ng, bsz, t, layer)
    if caches is None:
        ob = _dsa_prompt(q, qim, wi, kb, vt, ki2, bsz, t, _row_tile(t, 256), tk)
    else:
        ck, cv, kidx2, past, cache_off = caches
        tk = _row_tile(past, 512)
        nsub = 2 if (past // tk) % 2 == 0 else 1
        ob = _dsa_sample(q, qim, wi, kb, vb, ki2, kidx2, ck, cv, bsz, t, past, tk, nsub, cache_off)
    return oa, ob, ga, gb, kf, vf, kif, s_new


def kernel(x_prompt, x_sample, cache_k, cache_v, cache_kidx, state_hgrn, norm1_g, w_in, lower_bounds, hgrn_norm_g, q_norm_g, k_norm_g, w_branch_a, w_branch_b, w_out, norm2_g, w_router, b_router, w_gate_up, b_gate_up, w_down, b_down):
    bp, tp, d = x_prompt.shape
    bs, ts, _ = x_sample.shape
    depth = w_in.shape[0]
    past = cache_k.shape[2]
    kvw = B_KV_HEADS * HEAD_DIM
    pos_p = jnp.arange(tp, dtype=I32)
    pos_s = past + jnp.arange(ts, dtype=I32)
    xp = x_prompt.reshape(bp * tp, d)
    xs = x_sample.reshape(bs * ts, d)
    n_main = 2 * A_HEADS * A_DK + 2 * A_HEADS * A_DV + B_HEADS * HEAD_DIM + 2 * kvw + IDX_HEADS * IDX_DIM
    n_small = n_main + IDX_DIM + IDX_HEADS
    outs = [[] for _ in range(8)]
    for l in range(depth):
        w = w_in[l]
        wm = w[:, :n_main].astype(BF16)
        w_ik = w[:, n_main:n_main + IDX_DIM]
        w_iw = w[:, n_main + IDX_DIM:n_small]
        ws = jnp.concatenate(
            [w_ik, w_ik, w_iw, jnp.zeros((d, LANES - IDX_HEADS), w.dtype)], axis=1).astype(BF16)
        wg = w[:, n_small:].astype(BF16)
        wts = (norm1_g[l][None], wm, ws, wg, q_norm_g[l][None], k_norm_g[l][None],
               lower_bounds, hgrn_norm_g[l][None])
        s0_p = jnp.zeros((bp, A_HEADS, A_DK, A_DV), F32)
        oa_p, ob_p, ga_p, gb_p, kp, vp, kip, sp = _mixers(xp, bp, tp, pos_p, s0_p, None, wts, l)
        kidx2 = jnp.concatenate([cache_kidx[l], cache_kidx[l]], axis=-1).astype(BF16)
        caches = (cache_k.reshape(-1, HEAD_DIM), cache_v.reshape(-1, HEAD_DIM),
                  kidx2.reshape(bs * past, LANES), past, l * bs)
        oa_s, ob_s, ga_s, gb_s, ks, vs, kis, ss = _mixers(
            xs, bs, ts, pos_s, state_hgrn[l], caches, wts, l)

        wa = w_branch_a[l].astype(BF16)
        wb = w_branch_b[l].astype(BF16)
        wo = w_out[l].astype(BF16)
        wr = jnp.pad(w_router[l], ((0, 0), (0, LANES - N_EXPERTS))).astype(BF16)
        br = jnp.pad(b_router[l], (0, LANES - N_EXPERTS))[None]
        g2 = norm2_g[l][None]
        bgu = b_gate_up[l][:, None, :]
        bd = b_down[l][:, None, :]
        streams = []
        for (x2, oa, ob, ga, gb) in ((xp, oa_p, ob_p, ga_p, gb_p), (xs, oa_s, ob_s, ga_s, gb_s)):
            r = x2.shape[0]
            streams.append(_post(oa, ob, ga, gb, x2, wa, wb, wo, g2, wr, br, _row_tile(r, 256)))
        xp, xs = _moe(streams, w_gate_up[l], bgu, w_down[l], bd)
        for lst, v in zip(outs, (kp.reshape(bp, tp, B_KV_HEADS, HEAD_DIM),
                                 vp.reshape(bp, tp, B_KV_HEADS, HEAD_DIM),
                                 kip.reshape(bp, tp, IDX_DIM), sp,
                                 ks.reshape(bs, ts, B_KV_HEADS, HEAD_DIM),
                                 vs.reshape(bs, ts, B_KV_HEADS, HEAD_DIM),
                                 kis.reshape(bs, ts, IDX_DIM), ss)):
            lst.append(v)
    return (xp.reshape(bp, tp, d), xs.reshape(bs, ts, d)) + tuple(jnp.stack(o) for o in outs)
```

```python
import functools

import jax
import jax.numpy as jnp
from jax import lax
from jax.experimental import pallas as pl
from jax.experimental.pallas import tpu as pltpu

F32 = jnp.float32
BF16 = jnp.bfloat16
I32 = jnp.int32

CHUNK = 64
A_HEADS = 8
A_DK = 128
A_DV = 128
B_HEADS = 8
B_KV_HEADS = 4
HEAD_DIM = 128
IDX_HEADS = 8
IDX_DIM = 64
IDX_TOPK_MAX = 256
ROPE_THETA = 10000.0
N_EXPERTS = 32
TOP_K = 4
SWIGLU_LIMIT = 7.0
SWIGLU_ALPHA = 1.702
EPS = 1e-6

LANES = 128
INT_MIN = -(2 ** 31)
NEG = -1e30
VMEM_LIMIT = 56 * 1024 * 1024


def _rms(x):
    return x * lax.rsqrt(jnp.mean(x * x, axis=-1, keepdims=True) + EPS)


def _silu(x):
    return x * jax.nn.sigmoid(x)


def _dot(a, b):
    return jnp.dot(a, b, preferred_element_type=F32)


def _dot_nt(a, b):
    return lax.dot_general(a, b, (((1,), (1,)), ((), ())), preferred_element_type=F32)


def _dot_tn(a, b):
    return lax.dot_general(a, b, (((0,), (0,)), ((), ())), preferred_element_type=F32)


def _const_spec(shape):
    zeros = (0,) * len(shape)
    return pl.BlockSpec(shape, lambda *_: zeros, pipeline_mode=pl.Buffered(1))


def _rope_tables(pos, d, reps):
    inv = 1.0 / (ROPE_THETA ** (jnp.arange(0, d, 2, dtype=F32) / d))
    ang = pos.astype(F32)[:, None] * inv[None, :]
    cos = jnp.cos(ang)
    sin = jnp.sin(ang)
    cos_t = jnp.concatenate([cos, cos] * reps, axis=-1)
    sin_t = jnp.concatenate([-sin, sin] * reps, axis=-1)
    return cos_t, sin_t


def _inproj_body(x_ref, g1_ref, wm_ref, ws_ref, wg_ref, qg_ref, kg_ref, ca_ref, sa_ref, cb_ref,
                 sb_ref, aq_o, af_o, ai_o, ag_o, q_o, kf_o, kb_o, vf_o, vb_o, qim_o, kif_o, ki2_o,
                 wi_o, ga_o, gb_o, vt_o):
    x = x_ref[...]
    tm = x.shape[0]
    hb = (_rms(x) * g1_ref[...]).astype(BF16)
    a_qk = A_HEADS * A_DK
    a_v = A_HEADS * A_DV
    b_q = B_HEADS * HEAD_DIM
    b_kv = B_KV_HEADS * HEAD_DIM
    iq_w = IDX_HEADS * IDX_DIM
    o = 0
    aq_o[...] = _dot(hb, wm_ref[:, o:o + a_qk]).astype(BF16)
    o += a_qk
    af_o[...] = _dot(hb, wm_ref[:, o:o + a_qk])
    o += a_qk
    ai_o[...] = _dot(hb, wm_ref[:, o:o + a_v]).astype(BF16)
    o += a_v
    ag_o[...] = _dot(hb, wm_ref[:, o:o + a_v]).astype(BF16)
    o += a_v

    ca = ca_ref[...]
    sa = sa_ref[...]

    def rope_head(y):
        return y * ca + pltpu.roll(y, HEAD_DIM // 2, 1) * sa

    zq = _dot(hb, wm_ref[:, o:o + b_q])
    o += b_q
    for h in range(B_HEADS):
        sl = slice(h * HEAD_DIM, (h + 1) * HEAD_DIM)
        q_o[:, sl] = (rope_head(_rms(zq[:, sl]) * qg_ref[...]) * Q_SCALE).astype(BF16)
    zk = _dot(hb, wm_ref[:, o:o + b_kv])
    o += b_kv
    for h in range(B_KV_HEADS):
        sl = slice(h * HEAD_DIM, (h + 1) * HEAD_DIM)
        y = rope_head(_rms(zk[:, sl]) * kg_ref[...])
        kf_o[pl.ds(h, tm, stride=B_KV_HEADS), :] = y
        kb_o[:, sl] = y.astype(BF16)
    zv = _dot(hb, wm_ref[:, o:o + b_kv])
    o += b_kv
    for h in range(B_KV_HEADS):
        vf_o[pl.ds(h, tm, stride=B_KV_HEADS), :] = zv[:, h * HEAD_DIM:(h + 1) * HEAD_DIM]
    vb_o[...] = zv.astype(BF16)
    vt_o[0] = zv.T.astype(BF16)

    cb = cb_ref[...]
    sb = sb_ref[...]
    lane = lax.broadcasted_iota(I32, (tm, LANES), 1)
    first_half = (lane & (IDX_DIM - 1)) < (IDX_DIM // 2)

    def rope_idx(y):
        partner = jnp.where(first_half, pltpu.roll(y, LANES - IDX_DIM // 2, 1),
                            pltpu.roll(y, IDX_DIM // 2, 1))
        return y * cb + partner * sb

    zi = _dot(hb, wm_ref[:, o:o + iq_w])
    for p in range(iq_w // LANES):
        y = rope_idx(zi[:, p * LANES:(p + 1) * LANES])
        qim_o[:, (2 * p) * LANES:(2 * p + 1) * LANES] = jnp.where(lane < IDX_DIM, y, 0.0).astype(BF16)
        qim_o[:, (2 * p + 1) * LANES:(2 * p + 2) * LANES] = jnp.where(lane >= IDX_DIM, y, 0.0).astype(BF16)
    zs = _dot(hb, ws_ref[...])
    y = rope_idx(zs[:, :LANES])
    ki2_o[...] = y.astype(BF16)
    kif_o[...] = y[:, :IDX_DIM]
    wi_o[...] = zs[:, LANES:] * (IDX_HEADS ** -0.5 * IDX_DIM ** -0.5)
    zg = _dot(hb, wg_ref[...])
    d = zg.shape[1] // 2
    ga_o[...] = zg[:, :d].astype(BF16)
    gb_o[...] = zg[:, d:].astype(BF16)


def _inproj(x2, g1, wm, ws, wg, qg, kg, tabs, tm, vt_tile):
    r, d = x2.shape
    ca, sa, cb, sb = tabs
    npos = ca.shape[0] // tm
    per = vt_tile // tm
    row = lambda w: pl.BlockSpec((tm, w), lambda i: (i, 0))
    tab = pl.BlockSpec((tm, LANES), lambda i: (i % npos, 0))
    a_qk = A_HEADS * A_DK
    b_q = B_HEADS * HEAD_DIM
    b_kv = B_KV_HEADS * HEAD_DIM
    kvh = B_KV_HEADS
    outs = [
        (1, a_qk, BF16), (1, a_qk, F32), (1, a_qk, BF16), (1, a_qk, BF16),
        (1, b_q, BF16), (kvh, HEAD_DIM, F32), (1, b_kv, BF16),
        (kvh, HEAD_DIM, F32), (1, b_kv, BF16),
        (1, IDX_HEADS * LANES, BF16), (1, IDX_DIM, F32), (1, LANES, BF16),
        (1, LANES, F32), (1, d, BF16), (1, d, BF16),
    ]
    return pl.pallas_call(
        _inproj_body,
        grid=(r // tm,),
        in_specs=[row(d), _const_spec(g1.shape), _const_spec(wm.shape), _const_spec(ws.shape),
                  _const_spec(wg.shape), _const_spec(qg.shape), _const_spec(kg.shape),
                  tab, tab, tab, tab],
        out_specs=[pl.BlockSpec((tm * k, w), lambda i: (i, 0)) for k, w, _ in outs] + [
            pl.BlockSpec((1, b_kv, tm), lambda i: (i // per, 0, i % per))],
        out_shape=[jax.ShapeDtypeStruct((r * k, w), dt) for k, w, dt in outs] + [
            jax.ShapeDtypeStruct((r // vt_tile, b_kv, vt_tile), BF16)],
        compiler_params=pltpu.CompilerParams(dimension_semantics=("arbitrary",),
                                             vmem_limit_bytes=VMEM_LIMIT),
        name=f"inproj_{r // tm}",
    )(x2, g1, wm, ws, wg, qg, kg, ca, sa, cb, sb)


HGRN_BATCH = 4


def _hgrn_body(aq_ref, af_ref, ai_ref, ag_ref, s0_ref, lbp_ref, ng_ref, oa_ref, sn_ref, st_ref,
               *, layer):
    c = pl.program_id(1)
    nc = pl.num_programs(1)
    nb = aq_ref.shape[0]

    @pl.when(c == 0)
    def _():
        for b in range(nb):
            for h in range(A_HEADS):
                st_ref[b, h] = s0_ref[b, h].T

    lbp = lbp_ref[...]
    e = jnp.exp(lbp - jnp.max(lbp, axis=0, keepdims=True))
    sm = e / jnp.sum(e, axis=0, keepdims=True)
    lb = jnp.sum(sm[:layer + 1], axis=0, keepdims=True)

    n = aq_ref.shape[1]
    r_i = lax.broadcasted_iota(I32, (n, n), 0)
    c_i = lax.broadcasted_iota(I32, (n, n), 1)
    causal = r_i >= c_i
    tri = jnp.where(causal, 1.0, 0.0).astype(BF16)
    prep = []
    for b in range(nb):
        f = lb + (1.0 - lb) * jax.nn.sigmoid(af_ref[b])
        lf = jnp.log(f)
        hi = lf.astype(BF16)
        r1 = lf - hi.astype(F32)
        mid = r1.astype(BF16)
        lo = (r1 - mid.astype(F32)).astype(BF16)
        cum = _dot(tri, hi) + _dot(tri, mid) + _dot(tri, lo)
        last = cum[n - 1:n, :]
        qd = _silu(aq_ref[b].astype(F32)) * jnp.exp(cum)
        k = 1.0 - f
        prep.append((qd, k * jnp.exp(-cum), k * jnp.exp(last - cum), jnp.exp(last),
                     _silu(ag_ref[b].astype(F32))))
    for h in range(A_HEADS):
        sl = slice(h * A_DK, (h + 1) * A_DK)
        for b, (qd, kd, kt, el, gate) in enumerate(prep):
            qd_h = qd[:, sl].astype(BF16)
            v_h = ai_ref[b, :, sl]
            att = jnp.where(causal, _dot_nt(qd_h, kd[:, sl].astype(BF16)), 0.0)
            st = st_ref[b, h]
            o = _dot(att.astype(BF16), v_h) + _dot_nt(qd_h, st.astype(BF16))
            st_ref[b, h] = st * el[:, sl] + _dot_tn(v_h, kt[:, sl].astype(BF16))
            oa_ref[b, :, sl] = (_rms(o) * ng_ref[...] * gate[:, sl]).astype(BF16)

    @pl.when(c == nc - 1)
    def _():
        for b in range(nb):
            for h in range(A_HEADS):
                sn_ref[b, h] = st_ref[b, h].T


def _hgrn(aq, af, ai, ag, s0, lbp, ng, bsz, t, layer):
    d = aq.shape[1]
    nc = t // CHUNK
    nb = _row_tile(bsz, HGRN_BATCH)
    seq = lambda a: a.reshape(bsz, t, d)
    row = pl.BlockSpec((nb, CHUNK, d), lambda g, c: (g, c, 0))
    st = pl.BlockSpec((nb, A_HEADS, A_DK, A_DV), lambda g, c: (g, 0, 0, 0))
    oa, s_new = pl.pallas_call(
        functools.partial(_hgrn_body, layer=layer),
        grid=(bsz // nb, nc),
        in_specs=[row, row, row, row, st, _const_spec(lbp.shape), _const_spec(ng.shape)],
        out_specs=[row, st],
        out_shape=[jax.ShapeDtypeStruct((bsz, t, d), BF16),
                   jax.ShapeDtypeStruct((bsz, A_HEADS, A_DK, A_DV), F32)],
        scratch_shapes=[pltpu.VMEM((nb, A_HEADS, A_DV, A_DK), F32)],
        compiler_params=pltpu.CompilerParams(dimension_semantics=("arbitrary", "arbitrary"),
                                             vmem_limit_bytes=VMEM_LIMIT),
        name=f"hgrn_{bsz // nb}x{nc}",
    )(seq(aq), seq(af), seq(ai), seq(ag), s0, lbp, ng)
    return oa.reshape(bsz * t, d), s_new


def _chunk_of(pos):
    return jnp.right_shift(pos, CHUNK.bit_length() - 1)


def _transpose_wi(wi, tq):
    if tq % LANES:
        wi = jnp.concatenate([wi, jnp.zeros((LANES - tq % LANES, LANES), F32)], axis=0)
    return wi.T[:IDX_HEADS, :tq]


def _score_tile(ki2_tile, qim_ref, wt):
    acc = None
    for h in range(IDX_HEADS):
        s = _dot_nt(ki2_tile, qim_ref[:, h * LANES:(h + 1) * LANES])
        term = jnp.maximum(s, 0.0) * wt[h:h + 1, :]
        acc = term if acc is None else acc + term
    return acc


def _mask_scores(score, adm):
    score = jnp.where(score == 0.0, 0.0, score)
    return score if adm is None else jnp.where(adm, score, -jnp.inf)


def _sort_key(score):
    bits = lax.bitcast_convert_type(score, I32)
    return jnp.where(bits < 0, bits ^ 0x7FFFFFFF, bits)


def _key_to_score(key):
    return lax.bitcast_convert_type(jnp.where(key < 0, key ^ 0x7FFFFFFF, key), F32)


SUBLANES = 8
COUNT_ACCS = 4
KEY_BITS = 32
PLANE_KEYS = SUBLANES * KEY_BITS


def _store_bit_planes(planes_ref, key, blk0):
    u = key ^ INT_MIN
    tk = u.shape[0]
    for blk in range(tk // PLANE_KEYS):
        a = [u[blk * PLANE_KEYS + i * SUBLANES:blk * PLANE_KEYS + (i + 1) * SUBLANES, :]
             for i in range(KEY_BITS)]
        for j, m in ((16, 0x0000FFFF), (8, 0x00FF00FF), (4, 0x0F0F0F0F), (2, 0x33333333),
                     (1, 0x55555555)):
            k = 0
            while k < KEY_BITS:
                t = (a[k] ^ lax.shift_right_logical(a[k + j], j)) & m
                a[k] = a[k] ^ t
                a[k + j] = a[k + j] ^ lax.shift_left(t, j)
                k = (k + j + 1) & ~j
        row = pl.multiple_of((blk0 + blk) * SUBLANES, SUBLANES)
        for p in range(KEY_BITS):
            planes_ref[p, pl.ds(row, SUBLANES), :] = a[p]


def _select(sc_ref, planes_ref, bias_ref, cut_ref, stat_ref, nkt, krow, tk, tq, nbits):
    nrow = planes_ref.shape[1]
    nblk = nkt * (tk // PLANE_KEYS)
    blk_of_row = jnp.right_shift(lax.broadcasted_iota(I32, (nrow, tq), 0),
                                 SUBLANES.bit_length() - 1)

    def word_count(words):
        pc = lax.population_count(words).reshape(nrow // SUBLANES, SUBLANES, tq)
        return jnp.sum(jnp.sum(pc, axis=0).astype(F32), axis=0, keepdims=True)

    def bit_body(i, carry):
        alive, above, thr_u = carry
        ones = alive & planes_ref[i]
        c1 = word_count(ones)
        take = above + c1 >= krow
        alive = jnp.where(take, ones, alive ^ ones)
        above = jnp.where(take, above, above + c1)
        thr_u = jnp.where(take, thr_u | jnp.left_shift(jnp.int32(1), KEY_BITS - 1 - i), thr_u)
        return alive, above, thr_u

    alive0 = jnp.where(blk_of_row < nblk, -1, 0).astype(I32)
    _, _, thr_u = lax.fori_loop(
        0, KEY_BITS, bit_body, (alive0, jnp.zeros((1, tq), F32), jnp.zeros((1, tq), I32)))

    nchunk = tk // SUBLANES
    sub = lax.broadcasted_iota(I32, (SUBLANES, tq), 0)

    def count(pred_fn):
        def body(j, accs):
            t = sc_ref[j]
            accs = list(accs)
            for c in range(nchunk):
                a = c % COUNT_ACCS
                accs[a] = pred_fn(t[c * SUBLANES:(c + 1) * SUBLANES, :], j, c, accs[a])
            return tuple(accs)
        zero = jnp.zeros((SUBLANES, tq), F32)
        accs = lax.fori_loop(0, nkt, body, (zero,) * COUNT_ACCS)
        tot = accs[0]
        for a in accs[1:]:
            tot = tot + a
        return jnp.sum(tot, axis=0, keepdims=True)

    def count_ge(cand):
        c8 = jnp.broadcast_to(cand, (SUBLANES, tq))
        return count(lambda t, j, c, acc: jnp.where(t >= c8, acc + 1.0, acc))

    def count_gt(cand):
        c8 = jnp.broadcast_to(cand, (SUBLANES, tq))
        return count(lambda t, j, c, acc: jnp.where(t > c8, acc + 1.0, acc))

    def record(thr):
        stat_ref[0:1, :] = thr
        stat_ref[1:2, :] = count_ge(thr)
        stat_ref[2:3, :] = count_gt(thr)

    record(_key_to_score(thr_u ^ INT_MIN))
    good = (stat_ref[2:3, :] < krow) & (stat_ref[1:2, :] >= krow)

    @pl.when(jnp.max(jnp.where(good, 0.0, 1.0)) > 0.0)
    def _():
        def cmp_body(i, tu):
            cand_u = tu | jnp.left_shift(jnp.int32(1), KEY_BITS - 1 - i)
            return jnp.where(count_ge(_key_to_score(cand_u ^ INT_MIN)) >= krow, cand_u, tu)
        tu = lax.fori_loop(0, KEY_BITS, cmp_body, jnp.zeros((1, tq), I32))
        record(_key_to_score(tu ^ INT_MIN))

    thr = stat_ref[0:1, :]
    n_ge = stat_ref[1:2, :]
    need = krow - stat_ref[2:3, :]

    cut_ref[...] = jnp.full(cut_ref.shape, 2 ** 31 - 1, I32)

    @pl.when(jnp.max(jnp.where(n_ge > krow, 1.0, 0.0)) > 0.0)
    def _():
        thr8 = jnp.broadcast_to(thr, (SUBLANES, tq))

        def idx_body(i, cut):
            cand = cut | jnp.left_shift(jnp.int32(1), nbits - 1 - i)
            c8 = jnp.broadcast_to(cand, (SUBLANES, tq))
            below = count(lambda t, j, c, acc: jnp.where(
                t == thr8, jnp.where(j * tk + c * SUBLANES + sub < c8, acc + 1.0, acc), acc))
            return jnp.where(below < need, cand, cut)
        cut_ref[...] = lax.fori_loop(0, nbits, idx_body, jnp.zeros((1, tq), I32))

    cut = cut_ref[...]

    def write(j, carry):
        t = sc_ref[j]
        idx = j * tk + lax.broadcasted_iota(I32, (tk, tq), 0)
        tie = jnp.where(t == thr, jnp.where(idx <= cut, 0.0, NEG), NEG)
        bias_ref[j] = jnp.where(t > thr, 0.0, tie)
        return carry

    lax.fori_loop(0, nkt, write, 0)


def _stack_heads(q_ref, u, stack):
    heads = [q_ref[:, h * HEAD_DIM:(h + 1) * HEAD_DIM] for h in range(u * stack, (u + 1) * stack)]
    return heads[0] if stack == 1 else jnp.concatenate(heads, axis=0)


def _store_heads(o_ref, u, stack, o, tq):
    for r in range(stack):
        h = u * stack + r
        o_ref[:, h * HEAD_DIM:(h + 1) * HEAD_DIM] = o[r * tq:(r + 1) * tq].astype(o_ref.dtype)


Q_SCALE = HEAD_DIM ** -0.5 * 1.4426950408889634
SUM_ROWS = 16


def _attn_reset(m_ref, l_ref, acc_ref):
    m_ref[...] = jnp.full(m_ref.shape, NEG, F32)
    l_ref[...] = jnp.zeros(l_ref.shape, F32)
    acc_ref[...] = jnp.zeros(acc_ref.shape, F32)


def _attn_logits(q_ref, u, stack, kt, bias, m_ref):
    s = _dot_nt(kt, _stack_heads(q_ref, u, stack)) + jnp.concatenate([bias] * stack, axis=1)
    return s, jnp.maximum(m_ref[u], jnp.max(s, axis=0, keepdims=True))


def _attn_accumulate(u, s, m_new, m_ref, l_ref, acc_ref, pv):
    alpha = jnp.exp2(m_ref[u] - m_new)
    p = jnp.exp2(s - m_new)
    if l_ref is not None:
        l_ref[u] = alpha * l_ref[u] + jnp.sum(p, axis=0, keepdims=True)
    acc_ref[u] = alpha * acc_ref[u] + pv(p.astype(BF16))
    m_ref[u] = m_new


def _attn_step(q_ref, u, stack, kt, bias, m_ref, l_ref, acc_ref, pv):
    s, m_new = _attn_logits(q_ref, u, stack, kt, bias, m_ref)
    _attn_accumulate(u, s, m_new, m_ref, l_ref, acc_ref, pv)


def _attn_finish(o_ref, u, stack, l_ref, acc_ref, tq):
    acc = acc_ref[u]
    l = acc[HEAD_DIM:HEAD_DIM + 1] if l_ref is None else l_ref[u]
    _store_heads(o_ref, u, stack, (acc[:HEAD_DIM] / l).T, tq)


def _dsa_prompt_body(q_ref, qim_ref, wi_ref, kb_ref, vt_ref, ki2_ref, o_ref, sc_ref, planes_ref,
                     bias_ref, cut_ref, stat_ref, m_ref, l_ref, acc_ref,
                     *, tq, tk, topk, nbits):
    q0 = pl.program_id(1) * tq
    nkt = lax.div(q0 + tq + tk - 1, tk)
    wt = _transpose_wi(wi_ref[...], tq)
    qchunk = _chunk_of(q0 + lax.broadcasted_iota(I32, (1, tq), 1))

    @pl.when((pl.program_id(0) == 0) & (pl.program_id(1) == 0))
    def _():
        planes_ref[...] = jnp.zeros(planes_ref.shape, I32)

    def score_body(j, carry):
        ks = pl.multiple_of(j * tk, tk)
        score = _score_tile(ki2_ref[pl.ds(ks, tk), :], qim_ref, wt)
        kpos = ks + lax.broadcasted_iota(I32, (tk, tq), 0)
        score = _mask_scores(score, _chunk_of(kpos) <= qchunk)
        sc_ref[j] = score
        _store_bit_planes(planes_ref, _sort_key(score), j * (tk // PLANE_KEYS))
        return carry

    lax.fori_loop(0, nkt, score_body, 0)

    krow = jnp.minimum((qchunk + 1) * CHUNK, topk).astype(F32)
    _select(sc_ref, planes_ref, bias_ref, cut_ref, stat_ref, nkt, krow, tk, tq, nbits)

    _attn_reset(m_ref, l_ref, acc_ref)

    rep = B_HEADS // B_KV_HEADS
    ones = jnp.ones((SUM_ROWS, tk), BF16)

    def att_body(j, carry):
        ks = pl.multiple_of(j * tk, tk)
        bias = bias_ref[j]
        gsl = lambda g: slice(g * HEAD_DIM, (g + 1) * HEAD_DIM)
        logits = lambda g: _attn_logits(q_ref, g, rep, kb_ref[pl.ds(ks, tk), gsl(g)], bias, m_ref)
        pending = logits(0)
        for g in range(B_KV_HEADS):
            nxt = logits(g + 1) if g + 1 < B_KV_HEADS else None
            vt = jnp.concatenate([vt_ref[j, gsl(g), :], ones], axis=0)
            _attn_accumulate(g, *pending, m_ref, None, acc_ref, lambda p, vt=vt: _dot(vt, p))
            pending = nxt
        return carry

    lax.fori_loop(0, nkt, att_body, 0)
    for g in range(B_KV_HEADS):
        _attn_finish(o_ref, g, rep, None, acc_ref, tq)


def _dsa_scratch(nt, tk, tq, stack, acc_rows):
    units = B_HEADS // stack
    return [pltpu.VMEM((nt, tk, tq), F32),
            pltpu.VMEM((KEY_BITS, nt * tk // KEY_BITS, tq), I32),
            pltpu.VMEM((nt, tk, tq), F32),
            pltpu.VMEM((1, tq), I32),
            pltpu.VMEM((SUBLANES, tq), F32),
            pltpu.VMEM((units, 1, stack * tq), F32),
            pltpu.VMEM((units, 1, stack * tq), F32),
            pltpu.VMEM((units, acc_rows, stack * tq), F32)]


def _dsa_prompt(q, qim, wi, kb, vt, ki2, bsz, t, tq, tk):
    nq = t // tq
    nt = t // tk
    topk = min(IDX_TOPK_MAX, t // 4)
    nbits = max(1, (t - 1).bit_length())
    qrow = lambda w: pl.BlockSpec((tq, w), lambda b, i: (b * nq + i, 0))
    seq = lambda w: pl.BlockSpec((t, w), lambda b, i: (b, 0))
    return pl.pallas_call(
        functools.partial(_dsa_prompt_body, tq=tq, tk=tk, topk=topk, nbits=nbits),
        grid=(bsz, nq),
        in_specs=[qrow(q.shape[1]), qrow(qim.shape[1]), qrow(LANES), seq(kb.shape[1]),
                  pl.BlockSpec((nt, vt.shape[1], tk), lambda b, i: (b, 0, 0)), seq(LANES)],
        out_specs=qrow(q.shape[1]),
        out_shape=jax.ShapeDtypeStruct(q.shape, BF16),
        scratch_shapes=_dsa_scratch(nt, tk, tq, B_HEADS // B_KV_HEADS, HEAD_DIM + SUM_ROWS),
        compiler_params=pltpu.CompilerParams(dimension_semantics=("arbitrary", "arbitrary"),
                                             vmem_limit_bytes=VMEM_LIMIT),
        name="dsa_prompt",
    )(q, qim, wi, kb, vt, ki2)


def _dsa_sample_body(q_ref, qim_ref, wi_ref, kn_ref, vn_ref, ki2n_ref, kidx2_ref, ck_ref, cv_ref,
                     o_ref, sc_ref, planes_ref, bias_ref, cut_ref, stat_ref, m_ref, l_ref, acc_ref,
                     *, tq, tk, nsub, npt, past, topk, nbits):
    j = pl.program_id(1)
    nj = pl.num_programs(1)
    pad = jnp.zeros((tk - tq, LANES), BF16)
    nb = tk // PLANE_KEYS

    @pl.when(j == 0)
    def _():
        wt = _transpose_wi(wi_ref[...], tq)

        def score_body(jt, carry):
            ks = pl.multiple_of(jt * tk, tk)
            score = _mask_scores(_score_tile(kidx2_ref[pl.ds(ks, tk), :], qim_ref, wt), None)
            sc_ref[jt] = score
            _store_bit_planes(planes_ref, _sort_key(score), jt * nb)
            return carry

        lax.fori_loop(0, npt, score_body, 0)
        ki2n = jnp.concatenate([ki2n_ref[...], pad], axis=0)
        krow_i = lax.broadcasted_iota(I32, (tk, tq), 0)
        qchunk = _chunk_of(past + lax.broadcasted_iota(I32, (1, tq), 1))
        adm = jnp.where(krow_i < tq, _chunk_of(past + krow_i), 2 ** 30) <= qchunk
        score = _mask_scores(_score_tile(ki2n, qim_ref, wt), adm)
        sc_ref[npt] = score
        _store_bit_planes(planes_ref, _sort_key(score), npt * nb)
        krow = jnp.minimum((qchunk + 1) * CHUNK, topk).astype(F32)
        _select(sc_ref, planes_ref, bias_ref, cut_ref, stat_ref, npt + 1, krow, tk, tq, nbits)
        _attn_reset(m_ref, l_ref, acc_ref)

    rep = B_HEADS // B_KV_HEADS

    def update(g, kt, vt, bias):
        _attn_step(q_ref, g, rep, kt, bias, m_ref, l_ref, acc_ref, lambda p: _dot_tn(vt, p))

    for su in range(nsub):
        bias = bias_ref[j * nsub + su]
        rows = lambda g: pl.ds(su * tk * B_KV_HEADS + g, tk, stride=B_KV_HEADS)
        logits = lambda g: _attn_logits(q_ref, g, rep, ck_ref[rows(g), :].astype(BF16), bias,
                                        m_ref)
        pending = logits(0)
        for g in range(B_KV_HEADS):
            nxt = logits(g + 1) if g + 1 < B_KV_HEADS else None
            vt = cv_ref[rows(g), :].astype(BF16)
            _attn_accumulate(g, *pending, m_ref, l_ref, acc_ref, lambda p, vt=vt: _dot_tn(vt, p))
            pending = nxt

    @pl.when(j == nj - 1)
    def _():
        bias = bias_ref[npt]
        for g in range(B_KV_HEADS):
            gs = slice(g * HEAD_DIM, (g + 1) * HEAD_DIM)
            kt = jnp.concatenate([kn_ref[:, gs], pad], axis=0)
            vt = jnp.concatenate([vn_ref[:, gs], pad], axis=0)
            update(g, kt, vt, bias)
            _attn_finish(o_ref, g, rep, l_ref, acc_ref, tq)


def _dsa_sample(q, qim, wi, kn, vn, ki2n, kidx2, ck, cv, bsz, t, past, tk, nsub, cache_off):
    tq = t
    npt = past // tk
    nj = npt // nsub
    topk = min(IDX_TOPK_MAX, (past + t) // 4)
    nbits = max(1, (past + t - 1).bit_length())
    rep = B_HEADS // B_KV_HEADS
    qrow = lambda w: pl.BlockSpec((tq, w), lambda b, j: (b, 0))
    kvw = kn.shape[1]
    cache = pl.BlockSpec((nsub * tk * B_KV_HEADS, HEAD_DIM),
                         lambda b, j: ((cache_off + b) * nj + j, 0))
    return pl.pallas_call(
        functools.partial(_dsa_sample_body, tq=tq, tk=tk, nsub=nsub, npt=npt, past=past,
                          topk=topk, nbits=nbits),
        grid=(bsz, nj),
        in_specs=[qrow(q.shape[1]), qrow(qim.shape[1]), qrow(LANES), qrow(kvw), qrow(kvw),
                  qrow(LANES), pl.BlockSpec((past, LANES), lambda b, j: (b, 0)), cache, cache],
        out_specs=qrow(q.shape[1]),
        out_shape=jax.ShapeDtypeStruct(q.shape, BF16),
        scratch_shapes=_dsa_scratch(npt + 1, tk, tq, rep, HEAD_DIM),
        compiler_params=pltpu.CompilerParams(dimension_semantics=("arbitrary", "arbitrary"),
                                             vmem_limit_bytes=VMEM_LIMIT),
        name="dsa_sample",
    )(q, qim, wi, kn, vn, ki2n, kidx2, ck, cv)


def _post_body(oa_ref, ob_ref, ga_ref, gb_ref, x_ref, wa_ref, wb_ref, wo_ref, g2_ref, wr_ref,
               br_ref, x1_o, t_o, eidx_o, prob_o, cnt_o):
    ya = _dot(oa_ref[...], wa_ref[...])
    yb = _dot(ob_ref[...], wb_ref[...])
    merged = (jax.nn.sigmoid(ga_ref[...].astype(F32)) * ya
              + jax.nn.sigmoid(gb_ref[...].astype(F32)) * yb)
    x1 = x_ref[...] + _dot(merged.astype(BF16), wo_ref[...])
    x1_o[...] = x1
    tok = _rms(x1) * g2_ref[...]
    t_o[...] = tok
    logits = _dot(tok.astype(BF16), wr_ref[...]) + br_ref[...]
    tm = logits.shape[0]
    lane = lax.broadcasted_iota(I32, (tm, LANES), 1).astype(F32)
    cur = jnp.where(lane < N_EXPERTS, logits, -jnp.inf)
    top = None
    den = jnp.zeros((tm, 1), F32)
    eidx = jnp.zeros((tm, LANES), F32)
    prob = jnp.zeros((tm, LANES), F32)
    chosen = jnp.zeros((tm, LANES), F32)
    for k in range(TOP_K):
        mx = jnp.max(cur, axis=1, keepdims=True)
        first = jnp.min(jnp.where(cur == mx, lane, float(LANES)), axis=1, keepdims=True)
        if top is None:
            top = mx
        e = jnp.exp(mx - top)
        den = den + e
        eidx = jnp.where(lane == k, first, eidx)
        prob = jnp.where(lane == k, e, prob)
        hit = lane == first
        chosen = jnp.where(hit, 1.0, chosen)
        cur = jnp.where(hit, -jnp.inf, cur)
    eidx_o[...] = eidx.astype(I32)
    prob_o[...] = prob / den

    @pl.when(pl.program_id(0) == 0)
    def _():
        cnt_o[...] = jnp.zeros(cnt_o.shape, F32)

    cnt_o[...] += jnp.sum(chosen, axis=0, keepdims=True)


def _post(oa, ob, ga, gb, x2, wa, wb, wo, g2, wr, br, tm):
    r, d = x2.shape
    row = lambda w: pl.BlockSpec((tm, w), lambda i: (i, 0))
    return pl.pallas_call(
        _post_body,
        grid=(r // tm,),
        in_specs=[row(d), row(d), row(d), row(d), row(d), _const_spec(wa.shape),
                  _const_spec(wb.shape), _const_spec(wo.shape), _const_spec(g2.shape),
                  _const_spec(wr.shape), _const_spec(br.shape)],
        out_specs=[row(d), row(d), row(LANES), row(LANES),
                   pl.BlockSpec((1, LANES), lambda i: (0, 0))],
        out_shape=[jax.ShapeDtypeStruct((r, d), F32), jax.ShapeDtypeStruct((r, d), F32),
                   jax.ShapeDtypeStruct((r, LANES), I32), jax.ShapeDtypeStruct((r, LANES), F32),
                   jax.ShapeDtypeStruct((1, LANES), F32)],
        compiler_params=pltpu.CompilerParams(dimension_semantics=("arbitrary",),
                                             vmem_limit_bytes=VMEM_LIMIT),
        name=f"post_{r // tm}",
    )(oa, ob, ga, gb, x2, wa, wb, wo, g2, wr, br)


MOE_TILE = 256
DMA_PRIORITIES = 2
ISSUE_UNROLL = 8


def _rank_body(eidx_ref, cin_ref, pos_o, carry_ref):
    @pl.when(pl.program_id(0) == 0)
    def _():
        carry_ref[...] = cin_ref[...]

    eidx = eidx_ref[...]
    tm = eidx.shape[0]
    lane = lax.broadcasted_iota(I32, (tm, LANES), 1)
    hits = [lane == eidx[:, k:k + 1] for k in range(TOP_K)]
    onehot = jnp.zeros((tm, LANES), F32)
    for hit in hits:
        onehot = onehot + jnp.where(hit, 1.0, 0.0)
    r_i = lax.broadcasted_iota(I32, (tm, tm), 0)
    c_i = lax.broadcasted_iota(I32, (tm, tm), 1)
    before = jnp.where(c_i < r_i, 1.0, 0.0).astype(BF16)
    base = carry_ref[...] + _dot(before, onehot.astype(BF16))
    rank = jnp.zeros((tm, LANES), F32)
    for k, hit in enumerate(hits):
        rk = jnp.sum(jnp.where(hit, base, 0.0), axis=1, keepdims=True)
        rank = jnp.where(lane == k, rk, rank)
    pos_o[...] = rank.T[:SUBLANES, :].astype(I32)
    carry_ref[...] += jnp.sum(onehot, axis=0, keepdims=True)


def _rank(eidx, cin, tm):
    r = eidx.shape[0]
    return pl.pallas_call(
        _rank_body,
        grid=(r // tm,),
        in_specs=[pl.BlockSpec((tm, LANES), lambda i: (i, 0)), _const_spec(cin.shape)],
        out_specs=pl.BlockSpec((SUBLANES, tm), lambda i: (0, i)),
        out_shape=jax.ShapeDtypeStruct((SUBLANES, r), I32),
        scratch_shapes=[pltpu.VMEM((1, LANES), F32)],
        compiler_params=pltpu.CompilerParams(dimension_semantics=("arbitrary",)),
        name=f"moe_rank_{r // tm}",
    )(eidx, cin)


def _wait_rows(ref, sem, times):
    for _ in range(times):
        pltpu.make_async_copy(ref, ref, sem).wait()


def _dispatch_body(pos_ref, tok_ref, xs_in, xs_out, ring, sems):
    del xs_in
    i = pl.program_id(0)
    tm = tok_ref.shape[0]

    last = pl.num_programs(0) - 1
    for slot in range(2):
        mine = i % 2 == slot

        @pl.when(mine)
        def _(slot=slot):
            ring[slot] = tok_ref[...]

            def issue(t, carry):
                for k in range(TOP_K):
                    pltpu.make_async_copy(ring.at[slot, pl.ds(t, 1)],
                                          xs_out.at[pl.ds(pos_ref[k, t], 1)],
                                          sems.at[slot]).start(priority=k % DMA_PRIORITIES)
                return carry

            lax.fori_loop(0, tm, issue, 0)

        @pl.when(mine & (i > 0))
        def _(slot=slot):
            _wait_rows(ring.at[1 - slot], sems.at[1 - slot], TOP_K)

        @pl.when(mine & (i == last))
        def _(slot=slot):
            _wait_rows(ring.at[slot], sems.at[slot], TOP_K)


def _dispatch(pos, tok, xs, tm):
    r, d = tok.shape
    return pl.pallas_call(
        _dispatch_body,
        grid=(r // tm,),
        in_specs=[pl.BlockSpec((SUBLANES, tm), lambda i: (0, i), memory_space=pltpu.SMEM),
                  pl.BlockSpec((tm, d), lambda i: (i, 0)),
                  pl.BlockSpec(memory_space=pl.ANY)],
        out_specs=pl.BlockSpec(memory_space=pl.ANY),
        out_shape=jax.ShapeDtypeStruct(xs.shape, xs.dtype),
        scratch_shapes=[pltpu.VMEM((2, tm, d), tok.dtype), pltpu.SemaphoreType.DMA((2,))],
        input_output_aliases={2: 0},
        compiler_params=pltpu.CompilerParams(dimension_semantics=("arbitrary",),
                                             has_side_effects=True),
        name=f"moe_dispatch_{r // tm}",
    )(pos, tok, xs)


def _experts_body(te_ref, na_ref, x_ref, wgu_ref, bgu_ref, wd_ref, bd_ref, y_ref, wgu_s, wd_s):
    r = pl.program_id(0)
    e = te_ref[r]
    prev = te_ref[jnp.maximum(r - 1, 0)]

    @pl.when((r == 0) | (e != prev))
    def _():
        rows = 128
        for c in range(wgu_s.shape[0] // rows):
            wgu_s[c * rows:(c + 1) * rows, :] = wgu_ref[0, c * rows:(c + 1) * rows, :].astype(BF16)
        for c in range(wd_s.shape[0] // rows):
            wd_s[c * rows:(c + 1) * rows, :] = wd_ref[0, c * rows:(c + 1) * rows, :].astype(BF16)

    @pl.when(r < na_ref[0])
    def _():
        gu = _dot(x_ref[...].astype(BF16), wgu_s[...]) + bgu_ref[0]
        dff = gu.shape[1] // 2
        gate = jnp.minimum(gu[:, :dff], SWIGLU_LIMIT)
        up = jnp.clip(gu[:, dff:], -SWIGLU_LIMIT, SWIGLU_LIMIT)
        act = (up + 1.0) * gate * jax.nn.sigmoid(SWIGLU_ALPHA * gate)
        y_ref[...] = _dot(act.astype(BF16), wd_s[...]) + bd_ref[0]

    @pl.when(r >= na_ref[0])
    def _():
        y_ref[...] = jnp.zeros(y_ref.shape, F32)


def _experts(tile_expert, n_active, xs, wgu, bgu, wd, bd):
    p, d = xs.shape
    _, _, dff2 = wgu.shape
    tm = MOE_TILE
    grid_spec = pltpu.PrefetchScalarGridSpec(
        num_scalar_prefetch=2,
        grid=(p // tm,),
        in_specs=[pl.BlockSpec((tm, d), lambda r, te, na: (r, 0)),
                  pl.BlockSpec((1, d, dff2), lambda r, te, na: (te[r], 0, 0)),
                  pl.BlockSpec((1, 1, dff2), lambda r, te, na: (te[r], 0, 0)),
                  pl.BlockSpec((1, dff2 // 2, d), lambda r, te, na: (te[r], 0, 0)),
                  pl.BlockSpec((1, 1, d), lambda r, te, na: (te[r], 0, 0))],
        out_specs=pl.BlockSpec((tm, d), lambda r, te, na: (r, 0)),
        scratch_shapes=[pltpu.VMEM((d, dff2), BF16), pltpu.VMEM((dff2 // 2, d), BF16)],
    )
    return pl.pallas_call(
        _experts_body,
        grid_spec=grid_spec,
        out_shape=jax.ShapeDtypeStruct((p, d), F32),
        compiler_params=pltpu.CompilerParams(dimension_semantics=("arbitrary",),
                                             vmem_limit_bytes=VMEM_LIMIT),
        name="moe_experts",
    )(tile_expert, n_active, xs, wgu, bgu, wd, bd)


def _combine_body(pos_ref, prob_ref, x1_ref, y_hbm, out_ref, buf, sems):
    i = pl.program_id(0)
    n = pl.num_programs(0) - 1
    tm = x1_ref.shape[0]

    for slot in range(2):
        mine = i % 2 == slot

        @pl.when(mine & (i < n))
        def _(slot=slot):
            def issue(t, carry):
                for k in range(TOP_K):
                    pltpu.make_async_copy(y_hbm.at[pl.ds(pos_ref[k, t], 1)],
                                          buf.at[slot, k, pl.ds(t, 1)],
                                          sems.at[slot]).start(priority=k % DMA_PRIORITIES)
                return carry
            lax.fori_loop(0, tm, issue, 0)

        @pl.when(mine & (i > 0))
        def _(slot=slot):
            _wait_rows(buf.at[1 - slot, 0], sems.at[1 - slot], TOP_K)
            prob = prob_ref[...]
            ffn = prob[:, 0:1] * buf[1 - slot, 0]
            for k in range(1, TOP_K):
                ffn = ffn + prob[:, k:k + 1] * buf[1 - slot, k]
            out_ref[...] = x1_ref[...] + ffn


def _combine(pos, prob, x1, ys, tm):
    r, d = x1.shape
    n = r // tm
    prev = lambda w: pl.BlockSpec((tm, w), lambda i: (jnp.maximum(i - 1, 0), 0))
    return pl.pallas_call(
        _combine_body,
        grid=(n + 1,),
        in_specs=[pl.BlockSpec((SUBLANES, tm), lambda i: (0, jnp.minimum(i, n - 1)),
                               memory_space=pltpu.SMEM),
                  prev(LANES), prev(d), pl.BlockSpec(memory_space=pl.ANY)],
        out_specs=prev(d),
        out_shape=jax.ShapeDtypeStruct((r, d), F32),
        scratch_shapes=[pltpu.VMEM((2, TOP_K, tm, d), F32), pltpu.SemaphoreType.DMA((2,))],
        compiler_params=pltpu.CompilerParams(dimension_semantics=("arbitrary",),
                                             vmem_limit_bytes=VMEM_LIMIT),
        name=f"moe_combine_{n}",
    )(pos, prob, x1, ys)


def _moe(streams, wgu, bgu, wd, bd):
    tm = MOE_TILE
    counts = [s[4][0, :N_EXPERTS].astype(I32) for s in streams]
    cnt = sum(counts)
    padded = ((cnt + tm - 1) // tm) * tm
    ends = jnp.cumsum(padded)
    n_pairs = sum(s[0].shape[0] for s in streams) * TOP_K
    n_tiles = n_pairs // tm + N_EXPERTS
    tile_start = jnp.arange(n_tiles, dtype=I32) * tm
    tile_expert = jnp.minimum(jnp.sum((ends[None, :] <= tile_start[:, None]).astype(I32), axis=1),
                              N_EXPERTS - 1)
    n_active = (ends[-1:] // tm).astype(I32)
    d = streams[0][0].shape[1]
    xs = jnp.zeros((n_tiles * tm, d), F32)
    start = ends - padded
    poss = []
    for (x1, tok, eidx, prob, _), c in zip(streams, counts):
        cin = jnp.pad(start, (0, LANES - N_EXPERTS)).astype(F32)[None]
        pos = _rank(eidx, cin, _row_tile(eidx.shape[0], 256))
        poss.append(pos)
        xs = _dispatch(pos, tok, xs, _row_tile(tok.shape[0], 256))
        start = start + c
    ys = _experts(tile_expert, n_active, xs, wgu, bgu, wd, bd)
    return [_combine(pos, s[3], s[0], ys, _row_tile(s[0].shape[0], 256))
            for s, pos in zip(streams, poss)]


def _row_tile(r, want):
    tm = min(r, want)
    assert r % tm == 0, (r, tm)
    return tm


def _mixers(x2, bsz, t, pos, s0, caches, wts, layer):
    (g1, wm, ws, wg, qg, kg, lbp, ng) = wts
    r = x2.shape[0]
    tm = _row_tile(r, 256)
    assert t % tm == 0 or tm % t == 0
    tabs_a = _rope_tables(pos, HEAD_DIM, 1)
    tabs_b = _rope_tables(pos, IDX_DIM, LANES // IDX_DIM)
    tabs = tabs_a + tabs_b
    if tm > t:
        tabs = tuple(jnp.tile(tb, (tm // t, 1)) for tb in tabs)
    tk = max(tm, _row_tile(t, 512)) if caches is None else tm
    (aq, af, ai, ag, q, kf, kb, vf, vb, qim, kif, ki2, wi, ga, gb, vt) = _inproj(
        x2, g1, wm, ws, wg, qg, kg, tabs, tm, tk)
    oa, s_new = _hgrn(aq, af, ai, ag, s0, lbp, ng, bsz, t, layer)
    if caches is None:
        ob = _dsa_prompt(q, qim, wi, kb, vt, ki2, bsz, t, _row_tile(t, 256), tk)
    else:
        ck, cv, kidx2, past, cache_off = caches
        tk = _row_tile(past, 512)
        nsub = 2 if (past // tk) % 2 == 0 else 1
        ob = _dsa_sample(q, qim, wi, kb, vb, ki2, kidx2, ck, cv, bsz, t, past, tk, nsub, cache_off)
    return oa, ob, ga, gb, kf, vf, kif, s_new


def kernel(x_prompt, x_sample, cache_k, cache_v, cache_kidx, state_hgrn, norm1_g, w_in, lower_bounds, hgrn_norm_g, q_norm_g, k_norm_g, w_branch_a, w_branch_b, w_out, norm2_g, w_router, b_router, w_gate_up, b_gate_up, w_down, b_down):
    bp, tp, d = x_prompt.shape
    bs, ts, _ = x_sample.shape
    depth = w_in.shape[0]
    past = cache_k.shape[2]
    kvw = B_KV_HEADS * HEAD_DIM
    pos_p = jnp.arange(tp, dtype=I32)
    pos_s = past + jnp.arange(ts, dtype=I32)
    xp = x_prompt.reshape(bp * tp, d)
    xs = x_sample.reshape(bs * ts, d)
    n_main = 2 * A_HEADS * A_DK + 2 * A_HEADS * A_DV + B_HEADS * HEAD_DIM + 2 * kvw + IDX_HEADS * IDX_DIM
    n_small = n_main + IDX_DIM + IDX_HEADS
    outs = [[] for _ in range(8)]
    for l in range(depth):
        w = w_in[l]
        wm = w[:, :n_main].astype(BF16)
        w_ik = w[:, n_main:n_main + IDX_DIM]
        w_iw = w[:, n_main + IDX_DIM:n_small]
        ws = jnp.concatenate(
            [w_ik, w_ik, w_iw, jnp.zeros((d, LANES - IDX_HEADS), w.dtype)], axis=1).astype(BF16)
        wg = w[:, n_small:].astype(BF16)
        wts = (norm1_g[l][None], wm, ws, wg, q_norm_g[l][None], k_norm_g[l][None],
               lower_bounds, hgrn_norm_g[l][None])
        s0_p = jnp.zeros((bp, A_HEADS, A_DK, A_DV), F32)
        oa_p, ob_p, ga_p, gb_p, kp, vp, kip, sp = _mixers(xp, bp, tp, pos_p, s0_p, None, wts, l)
        kidx2 = jnp.concatenate([cache_kidx[l], cache_kidx[l]], axis=-1).astype(BF16)
        caches = (cache_k.reshape(-1, HEAD_DIM), cache_v.reshape(-1, HEAD_DIM),
                  kidx2.reshape(bs * past, LANES), past, l * bs)
        oa_s, ob_s, ga_s, gb_s, ks, vs, kis, ss = _mixers(
            xs, bs, ts, pos_s, state_hgrn[l], caches, wts, l)

        wa = w_branch_a[l].astype(BF16)
        wb = w_branch_b[l].astype(BF16)
        wo = w_out[l].astype(BF16)
        wr = jnp.pad(w_router[l], ((0, 0), (0, LANES - N_EXPERTS))).astype(BF16)
        br = jnp.pad(b_router[l], (0, LANES - N_EXPERTS))[None]
        g2 = norm2_g[l][None]
        bgu = b_gate_up[l][:, None, :]
        bd = b_down[l][:, None, :]
        streams = []
        for (x2, oa, ob, ga, gb) in ((xp, oa_p, ob_p, ga_p, gb_p), (xs, oa_s, ob_s, ga_s, gb_s)):
            r = x2.shape[0]
            streams.append(_post(oa, ob, ga, gb, x2, wa, wb, wo, g2, wr, br, _row_tile(r, 256)))
        xp, xs = _moe(streams, w_gate_up[l], bgu, w_down[l], bd)
        for lst, v in zip(outs, (kp.reshape(bp, tp, B_KV_HEADS, HEAD_DIM),
                                 vp.reshape(bp, tp, B_KV_HEADS, HEAD_DIM),
                                 kip.reshape(bp, tp, IDX_DIM), sp,
                                 ks.reshape(bs, ts, B_KV_HEADS, HEAD_DIM),
                                 vs.reshape(bs, ts, B_KV_HEADS, HEAD_DIM),
                                 kis.reshape(bs, ts, IDX_DIM), ss)):
            lst.append(v)
    return (xp.reshape(bp, tp, d), xs.reshape(bs, ts, d)) + tuple(jnp.stack(o) for o in outs)
```

```python
import functools

import jax
import jax.numpy as jnp
from jax import lax
from jax.experimental import pallas as pl
from jax.experimental.pallas import tpu as pltpu

F32 = jnp.float32
BF16 = jnp.bfloat16
I32 = jnp.int32

CHUNK = 64
A_HEADS = 8
A_DK = 128
A_DV = 128
B_HEADS = 8
B_KV_HEADS = 4
HEAD_DIM = 128
IDX_HEADS = 8
IDX_DIM = 64
IDX_TOPK_MAX = 256
ROPE_THETA = 10000.0
N_EXPERTS = 32
TOP_K = 4
SWIGLU_LIMIT = 7.0
SWIGLU_ALPHA = 1.702
EPS = 1e-6

LANES = 128
INT_MIN = -(2 ** 31)
NEG = -1e30
VMEM_LIMIT = 56 * 1024 * 1024


def _rms(x):
    return x * lax.rsqrt(jnp.mean(x * x, axis=-1, keepdims=True) + EPS)


def _silu(x):
    return x * jax.nn.sigmoid(x)


def _dot(a, b):
    return jnp.dot(a, b, preferred_element_type=F32)


def _dot_nt(a, b):
    return lax.dot_general(a, b, (((1,), (1,)), ((), ())), preferred_element_type=F32)


def _dot_tn(a, b):
    return lax.dot_general(a, b, (((0,), (0,)), ((), ())), preferred_element_type=F32)


def _const_spec(shape):
    zeros = (0,) * len(shape)
    return pl.BlockSpec(shape, lambda *_: zeros, pipeline_mode=pl.Buffered(1))


def _rope_tables(pos, d, reps):
    inv = 1.0 / (ROPE_THETA ** (jnp.arange(0, d, 2, dtype=F32) / d))
    ang = pos.astype(F32)[:, None] * inv[None, :]
    cos = jnp.cos(ang)
    sin = jnp.sin(ang)
    cos_t = jnp.concatenate([cos, cos] * reps, axis=-1)
    sin_t = jnp.concatenate([-sin, sin] * reps, axis=-1)
    return cos_t, sin_t


def _inproj_body(x_ref, g1_ref, wm_ref, ws_ref, wg_ref, qg_ref, kg_ref, ca_ref, sa_ref, cb_ref,
                 sb_ref, aq_o, af_o, ai_o, ag_o, q_o, kf_o, kb_o, vf_o, vb_o, qim_o, kif_o, ki2_o,
                 wi_o, ga_o, gb_o, vt_o):
    x = x_ref[...]
    tm = x.shape[0]
    hb = (_rms(x) * g1_ref[...]).astype(BF16)
    a_qk = A_HEADS * A_DK
    a_v = A_HEADS * A_DV
    b_q = B_HEADS * HEAD_DIM
    b_kv = B_KV_HEADS * HEAD_DIM
    iq_w = IDX_HEADS * IDX_DIM
    o = 0
    aq_o[...] = _dot(hb, wm_ref[:, o:o + a_qk]).astype(BF16)
    o += a_qk
    af_o[...] = _dot(hb, wm_ref[:, o:o + a_qk])
    o += a_qk
    ai_o[...] = _dot(hb, wm_ref[:, o:o + a_v]).astype(BF16)
    o += a_v
    ag_o[...] = _dot(hb, wm_ref[:, o:o + a_v]).astype(BF16)
    o += a_v

    ca = ca_ref[...]
    sa = sa_ref[...]

    def rope_head(y):
        return y * ca + pltpu.roll(y, HEAD_DIM // 2, 1) * sa

    zq = _dot(hb, wm_ref[:, o:o + b_q])
    o += b_q
    for h in range(B_HEADS):
        sl = slice(h * HEAD_DIM, (h + 1) * HEAD_DIM)
        q_o[:, sl] = (rope_head(_rms(zq[:, sl]) * qg_ref[...]) * Q_SCALE).astype(BF16)
    zk = _dot(hb, wm_ref[:, o:o + b_kv])
    o += b_kv
    for h in range(B_KV_HEADS):
        sl = slice(h * HEAD_DIM, (h + 1) * HEAD_DIM)
        y = rope_head(_rms(zk[:, sl]) * kg_ref[...])
        kf_o[pl.ds(h, tm, stride=B_KV_HEADS), :] = y
        kb_o[:, sl] = y.astype(BF16)
    zv = _dot(hb, wm_ref[:, o:o + b_kv])
    o += b_kv
    for h in range(B_KV_HEADS):
        vf_o[pl.ds(h, tm, stride=B_KV_HEADS), :] = zv[:, h * HEAD_DIM:(h + 1) * HEAD_DIM]
    vb_o[...] = zv.astype(BF16)
    vt_o[0] = zv.T.astype(BF16)

    cb = cb_ref[...]
    sb = sb_ref[...]
    lane = lax.broadcasted_iota(I32, (tm, LANES), 1)
    first_half = (lane & (IDX_DIM - 1)) < (IDX_DIM // 2)

    def rope_idx(y):
        partner = jnp.where(first_half, pltpu.roll(y, LANES - IDX_DIM // 2, 1),
                            pltpu.roll(y, IDX_DIM // 2, 1))
        return y * cb + partner * sb

    zi = _dot(hb, wm_ref[:, o:o + iq_w])
    for p in range(iq_w // LANES):
        y = rope_idx(zi[:, p * LANES:(p + 1) * LANES])
        qim_o[:, (2 * p) * LANES:(2 * p + 1) * LANES] = jnp.where(lane < IDX_DIM, y, 0.0).astype(BF16)
        qim_o[:, (2 * p + 1) * LANES:(2 * p + 2) * LANES] = jnp.where(lane >= IDX_DIM, y, 0.0).astype(BF16)
    zs = _dot(hb, ws_ref[...])
    y = rope_idx(zs[:, :LANES])
    ki2_o[...] = y.astype(BF16)
    kif_o[...] = y[:, :IDX_DIM]
    wi_o[...] = zs[:, LANES:] * (IDX_HEADS ** -0.5 * IDX_DIM ** -0.5)
    zg = _dot(hb, wg_ref[...])
    d = zg.shape[1] // 2
    ga_o[...] = zg[:, :d].astype(BF16)
    gb_o[...] = zg[:, d:].astype(BF16)


def _inproj(x2, g1, wm, ws, wg, qg, kg, tabs, tm, vt_tile):
    r, d = x2.shape
    ca, sa, cb, sb = tabs
    npos = ca.shape[0] // tm
    per = vt_tile // tm
    row = lambda w: pl.BlockSpec((tm, w), lambda i: (i, 0))
    tab = pl.BlockSpec((tm, LANES), lambda i: (i % npos, 0))
    a_qk = A_HEADS * A_DK
    b_q = B_HEADS * HEAD_DIM
    b_kv = B_KV_HEADS * HEAD_DIM
    kvh = B_KV_HEADS
    outs = [
        (1, a_qk, BF16), (1, a_qk, F32), (1, a_qk, BF16), (1, a_qk, BF16),
        (1, b_q, BF16), (kvh, HEAD_DIM, F32), (1, b_kv, BF16),
        (kvh, HEAD_DIM, F32), (1, b_kv, BF16),
        (1, IDX_HEADS * LANES, BF16), (1, IDX_DIM, F32), (1, LANES, BF16),
        (1, LANES, F32), (1, d, BF16), (1, d, BF16),
    ]
    return pl.pallas_call(
        _inproj_body,
        grid=(r // tm,),
        in_specs=[row(d), _const_spec(g1.shape), _const_spec(wm.shape), _const_spec(ws.shape),
                  _const_spec(wg.shape), _const_spec(qg.shape), _const_spec(kg.shape),
                  tab, tab, tab, tab],
        out_specs=[pl.BlockSpec((tm * k, w), lambda i: (i, 0)) for k, w, _ in outs] + [
            pl.BlockSpec((1, b_kv, tm), lambda i: (i // per, 0, i % per))],
        out_shape=[jax.ShapeDtypeStruct((r * k, w), dt) for k, w, dt in outs] + [
            jax.ShapeDtypeStruct((r // vt_tile, b_kv, vt_tile), BF16)],
        compiler_params=pltpu.CompilerParams(dimension_semantics=("arbitrary",),
                                             vmem_limit_bytes=VMEM_LIMIT),
        name=f"inproj_{r // tm}",
    )(x2, g1, wm, ws, wg, qg, kg, ca, sa, cb, sb)


HGRN_BATCH = 4


def _hgrn_body(aq_ref, af_ref, ai_ref, ag_ref, s0_ref, lbp_ref, ng_ref, oa_ref, sn_ref, st_ref,
               *, layer):
    c = pl.program_id(1)
    nc = pl.num_programs(1)
    nb = aq_ref.shape[0]

    @pl.when(c == 0)
    def _():
        for b in range(nb):
            for h in range(A_HEADS):
                st_ref[b, h] = s0_ref[b, h].T

    lbp = lbp_ref[...]
    e = jnp.exp(lbp - jnp.max(lbp, axis=0, keepdims=True))
    sm = e / jnp.sum(e, axis=0, keepdims=True)
    lb = jnp.sum(sm[:layer + 1], axis=0, keepdims=True)

    n = aq_ref.shape[1]
    r_i = lax.broadcasted_iota(I32, (n, n), 0)
    c_i = lax.broadcasted_iota(I32, (n, n), 1)
    causal = r_i >= c_i
    tri = jnp.where(causal, 1.0, 0.0).astype(BF16)
    prep = []
    for b in range(nb):
        f = lb + (1.0 - lb) * jax.nn.sigmoid(af_ref[b])
        lf = jnp.log(f)
        hi = lf.astype(BF16)
        r1 = lf - hi.astype(F32)
        mid = r1.astype(BF16)
        lo = (r1 - mid.astype(F32)).astype(BF16)
        cum = _dot(tri, hi) + _dot(tri, mid) + _dot(tri, lo)
        last = cum[n - 1:n, :]
        qd = _silu(aq_ref[b].astype(F32)) * jnp.exp(cum)
        k = 1.0 - f
        prep.append((qd, k * jnp.exp(-cum), k * jnp.exp(last - cum), jnp.exp(last),
                     _silu(ag_ref[b].astype(F32))))
    for h in range(A_HEADS):
        sl = slice(h * A_DK, (h + 1) * A_DK)
        for b, (qd, kd, kt, el, gate) in enumerate(prep):
            qd_h = qd[:, sl].astype(BF16)
            v_h = ai_ref[b, :, sl]
            att = jnp.where(causal, _dot_nt(qd_h, kd[:, sl].astype(BF16)), 0.0)
            st = st_ref[b, h]
            o = _dot(att.astype(BF16), v_h) + _dot_nt(qd_h, st.astype(BF16))
            st_ref[b, h] = st * el[:, sl] + _dot_tn(v_h, kt[:, sl].astype(BF16))
            oa_ref[b, :, sl] = (_rms(o) * ng_ref[...] * gate[:, sl]).astype(BF16)

    @pl.when(c == nc - 1)
    def _():
        for b in range(nb):
            for h in range(A_HEADS):
                sn_ref[b, h] = st_ref[b, h].T


def _hgrn(aq, af, ai, ag, s0, lbp, ng, bsz, t, layer):
    d = aq.shape[1]
    nc = t // CHUNK
    nb = _row_tile(bsz, HGRN_BATCH)
    seq = lambda a: a.reshape(bsz, t, d)
    row = pl.BlockSpec((nb, CHUNK, d), lambda g, c: (g, c, 0))
    st = pl.BlockSpec((nb, A_HEADS, A_DK, A_DV), lambda g, c: (g, 0, 0, 0))
    oa, s_new = pl.pallas_call(
        functools.partial(_hgrn_body, layer=layer),
        grid=(bsz // nb, nc),
        in_specs=[row, row, row, row, st, _const_spec(lbp.shape), _const_spec(ng.shape)],
        out_specs=[row, st],
        out_shape=[jax.ShapeDtypeStruct((bsz, t, d), BF16),
                   jax.ShapeDtypeStruct((bsz, A_HEADS, A_DK, A_DV), F32)],
        scratch_shapes=[pltpu.VMEM((nb, A_HEADS, A_DV, A_DK), F32)],
        compiler_params=pltpu.CompilerParams(dimension_semantics=("arbitrary", "arbitrary"),
                                             vmem_limit_bytes=VMEM_LIMIT),
        name=f"hgrn_{bsz // nb}x{nc}",
    )(seq(aq), seq(af), seq(ai), seq(ag), s0, lbp, ng)
    return oa.reshape(bsz * t, d), s_new


def _chunk_of(pos):
    return jnp.right_shift(pos, CHUNK.bit_length() - 1)


def _transpose_wi(wi, tq):
    if tq % LANES:
        wi = jnp.concatenate([wi, jnp.zeros((LANES - tq % LANES, LANES), F32)], axis=0)
    return wi.T[:IDX_HEADS, :tq]


def _score_tile(ki2_tile, qim_ref, wt):
    acc = None
    for h in range(IDX_HEADS):
        s = _dot_nt(ki2_tile, qim_ref[:, h * LANES:(h + 1) * LANES])
        term = jnp.maximum(s, 0.0) * wt[h:h + 1, :]
        acc = term if acc is None else acc + term
    return acc


def _mask_scores(score, adm):
    score = jnp.where(score == 0.0, 0.0, score)
    return score if adm is None else jnp.where(adm, score, -jnp.inf)


def _sort_key(score):
    bits = lax.bitcast_convert_type(score, I32)
    return jnp.where(bits < 0, bits ^ 0x7FFFFFFF, bits)


def _key_to_score(key):
    return lax.bitcast_convert_type(jnp.where(key < 0, key ^ 0x7FFFFFFF, key), F32)


SUBLANES = 8
COUNT_ACCS = 4
KEY_BITS = 32
PLANE_KEYS = SUBLANES * KEY_BITS


def _store_bit_planes(planes_ref, key, blk0):
    u = key ^ INT_MIN
    tk = u.shape[0]
    for blk in range(tk // PLANE_KEYS):
        a = [u[blk * PLANE_KEYS + i * SUBLANES:blk * PLANE_KEYS + (i + 1) * SUBLANES, :]
             for i in range(KEY_BITS)]
        for j, m in ((16, 0x0000FFFF), (8, 0x00FF00FF), (4, 0x0F0F0F0F), (2, 0x33333333),
                     (1, 0x55555555)):
            k = 0
            while k < KEY_BITS:
                t = (a[k] ^ lax.shift_right_logical(a[k + j], j)) & m
                a[k] = a[k] ^ t
                a[k + j] = a[k + j] ^ lax.shift_left(t, j)
                k = (k + j + 1) & ~j
        row = pl.multiple_of((blk0 + blk) * SUBLANES, SUBLANES)
        for p in range(KEY_BITS):
            planes_ref[p, pl.ds(row, SUBLANES), :] = a[p]


def _select(sc_ref, planes_ref, bias_ref, cut_ref, stat_ref, nkt, krow, tk, tq, nbits):
    nrow = planes_ref.shape[1]
    nblk = nkt * (tk // PLANE_KEYS)
    blk_of_row = jnp.right_shift(lax.broadcasted_iota(I32, (nrow, tq), 0),
                                 SUBLANES.bit_length() - 1)

    def word_count(words):
        pc = lax.population_count(words).reshape(nrow // SUBLANES, SUBLANES, tq)
        return jnp.sum(jnp.sum(pc, axis=0).astype(F32), axis=0, keepdims=True)

    def bit_body(i, carry):
        alive, above, thr_u = carry
        ones = alive & planes_ref[i]
        c1 = word_count(ones)
        take = above + c1 >= krow
        alive = jnp.where(take, ones, alive ^ ones)
        above = jnp.where(take, above, above + c1)
        thr_u = jnp.where(take, thr_u | jnp.left_shift(jnp.int32(1), KEY_BITS - 1 - i), thr_u)
        return alive, above, thr_u

    alive0 = jnp.where(blk_of_row < nblk, -1, 0).astype(I32)
    _, _, thr_u = lax.fori_loop(
        0, KEY_BITS, bit_body, (alive0, jnp.zeros((1, tq), F32), jnp.zeros((1, tq), I32)))

    nchunk = tk // SUBLANES
    sub = lax.broadcasted_iota(I32, (SUBLANES, tq), 0)

    def count(pred_fn):
        def body(j, accs):
            t = sc_ref[j]
            accs = list(accs)
            for c in range(nchunk):
                a = c % COUNT_ACCS
                accs[a] = pred_fn(t[c * SUBLANES:(c + 1) * SUBLANES, :], j, c, accs[a])
            return tuple(accs)
        zero = jnp.zeros((SUBLANES, tq), F32)
        accs = lax.fori_loop(0, nkt, body, (zero,) * COUNT_ACCS)
        tot = accs[0]
        for a in accs[1:]:
            tot = tot + a
        return jnp.sum(tot, axis=0, keepdims=True)

    def count_ge(cand):
        c8 = jnp.broadcast_to(cand, (SUBLANES, tq))
        return count(lambda t, j, c, acc: jnp.where(t >= c8, acc + 1.0, acc))

    def count_gt(cand):
        c8 = jnp.broadcast_to(cand, (SUBLANES, tq))
        return count(lambda t, j, c, acc: jnp.where(t > c8, acc + 1.0, acc))

    def record(thr):
        stat_ref[0:1, :] = thr
        stat_ref[1:2, :] = count_ge(thr)
        stat_ref[2:3, :] = count_gt(thr)

    record(_key_to_score(thr_u ^ INT_MIN))
    good = (stat_ref[2:3, :] < krow) & (stat_ref[1:2, :] >= krow)

    @pl.when(jnp.max(jnp.where(good, 0.0, 1.0)) > 0.0)
    def _():
        def cmp_body(i, tu):
            cand_u = tu | jnp.left_shift(jnp.int32(1), KEY_BITS - 1 - i)
            return jnp.where(count_ge(_key_to_score(cand_u ^ INT_MIN)) >= krow, cand_u, tu)
        tu = lax.fori_loop(0, KEY_BITS, cmp_body, jnp.zeros((1, tq), I32))
        record(_key_to_score(tu ^ INT_MIN))

    thr = stat_ref[0:1, :]
    n_ge = stat_ref[1:2, :]
    need = krow - stat_ref[2:3, :]

    cut_ref[...] = jnp.full(cut_ref.shape, 2 ** 31 - 1, I32)

    @pl.when(jnp.max(jnp.where(n_ge > krow, 1.0, 0.0)) > 0.0)
    def _():
        thr8 = jnp.broadcast_to(thr, (SUBLANES, tq))

        def idx_body(i, cut):
            cand = cut | jnp.left_shift(jnp.int32(1), nbits - 1 - i)
            c8 = jnp.broadcast_to(cand, (SUBLANES, tq))
            below = count(lambda t, j, c, acc: jnp.where(
                t == thr8, jnp.where(j * tk + c * SUBLANES + sub < c8, acc + 1.0, acc), acc))
            return jnp.where(below < need, cand, cut)
        cut_ref[...] = lax.fori_loop(0, nbits, idx_body, jnp.zeros((1, tq), I32))

    cut = cut_ref[...]

    def write(j, carry):
        t = sc_ref[j]
        idx = j * tk + lax.broadcasted_iota(I32, (tk, tq), 0)
        tie = jnp.where(t == thr, jnp.where(idx <= cut, 0.0, NEG), NEG)
        bias_ref[j] = jnp.where(t > thr, 0.0, tie)
        return carry

    lax.fori_loop(0, nkt, write, 0)


def _stack_heads(q_ref, u, stack):
    heads = [q_ref[:, h * HEAD_DIM:(h + 1) * HEAD_DIM] for h in range(u * stack, (u + 1) * stack)]
    return heads[0] if stack == 1 else jnp.concatenate(heads, axis=0)


def _store_heads(o_ref, u, stack, o, tq):
    for r in range(stack):
        h = u * stack + r
        o_ref[:, h * HEAD_DIM:(h + 1) * HEAD_DIM] = o[r * tq:(r + 1) * tq].astype(o_ref.dtype)


Q_SCALE = HEAD_DIM ** -0.5 * 1.4426950408889634
SUM_ROWS = 16


def _attn_reset(m_ref, l_ref, acc_ref):
    m_ref[...] = jnp.full(m_ref.shape, NEG, F32)
    l_ref[...] = jnp.zeros(l_ref.shape, F32)
    acc_ref[...] = jnp.zeros(acc_ref.shape, F32)


def _attn_logits(q_ref, u, stack, kt, bias, m_ref):
    s = _dot_nt(kt, _stack_heads(q_ref, u, stack)) + jnp.concatenate([bias] * stack, axis=1)
    return s, jnp.maximum(m_ref[u], jnp.max(s, axis=0, keepdims=True))


def _attn_accumulate(u, s, m_new, m_ref, l_ref, acc_ref, pv):
    alpha = jnp.exp2(m_ref[u] - m_new)
    p = jnp.exp2(s - m_new)
    if l_ref is not None:
        l_ref[u] = alpha * l_ref[u] + jnp.sum(p, axis=0, keepdims=True)
    acc_ref[u] = alpha * acc_ref[u] + pv(p.astype(BF16))
    m_ref[u] = m_new


def _attn_step(q_ref, u, stack, kt, bias, m_ref, l_ref, acc_ref, pv):
    s, m_new = _attn_logits(q_ref, u, stack, kt, bias, m_ref)
    _attn_accumulate(u, s, m_new, m_ref, l_ref, acc_ref, pv)


def _attn_finish(o_ref, u, stack, l_ref, acc_ref, tq):
    acc = acc_ref[u]
    l = acc[HEAD_DIM:HEAD_DIM + 1] if l_ref is None else l_ref[u]
    _store_heads(o_ref, u, stack, (acc[:HEAD_DIM] / l).T, tq)


def _dsa_prompt_body(q_ref, qim_ref, wi_ref, kb_ref, vt_ref, ki2_ref, o_ref, sc_ref, planes_ref,
                     bias_ref, cut_ref, stat_ref, m_ref, l_ref, acc_ref,
                     *, tq, tk, topk, nbits):
    q0 = pl.program_id(1) * tq
    nkt = lax.div(q0 + tq + tk - 1, tk)
    wt = _transpose_wi(wi_ref[...], tq)
    qchunk = _chunk_of(q0 + lax.broadcasted_iota(I32, (1, tq), 1))

    @pl.when((pl.program_id(0) == 0) & (pl.program_id(1) == 0))
    def _():
        planes_ref[...] = jnp.zeros(planes_ref.shape, I32)

    def score_body(j, carry):
        ks = pl.multiple_of(j * tk, tk)
        score = _score_tile(ki2_ref[pl.ds(ks, tk), :], qim_ref, wt)
        kpos = ks + lax.broadcasted_iota(I32, (tk, tq), 0)
        score = _mask_scores(score, _chunk_of(kpos) <= qchunk)
        sc_ref[j] = score
        _store_bit_planes(planes_ref, _sort_key(score), j * (tk // PLANE_KEYS))
        return carry

    lax.fori_loop(0, nkt, score_body, 0)

    krow = jnp.minimum((qchunk + 1) * CHUNK, topk).astype(F32)
    _select(sc_ref, planes_ref, bias_ref, cut_ref, stat_ref, nkt, krow, tk, tq, nbits)

    _attn_reset(m_ref, l_ref, acc_ref)

    rep = B_HEADS // B_KV_HEADS
    ones = jnp.ones((SUM_ROWS, tk), BF16)

    def att_body(j, carry):
        ks = pl.multiple_of(j * tk, tk)
        bias = bias_ref[j]
        gsl = lambda g: slice(g * HEAD_DIM, (g + 1) * HEAD_DIM)
        logits = lambda g: _attn_logits(q_ref, g, rep, kb_ref[pl.ds(ks, tk), gsl(g)], bias, m_ref)
        pending = [logits(0), logits(1)]
        for g in range(B_KV_HEADS):
            if g + 2 < B_KV_HEADS:
                pending.append(logits(g + 2))
            vt = jnp.concatenate([vt_ref[j, gsl(g), :], ones], axis=0)
            _attn_accumulate(g, *pending[g], m_ref, None, acc_ref, lambda p, vt=vt: _dot(vt, p))
        return carry

    lax.fori_loop(0, nkt, att_body, 0)
    for g in range(B_KV_HEADS):
        _attn_finish(o_ref, g, rep, None, acc_ref, tq)


def _dsa_scratch(nt, tk, tq, stack, acc_rows):
    units = B_HEADS // stack
    return [pltpu.VMEM((nt, tk, tq), F32),
            pltpu.VMEM((KEY_BITS, nt * tk // KEY_BITS, tq), I32),
            pltpu.VMEM((nt, tk, tq), F32),
            pltpu.VMEM((1, tq), I32),
            pltpu.VMEM((SUBLANES, tq), F32),
            pltpu.VMEM((units, 1, stack * tq), F32),
            pltpu.VMEM((units, 1, stack * tq), F32),
            pltpu.VMEM((units, acc_rows, stack * tq), F32)]


def _dsa_prompt(q, qim, wi, kb, vt, ki2, bsz, t, tq, tk):
    nq = t // tq
    nt = t // tk
    topk = min(IDX_TOPK_MAX, t // 4)
    nbits = max(1, (t - 1).bit_length())
    qrow = lambda w: pl.BlockSpec((tq, w), lambda b, i: (b * nq + i, 0))
    seq = lambda w: pl.BlockSpec((t, w), lambda b, i: (b, 0))
    return pl.pallas_call(
        functools.partial(_dsa_prompt_body, tq=tq, tk=tk, topk=topk, nbits=nbits),
        grid=(bsz, nq),
        in_specs=[qrow(q.shape[1]), qrow(qim.shape[1]), qrow(LANES), seq(kb.shape[1]),
                  pl.BlockSpec((nt, vt.shape[1], tk), lambda b, i: (b, 0, 0)), seq(LANES)],
        out_specs=qrow(q.shape[1]),
        out_shape=jax.ShapeDtypeStruct(q.shape, BF16),
        scratch_shapes=_dsa_scratch(nt, tk, tq, B_HEADS // B_KV_HEADS, HEAD_DIM + SUM_ROWS),
        compiler_params=pltpu.CompilerParams(dimension_semantics=("arbitrary", "arbitrary"),
                                             vmem_limit_bytes=VMEM_LIMIT),
        name="dsa_prompt",
    )(q, qim, wi, kb, vt, ki2)


def _dsa_sample_body(q_ref, qim_ref, wi_ref, kn_ref, vn_ref, ki2n_ref, kidx2_ref, ck_ref, cv_ref,
                     o_ref, sc_ref, planes_ref, bias_ref, cut_ref, stat_ref, m_ref, l_ref, acc_ref,
                     *, tq, tk, nsub, npt, past, topk, nbits):
    j = pl.program_id(1)
    nj = pl.num_programs(1)
    pad = jnp.zeros((tk - tq, LANES), BF16)
    nb = tk // PLANE_KEYS

    @pl.when(j == 0)
    def _():
        wt = _transpose_wi(wi_ref[...], tq)

        def score_body(jt, carry):
            ks = pl.multiple_of(jt * tk, tk)
            score = _mask_scores(_score_tile(kidx2_ref[pl.ds(ks, tk), :], qim_ref, wt), None)
            sc_ref[jt] = score
            _store_bit_planes(planes_ref, _sort_key(score), jt * nb)
            return carry

        lax.fori_loop(0, npt, score_body, 0)
        ki2n = jnp.concatenate([ki2n_ref[...], pad], axis=0)
        krow_i = lax.broadcasted_iota(I32, (tk, tq), 0)
        qchunk = _chunk_of(past + lax.broadcasted_iota(I32, (1, tq), 1))
        adm = jnp.where(krow_i < tq, _chunk_of(past + krow_i), 2 ** 30) <= qchunk
        score = _mask_scores(_score_tile(ki2n, qim_ref, wt), adm)
        sc_ref[npt] = score
        _store_bit_planes(planes_ref, _sort_key(score), npt * nb)
        krow = jnp.minimum((qchunk + 1) * CHUNK, topk).astype(F32)
        _select(sc_ref, planes_ref, bias_ref, cut_ref, stat_ref, npt + 1, krow, tk, tq, nbits)
        _attn_reset(m_ref, l_ref, acc_ref)

    rep = B_HEADS // B_KV_HEADS

    def update(g, kt, vt, bias):
        _attn_step(q_ref, g, rep, kt, bias, m_ref, l_ref, acc_ref, lambda p: _dot_tn(vt, p))

    for su in range(nsub):
        bias = bias_ref[j * nsub + su]
        rows = lambda g: pl.ds(su * tk * B_KV_HEADS + g, tk, stride=B_KV_HEADS)
        logits = lambda g: _attn_logits(q_ref, g, rep, ck_ref[rows(g), :].astype(BF16), bias,
                                        m_ref)
        pending = [logits(g) for g in range(B_KV_HEADS)]
        for g in range(B_KV_HEADS):
            vt = cv_ref[rows(g), :].astype(BF16)
            _attn_accumulate(g, *pending[g], m_ref, l_ref, acc_ref,
                             lambda p, vt=vt: _dot_tn(vt, p))

    @pl.when(j == nj - 1)
    def _():
        bias = bias_ref[npt]
        for g in range(B_KV_HEADS):
            gs = slice(g * HEAD_DIM, (g + 1) * HEAD_DIM)
            kt = jnp.concatenate([kn_ref[:, gs], pad], axis=0)
            vt = jnp.concatenate([vn_ref[:, gs], pad], axis=0)
            update(g, kt, vt, bias)
            _attn_finish(o_ref, g, rep, l_ref, acc_ref, tq)


def _dsa_sample(q, qim, wi, kn, vn, ki2n, kidx2, ck, cv, bsz, t, past, tk, nsub, cache_off):
    tq = t
    npt = past // tk
    nj = npt // nsub
    topk = min(IDX_TOPK_MAX, (past + t) // 4)
    nbits = max(1, (past + t - 1).bit_length())
    rep = B_HEADS // B_KV_HEADS
    qrow = lambda w: pl.BlockSpec((tq, w), lambda b, j: (b, 0))
    kvw = kn.shape[1]
    cache = pl.BlockSpec((nsub * tk * B_KV_HEADS, HEAD_DIM),
                         lambda b, j: ((cache_off + b) * nj + j, 0))
    return pl.pallas_call(
        functools.partial(_dsa_sample_body, tq=tq, tk=tk, nsub=nsub, npt=npt, past=past,
                          topk=topk, nbits=nbits),
        grid=(bsz, nj),
        in_specs=[qrow(q.shape[1]), qrow(qim.shape[1]), qrow(LANES), qrow(kvw), qrow(kvw),
                  qrow(LANES), pl.BlockSpec((past, LANES), lambda b, j: (b, 0)), cache, cache],
        out_specs=qrow(q.shape[1]),
        out_shape=jax.ShapeDtypeStruct(q.shape, BF16),
        scratch_shapes=_dsa_scratch(npt + 1, tk, tq, rep, HEAD_DIM),
        compiler_params=pltpu.CompilerParams(dimension_semantics=("arbitrary", "arbitrary"),
                                             vmem_limit_bytes=VMEM_LIMIT),
        name="dsa_sample",
    )(q, qim, wi, kn, vn, ki2n, kidx2, ck, cv)


def _post_body(oa_ref, ob_ref, ga_ref, gb_ref, x_ref, wa_ref, wb_ref, wo_ref, g2_ref, wr_ref,
               br_ref, x1_o, t_o, eidx_o, prob_o, cnt_o):
    ya = _dot(oa_ref[...], wa_ref[...])
    yb = _dot(ob_ref[...], wb_ref[...])
    merged = (jax.nn.sigmoid(ga_ref[...].astype(F32)) * ya
              + jax.nn.sigmoid(gb_ref[...].astype(F32)) * yb)
    x1 = x_ref[...] + _dot(merged.astype(BF16), wo_ref[...])
    x1_o[...] = x1
    tok = _rms(x1) * g2_ref[...]
    t_o[...] = tok
    logits = _dot(tok.astype(BF16), wr_ref[...]) + br_ref[...]
    tm = logits.shape[0]
    lane = lax.broadcasted_iota(I32, (tm, LANES), 1).astype(F32)
    cur = jnp.where(lane < N_EXPERTS, logits, -jnp.inf)
    top = None
    den = jnp.zeros((tm, 1), F32)
    eidx = jnp.zeros((tm, LANES), F32)
    prob = jnp.zeros((tm, LANES), F32)
    chosen = jnp.zeros((tm, LANES), F32)
    for k in range(TOP_K):
        mx = jnp.max(cur, axis=1, keepdims=True)
        first = jnp.min(jnp.where(cur == mx, lane, float(LANES)), axis=1, keepdims=True)
        if top is None:
            top = mx
        e = jnp.exp(mx - top)
        den = den + e
        eidx = jnp.where(lane == k, first, eidx)
        prob = jnp.where(lane == k, e, prob)
        hit = lane == first
        chosen = jnp.where(hit, 1.0, chosen)
        cur = jnp.where(hit, -jnp.inf, cur)
    eidx_o[...] = eidx.astype(I32)
    prob_o[...] = prob / den

    @pl.when(pl.program_id(0) == 0)
    def _():
        cnt_o[...] = jnp.zeros(cnt_o.shape, F32)

    cnt_o[...] += jnp.sum(chosen, axis=0, keepdims=True)


def _post(oa, ob, ga, gb, x2, wa, wb, wo, g2, wr, br, tm):
    r, d = x2.shape
    row = lambda w: pl.BlockSpec((tm, w), lambda i: (i, 0))
    return pl.pallas_call(
        _post_body,
        grid=(r // tm,),
        in_specs=[row(d), row(d), row(d), row(d), row(d), _const_spec(wa.shape),
                  _const_spec(wb.shape), _const_spec(wo.shape), _const_spec(g2.shape),
                  _const_spec(wr.shape), _const_spec(br.shape)],
        out_specs=[row(d), row(d), row(LANES), row(LANES),
                   pl.BlockSpec((1, LANES), lambda i: (0, 0))],
        out_shape=[jax.ShapeDtypeStruct((r, d), F32), jax.ShapeDtypeStruct((r, d), F32),
                   jax.ShapeDtypeStruct((r, LANES), I32), jax.ShapeDtypeStruct((r, LANES), F32),
                   jax.ShapeDtypeStruct((1, LANES), F32)],
        compiler_params=pltpu.CompilerParams(dimension_semantics=("arbitrary",),
                                             vmem_limit_bytes=VMEM_LIMIT),
        name=f"post_{r // tm}",
    )(oa, ob, ga, gb, x2, wa, wb, wo, g2, wr, br)


MOE_TILE = 256
DMA_PRIORITIES = 2
ISSUE_UNROLL = 8


def _rank_body(eidx_ref, cin_ref, pos_o, carry_ref):
    @pl.when(pl.program_id(0) == 0)
    def _():
        carry_ref[...] = cin_ref[...]

    eidx = eidx_ref[...]
    tm = eidx.shape[0]
    lane = lax.broadcasted_iota(I32, (tm, LANES), 1)
    hits = [lane == eidx[:, k:k + 1] for k in range(TOP_K)]
    onehot = jnp.zeros((tm, LANES), F32)
    for hit in hits:
        onehot = onehot + jnp.where(hit, 1.0, 0.0)
    r_i = lax.broadcasted_iota(I32, (tm, tm), 0)
    c_i = lax.broadcasted_iota(I32, (tm, tm), 1)
    before = jnp.where(c_i < r_i, 1.0, 0.0).astype(BF16)
    base = carry_ref[...] + _dot(before, onehot.astype(BF16))
    rank = jnp.zeros((tm, LANES), F32)
    for k, hit in enumerate(hits):
        rk = jnp.sum(jnp.where(hit, base, 0.0), axis=1, keepdims=True)
        rank = jnp.where(lane == k, rk, rank)
    pos_o[...] = rank.T[:SUBLANES, :].astype(I32)
    carry_ref[...] += jnp.sum(onehot, axis=0, keepdims=True)


def _rank(eidx, cin, tm):
    r = eidx.shape[0]
    return pl.pallas_call(
        _rank_body,
        grid=(r // tm,),
        in_specs=[pl.BlockSpec((tm, LANES), lambda i: (i, 0)), _const_spec(cin.shape)],
        out_specs=pl.BlockSpec((SUBLANES, tm), lambda i: (0, i)),
        out_shape=jax.ShapeDtypeStruct((SUBLANES, r), I32),
        scratch_shapes=[pltpu.VMEM((1, LANES), F32)],
        compiler_params=pltpu.CompilerParams(dimension_semantics=("arbitrary",)),
        name=f"moe_rank_{r // tm}",
    )(eidx, cin)


def _wait_rows(ref, sem, times):
    for _ in range(times):
        pltpu.make_async_copy(ref, ref, sem).wait()


def _zero_pad_rows(pad_ref, xs_out, zeros, sem):
    zeros[...] = jnp.zeros(zeros.shape, zeros.dtype)
    row = zeros.at[pl.ds(0, 1)]

    def for_each_row(fn):
        def per_range(g, carry):
            first = pad_ref[0, g]

            def per_row(t, c):
                fn(pltpu.make_async_copy(row, xs_out.at[pl.ds(first + t, 1)], sem))
                return c
            return lax.fori_loop(0, pad_ref[1, g], per_row, carry)
        lax.fori_loop(0, pad_ref.shape[1], per_range, 0)

    for_each_row(lambda copy: copy.start())
    for_each_row(lambda copy: copy.wait())


def _dispatch_body(pad_ref, *refs, bounds):
    ns = len(bounds) - 1
    pos_refs, tok_refs = refs[:ns], refs[ns:2 * ns]
    xs_out, ring, sems, zeros, zsem = refs[2 * ns:]
    i = pl.program_id(0)
    tm = ring.shape[1]
    last = pl.num_programs(0) - 1

    pl.when(i == 0)(functools.partial(_zero_pad_rows, pad_ref, xs_out, zeros, zsem))

    for slot in range(2):
        mine = i % 2 == slot
        for s in range(ns):
            @pl.when(mine & (i >= bounds[s]) & (i < bounds[s + 1]))
            def _(slot=slot, s=s):
                ring[slot] = tok_refs[s][...]

                def issue(t, carry):
                    for k in range(TOP_K):
                        pltpu.make_async_copy(ring.at[slot, pl.ds(t, 1)],
                                              xs_out.at[pl.ds(pos_refs[s][k, t], 1)],
                                              sems.at[slot]).start(priority=k % DMA_PRIORITIES)
                    return carry

                lax.fori_loop(0, tm, issue, 0)

        @pl.when(mine & (i > 0))
        def _(slot=slot):
            _wait_rows(ring.at[1 - slot], sems.at[1 - slot], TOP_K)

        @pl.when(mine & (i == last))
        def _(slot=slot):
            _wait_rows(ring.at[slot], sems.at[slot], TOP_K)


def _dispatch(poss, pad, toks, rows, tm):
    d = toks[0].shape[1]
    bounds = [0]
    for tok in toks:
        bounds.append(bounds[-1] + tok.shape[0] // tm)

    def local(s):
        lo, n = bounds[s], bounds[s + 1] - bounds[s]
        return lambda i: jnp.clip(i - lo, 0, n - 1)

    in_specs = [pl.BlockSpec(memory_space=pltpu.SMEM)]
    in_specs += [pl.BlockSpec((SUBLANES, tm), lambda i, f=local(s): (0, f(i)),
                              memory_space=pltpu.SMEM) for s in range(len(toks))]
    in_specs += [pl.BlockSpec((tm, d), lambda i, f=local(s): (f(i), 0)) for s in range(len(toks))]
    return pl.pallas_call(
        functools.partial(_dispatch_body, bounds=tuple(bounds)),
        grid=(bounds[-1],),
        in_specs=in_specs,
        out_specs=pl.BlockSpec(memory_space=pl.ANY),
        out_shape=jax.ShapeDtypeStruct((rows, d), toks[0].dtype),
        scratch_shapes=[pltpu.VMEM((2, tm, d), toks[0].dtype), pltpu.SemaphoreType.DMA((2,)),
                        pltpu.VMEM((SUBLANES, d), toks[0].dtype), pltpu.SemaphoreType.DMA],
        compiler_params=pltpu.CompilerParams(dimension_semantics=("arbitrary",),
                                             has_side_effects=True),
        name="moe_dispatch",
    )(pad, *poss, *toks)


def _experts_body(te_ref, na_ref, x_ref, wgu_ref, bgu_ref, wd_ref, bd_ref, y_ref, wgu_s, wd_s):
    r = pl.program_id(0)
    e = te_ref[r]
    prev = te_ref[jnp.maximum(r - 1, 0)]

    @pl.when((r == 0) | (e != prev))
    def _():
        rows = 128
        for c in range(wgu_s.shape[0] // rows):
            wgu_s[c * rows:(c + 1) * rows, :] = wgu_ref[0, c * rows:(c + 1) * rows, :].astype(BF16)
        for c in range(wd_s.shape[0] // rows):
            wd_s[c * rows:(c + 1) * rows, :] = wd_ref[0, c * rows:(c + 1) * rows, :].astype(BF16)

    @pl.when(r < na_ref[0])
    def _():
        gu = _dot(x_ref[...].astype(BF16), wgu_s[...]) + bgu_ref[0]
        dff = gu.shape[1] // 2
        gate = jnp.minimum(gu[:, :dff], SWIGLU_LIMIT)
        up = jnp.clip(gu[:, dff:], -SWIGLU_LIMIT, SWIGLU_LIMIT)
        act = (up + 1.0) * gate * jax.nn.sigmoid(SWIGLU_ALPHA * gate)
        y_ref[...] = _dot(act.astype(BF16), wd_s[...]) + bd_ref[0]

    @pl.when(r >= na_ref[0])
    def _():
        y_ref[...] = jnp.zeros(y_ref.shape, F32)


def _experts(tile_expert, n_active, xs, wgu, bgu, wd, bd):
    p, d = xs.shape
    _, _, dff2 = wgu.shape
    tm = MOE_TILE
    grid_spec = pltpu.PrefetchScalarGridSpec(
        num_scalar_prefetch=2,
        grid=(p // tm,),
        in_specs=[pl.BlockSpec((tm, d), lambda r, te, na: (r, 0)),
                  pl.BlockSpec((1, d, dff2), lambda r, te, na: (te[r], 0, 0)),
                  pl.BlockSpec((1, 1, dff2), lambda r, te, na: (te[r], 0, 0)),
                  pl.BlockSpec((1, dff2 // 2, d), lambda r, te, na: (te[r], 0, 0)),
                  pl.BlockSpec((1, 1, d), lambda r, te, na: (te[r], 0, 0))],
        out_specs=pl.BlockSpec((tm, d), lambda r, te, na: (r, 0)),
        scratch_shapes=[pltpu.VMEM((d, dff2), BF16), pltpu.VMEM((dff2 // 2, d), BF16)],
    )
    return pl.pallas_call(
        _experts_body,
        grid_spec=grid_spec,
        out_shape=jax.ShapeDtypeStruct((p, d), F32),
        compiler_params=pltpu.CompilerParams(dimension_semantics=("arbitrary",),
                                             vmem_limit_bytes=VMEM_LIMIT),
        name="moe_experts",
    )(tile_expert, n_active, xs, wgu, bgu, wd, bd)


def _combine_body(pos_ref, prob_ref, x1_ref, y_hbm, out_ref, buf, sems):
    i = pl.program_id(0)
    n = pl.num_programs(0) - 1
    tm = x1_ref.shape[0]

    for slot in range(2):
        mine = i % 2 == slot

        @pl.when(mine & (i < n))
        def _(slot=slot):
            def issue(t, carry):
                for k in range(TOP_K):
                    pltpu.make_async_copy(y_hbm.at[pl.ds(pos_ref[k, t], 1)],
                                          buf.at[slot, k, pl.ds(t, 1)],
                                          sems.at[slot]).start(priority=k % DMA_PRIORITIES)
                return carry
            lax.fori_loop(0, tm, issue, 0)

        @pl.when(mine & (i > 0))
        def _(slot=slot):
            _wait_rows(buf.at[1 - slot, 0], sems.at[1 - slot], TOP_K)
            prob = prob_ref[...]
            ffn = prob[:, 0:1] * buf[1 - slot, 0]
            for k in range(1, TOP_K):
                ffn = ffn + prob[:, k:k + 1] * buf[1 - slot, k]
            out_ref[...] = x1_ref[...] + ffn


def _combine(pos, prob, x1, ys, tm):
    r, d = x1.shape
    n = r // tm
    prev = lambda w: pl.BlockSpec((tm, w), lambda i: (jnp.maximum(i - 1, 0), 0))
    return pl.pallas_call(
        _combine_body,
        grid=(n + 1,),
        in_specs=[pl.BlockSpec((SUBLANES, tm), lambda i: (0, jnp.minimum(i, n - 1)),
                               memory_space=pltpu.SMEM),
                  prev(LANES), prev(d), pl.BlockSpec(memory_space=pl.ANY)],
        out_specs=prev(d),
        out_shape=jax.ShapeDtypeStruct((r, d), F32),
        scratch_shapes=[pltpu.VMEM((2, TOP_K, tm, d), F32), pltpu.SemaphoreType.DMA((2,))],
        compiler_params=pltpu.CompilerParams(dimension_semantics=("arbitrary",),
                                             vmem_limit_bytes=VMEM_LIMIT),
        name=f"moe_combine_{n}",
    )(pos, prob, x1, ys)


def _moe(streams, wgu, bgu, wd, bd):
    tm = MOE_TILE
    counts = [s[4][0, :N_EXPERTS].astype(I32) for s in streams]
    cnt = sum(counts)
    padded = ((cnt + tm - 1) // tm) * tm
    ends = jnp.cumsum(padded)
    n_pairs = sum(s[0].shape[0] for s in streams) * TOP_K
    n_tiles = n_pairs // tm + N_EXPERTS
    tile_start = jnp.arange(n_tiles, dtype=I32) * tm
    tile_expert = jnp.minimum(jnp.sum((ends[None, :] <= tile_start[:, None]).astype(I32), axis=1),
                              N_EXPERTS - 1)
    n_active = (ends[-1:] // tm).astype(I32)
    start = ends - padded
    pad = jnp.stack([jnp.append(start + cnt, ends[-1]),
                     jnp.append(padded - cnt, n_tiles * tm - ends[-1])])
    poss = []
    for (x1, tok, eidx, prob, _), c in zip(streams, counts):
        cin = jnp.pad(start, (0, LANES - N_EXPERTS)).astype(F32)[None]
        poss.append(_rank(eidx, cin, _row_tile(eidx.shape[0], 256)))
        start = start + c
    xs = _dispatch(poss, pad, [s[1] for s in streams], n_tiles * tm,
                   min(_row_tile(s[1].shape[0], 256) for s in streams))
    ys = _experts(tile_expert, n_active, xs, wgu, bgu, wd, bd)
    return [_combine(pos, s[3], s[0], ys, _row_tile(s[0].shape[0], 256))
            for s, pos in zip(streams, poss)]


def _row_tile(r, want):
    tm = min(r, want)
    assert r % tm == 0, (r, tm)
    return tm


def _mixers(x2, bsz, t, pos, s0, caches, wts, layer):
    (g1, wm, ws, wg, qg, kg, lbp, ng) = wts
    r = x2.shape[0]
    tm = _row_tile(r, 256)
    assert t % tm == 0 or tm % t == 0
    tabs_a = _rope_tables(pos, HEAD_DIM, 1)
    tabs_b = _rope_tables(pos, IDX_DIM, LANES // IDX_DIM)
    tabs = tabs_a + tabs_b
    if tm > t:
        tabs = tuple(jnp.tile(tb, (tm // t, 1)) for tb in tabs)
    tk = max(tm, _row_tile(t, 512)) if caches is None else tm
    (aq, af, ai, ag, q, kf, kb, vf, vb, qim, kif, ki2, wi, ga, gb, vt) = _inproj(
        x2, g1, wm, ws, wg, qg, kg, tabs, tm, tk)
    oa, s_new = _hgrn(aq, af, ai, ag, s0, lbp, ng, bsz, t, layer)
    if caches is None:
        ob = _dsa_prompt(q, qim, wi, kb, vt, ki2, bsz, t, _row_tile(t, 256), tk)
    else:
        ck, cv, kidx2, past, cache_off = caches
        tk = _row_tile(past, 512)
        nsub = 2 if (past // tk) % 2 == 0 else 1
        ob = _dsa_sample(q, qim, wi, kb, vb, ki2, kidx2, ck, cv, bsz, t, past, tk, nsub, cache_off)
    return oa, ob, ga, gb, kf, vf, kif, s_new


def kernel(x_prompt, x_sample, cache_k, cache_v, cache_kidx, state_hgrn, norm1_g, w_in, lower_bounds, hgrn_norm_g, q_norm_g, k_norm_g, w_branch_a, w_branch_b, w_out, norm2_g, w_router, b_router, w_gate_up, b_gate_up, w_down, b_down):
    bp, tp, d = x_prompt.shape
    bs, ts, _ = x_sample.shape
    depth = w_in.shape[0]
    past = cache_k.shape[2]
    kvw = B_KV_HEADS * HEAD_DIM
    pos_p = jnp.arange(tp, dtype=I32)
    pos_s = past + jnp.arange(ts, dtype=I32)
    xp = x_prompt.reshape(bp * tp, d)
    xs = x_sample.reshape(bs * ts, d)
    n_main = 2 * A_HEADS * A_DK + 2 * A_HEADS * A_DV + B_HEADS * HEAD_DIM + 2 * kvw + IDX_HEADS * IDX_DIM
    n_small = n_main + IDX_DIM + IDX_HEADS
    outs = [[] for _ in range(8)]
    for l in range(depth):
        w = w_in[l]
        wm = w[:, :n_main].astype(BF16)
        w_ik = w[:, n_main:n_main + IDX_DIM]
        w_iw = w[:, n_main + IDX_DIM:n_small]
        ws = jnp.concatenate(
            [w_ik, w_ik, w_iw, jnp.zeros((d, LANES - IDX_HEADS), w.dtype)], axis=1).astype(BF16)
        wg = w[:, n_small:].astype(BF16)
        wts = (norm1_g[l][None], wm, ws, wg, q_norm_g[l][None], k_norm_g[l][None],
               lower_bounds, hgrn_norm_g[l][None])
        s0_p = jnp.zeros((bp, A_HEADS, A_DK, A_DV), F32)
        oa_p, ob_p, ga_p, gb_p, kp, vp, kip, sp = _mixers(xp, bp, tp, pos_p, s0_p, None, wts, l)
        kidx2 = jnp.concatenate([cache_kidx[l], cache_kidx[l]], axis=-1).astype(BF16)
        caches = (cache_k.reshape(-1, HEAD_DIM), cache_v.reshape(-1, HEAD_DIM),
                  kidx2.reshape(bs * past, LANES), past, l * bs)
        oa_s, ob_s, ga_s, gb_s, ks, vs, kis, ss = _mixers(
            xs, bs, ts, pos_s, state_hgrn[l], caches, wts, l)

        wa = w_branch_a[l].astype(BF16)
        wb = w_branch_b[l].astype(BF16)
        wo = w_out[l].astype(BF16)
        wr = jnp.pad(w_router[l], ((0, 0), (0, LANES - N_EXPERTS))).astype(BF16)
        br = jnp.pad(b_router[l], (0, LANES - N_EXPERTS))[None]
        g2 = norm2_g[l][None]
        bgu = b_gate_up[l][:, None, :]
        bd = b_down[l][:, None, :]
        streams = []
        for (x2, oa, ob, ga, gb) in ((xp, oa_p, ob_p, ga_p, gb_p), (xs, oa_s, ob_s, ga_s, gb_s)):
            r = x2.shape[0]
            streams.append(_post(oa, ob, ga, gb, x2, wa, wb, wo, g2, wr, br, _row_tile(r, 512)))
        xp, xs = _moe(streams, w_gate_up[l], bgu, w_down[l], bd)
        for lst, v in zip(outs, (kp.reshape(bp, tp, B_KV_HEADS, HEAD_DIM),
                                 vp.reshape(bp, tp, B_KV_HEADS, HEAD_DIM),
                                 kip.reshape(bp, tp, IDX_DIM), sp,
                                 ks.reshape(bs, ts, B_KV_HEADS, HEAD_DIM),
                                 vs.reshape(bs, ts, B_KV_HEADS, HEAD_DIM),
                                 kis.reshape(bs, ts, IDX_DIM), ss)):
            lst.append(v)
    return (xp.reshape(bp, tp, d), xs.reshape(bs, ts, d)) + tuple(jnp.stack(o) for o in outs)
```

```python
import functools

import jax
import jax.numpy as jnp
from jax import lax
from jax.experimental import pallas as pl
from jax.experimental.pallas import tpu as pltpu

F32 = jnp.float32
BF16 = jnp.bfloat16
I32 = jnp.int32

CHUNK = 64
A_HEADS = 8
A_DK = 128
A_DV = 128
B_HEADS = 8
B_KV_HEADS = 4
HEAD_DIM = 128
IDX_HEADS = 8
IDX_DIM = 64
IDX_TOPK_MAX = 256
ROPE_THETA = 10000.0
N_EXPERTS = 32
TOP_K = 4
SWIGLU_LIMIT = 7.0
SWIGLU_ALPHA = 1.702
EPS = 1e-6

LANES = 128
INT_MIN = -(2 ** 31)
NEG = -1e30
VMEM_LIMIT = 56 * 1024 * 1024


def _rms(x):
    return x * lax.rsqrt(jnp.mean(x * x, axis=-1, keepdims=True) + EPS)


def _silu(x):
    return x * jax.nn.sigmoid(x)


def _dot(a, b):
    return jnp.dot(a, b, preferred_element_type=F32)


def _dot_nt(a, b):
    return lax.dot_general(a, b, (((1,), (1,)), ((), ())), preferred_element_type=F32)


def _dot_tn(a, b):
    return lax.dot_general(a, b, (((0,), (0,)), ((), ())), preferred_element_type=F32)


def _const_spec(shape):
    zeros = (0,) * len(shape)
    return pl.BlockSpec(shape, lambda *_: zeros, pipeline_mode=pl.Buffered(1))


def _rope_tables(pos, d, reps):
    inv = 1.0 / (ROPE_THETA ** (jnp.arange(0, d, 2, dtype=F32) / d))
    ang = pos.astype(F32)[:, None] * inv[None, :]
    cos = jnp.cos(ang)
    sin = jnp.sin(ang)
    cos_t = jnp.concatenate([cos, cos] * reps, axis=-1)
    sin_t = jnp.concatenate([-sin, sin] * reps, axis=-1)
    return cos_t, sin_t


def _inproj_body(x_ref, g1_ref, wm_ref, ws_ref, wg_ref, qg_ref, kg_ref, ca_ref, sa_ref, cb_ref,
                 sb_ref, aq_o, af_o, ai_o, ag_o, q_o, kf_o, kb_o, vf_o, vb_o, qim_o, kif_o, ki2_o,
                 wi_o, ga_o, gb_o, vt_o):
    x = x_ref[...]
    tm = x.shape[0]
    hb = (_rms(x) * g1_ref[...]).astype(BF16)
    a_qk = A_HEADS * A_DK
    a_v = A_HEADS * A_DV
    b_q = B_HEADS * HEAD_DIM
    b_kv = B_KV_HEADS * HEAD_DIM
    iq_w = IDX_HEADS * IDX_DIM
    o = 0
    aq_o[...] = _dot(hb, wm_ref[:, o:o + a_qk]).astype(BF16)
    o += a_qk
    af_o[...] = _dot(hb, wm_ref[:, o:o + a_qk])
    o += a_qk
    ai_o[...] = _dot(hb, wm_ref[:, o:o + a_v]).astype(BF16)
    o += a_v
    ag_o[...] = _dot(hb, wm_ref[:, o:o + a_v]).astype(BF16)
    o += a_v

    ca = ca_ref[...]
    sa = sa_ref[...]

    def rope_head(y):
        return y * ca + pltpu.roll(y, HEAD_DIM // 2, 1) * sa

    zq = _dot(hb, wm_ref[:, o:o + b_q])
    o += b_q
    for h in range(B_HEADS):
        sl = slice(h * HEAD_DIM, (h + 1) * HEAD_DIM)
        q_o[:, sl] = (rope_head(_rms(zq[:, sl]) * qg_ref[...]) * Q_SCALE).astype(BF16)
    zk = _dot(hb, wm_ref[:, o:o + b_kv])
    o += b_kv
    for h in range(B_KV_HEADS):
        sl = slice(h * HEAD_DIM, (h + 1) * HEAD_DIM)
        y = rope_head(_rms(zk[:, sl]) * kg_ref[...])
        kf_o[pl.ds(h, tm, stride=B_KV_HEADS), :] = y
        kb_o[:, sl] = y.astype(BF16)
    zv = _dot(hb, wm_ref[:, o:o + b_kv])
    o += b_kv
    for h in range(B_KV_HEADS):
        vf_o[pl.ds(h, tm, stride=B_KV_HEADS), :] = zv[:, h * HEAD_DIM:(h + 1) * HEAD_DIM]
    vb_o[...] = zv.astype(BF16)
    vt_o[0] = zv.T.astype(BF16)

    cb = cb_ref[...]
    sb = sb_ref[...]
    lane = lax.broadcasted_iota(I32, (tm, LANES), 1)
    first_half = (lane & (IDX_DIM - 1)) < (IDX_DIM // 2)

    def rope_idx(y):
        partner = jnp.where(first_half, pltpu.roll(y, LANES - IDX_DIM // 2, 1),
                            pltpu.roll(y, IDX_DIM // 2, 1))
        return y * cb + partner * sb

    zi = _dot(hb, wm_ref[:, o:o + iq_w])
    for p in range(iq_w // LANES):
        y = rope_idx(zi[:, p * LANES:(p + 1) * LANES])
        qim_o[:, (2 * p) * LANES:(2 * p + 1) * LANES] = jnp.where(lane < IDX_DIM, y, 0.0).astype(BF16)
        qim_o[:, (2 * p + 1) * LANES:(2 * p + 2) * LANES] = jnp.where(lane >= IDX_DIM, y, 0.0).astype(BF16)
    zs = _dot(hb, ws_ref[...])
    y = rope_idx(zs[:, :LANES])
    ki2_o[...] = y.astype(BF16)
    kif_o[...] = y[:, :IDX_DIM]
    wi_o[...] = zs[:, LANES:] * (IDX_HEADS ** -0.5 * IDX_DIM ** -0.5)
    zg = _dot(hb, wg_ref[...])
    d = zg.shape[1] // 2
    ga_o[...] = zg[:, :d].astype(BF16)
    gb_o[...] = zg[:, d:].astype(BF16)


def _inproj(x2, g1, wm, ws, wg, qg, kg, tabs, tm, vt_tile):
    r, d = x2.shape
    ca, sa, cb, sb = tabs
    npos = ca.shape[0] // tm
    per = vt_tile // tm
    row = lambda w: pl.BlockSpec((tm, w), lambda i: (i, 0))
    tab = pl.BlockSpec((tm, LANES), lambda i: (i % npos, 0))
    a_qk = A_HEADS * A_DK
    b_q = B_HEADS * HEAD_DIM
    b_kv = B_KV_HEADS * HEAD_DIM
    kvh = B_KV_HEADS
    outs = [
        (1, a_qk, BF16), (1, a_qk, F32), (1, a_qk, BF16), (1, a_qk, BF16),
        (1, b_q, BF16), (kvh, HEAD_DIM, F32), (1, b_kv, BF16),
        (kvh, HEAD_DIM, F32), (1, b_kv, BF16),
        (1, IDX_HEADS * LANES, BF16), (1, IDX_DIM, F32), (1, LANES, BF16),
        (1, LANES, F32), (1, d, BF16), (1, d, BF16),
    ]
    return pl.pallas_call(
        _inproj_body,
        grid=(r // tm,),
        in_specs=[row(d), _const_spec(g1.shape), _const_spec(wm.shape), _const_spec(ws.shape),
                  _const_spec(wg.shape), _const_spec(qg.shape), _const_spec(kg.shape),
                  tab, tab, tab, tab],
        out_specs=[pl.BlockSpec((tm * k, w), lambda i: (i, 0)) for k, w, _ in outs] + [
            pl.BlockSpec((1, b_kv, tm), lambda i: (i // per, 0, i % per))],
        out_shape=[jax.ShapeDtypeStruct((r * k, w), dt) for k, w, dt in outs] + [
            jax.ShapeDtypeStruct((r // vt_tile, b_kv, vt_tile), BF16)],
        compiler_params=pltpu.CompilerParams(dimension_semantics=("arbitrary",),
                                             vmem_limit_bytes=VMEM_LIMIT),
        name=f"inproj_{r // tm}",
    )(x2, g1, wm, ws, wg, qg, kg, ca, sa, cb, sb)


HGRN_BATCH = 4


def _hgrn_body(aq_ref, af_ref, ai_ref, ag_ref, s0_ref, lbp_ref, ng_ref, oa_ref, sn_ref, st_ref,
               *, layer):
    c = pl.program_id(1)
    nc = pl.num_programs(1)
    nb = aq_ref.shape[0]

    @pl.when(c == 0)
    def _():
        for b in range(nb):
            for h in range(A_HEADS):
                st_ref[b, h] = s0_ref[b, h].T

    lbp = lbp_ref[...]
    e = jnp.exp(lbp - jnp.max(lbp, axis=0, keepdims=True))
    sm = e / jnp.sum(e, axis=0, keepdims=True)
    lb = jnp.sum(sm[:layer + 1], axis=0, keepdims=True)

    n = aq_ref.shape[1]
    r_i = lax.broadcasted_iota(I32, (n, n), 0)
    c_i = lax.broadcasted_iota(I32, (n, n), 1)
    causal = r_i >= c_i
    tri = jnp.where(causal, 1.0, 0.0).astype(BF16)
    prep = []
    for b in range(nb):
        f = lb + (1.0 - lb) * jax.nn.sigmoid(af_ref[b])
        lf = jnp.log(f)
        hi = lf.astype(BF16)
        r1 = lf - hi.astype(F32)
        mid = r1.astype(BF16)
        lo = (r1 - mid.astype(F32)).astype(BF16)
        cum = _dot(tri, hi) + _dot(tri, mid) + _dot(tri, lo)
        last = cum[n - 1:n, :]
        qd = _silu(aq_ref[b].astype(F32)) * jnp.exp(cum)
        k = 1.0 - f
        prep.append((qd, k * jnp.exp(-cum), k * jnp.exp(last - cum), jnp.exp(last),
                     _silu(ag_ref[b].astype(F32))))
    for h in range(A_HEADS):
        sl = slice(h * A_DK, (h + 1) * A_DK)
        for b, (qd, kd, kt, el, gate) in enumerate(prep):
            qd_h = qd[:, sl].astype(BF16)
            v_h = ai_ref[b, :, sl]
            att = jnp.where(causal, _dot_nt(qd_h, kd[:, sl].astype(BF16)), 0.0)
            st = st_ref[b, h]
            o = _dot(att.astype(BF16), v_h) + _dot_nt(qd_h, st.astype(BF16))
            st_ref[b, h] = st * el[:, sl] + _dot_tn(v_h, kt[:, sl].astype(BF16))
            oa_ref[b, :, sl] = (_rms(o) * ng_ref[...] * gate[:, sl]).astype(BF16)

    @pl.when(c == nc - 1)
    def _():
        for b in range(nb):
            for h in range(A_HEADS):
                sn_ref[b, h] = st_ref[b, h].T


def _hgrn(aq, af, ai, ag, s0, lbp, ng, bsz, t, layer):
    d = aq.shape[1]
    nc = t // CHUNK
    nb = _row_tile(bsz, HGRN_BATCH)
    seq = lambda a: a.reshape(bsz, t, d)
    row = pl.BlockSpec((nb, CHUNK, d), lambda g, c: (g, c, 0))
    st = pl.BlockSpec((nb, A_HEADS, A_DK, A_DV), lambda g, c: (g, 0, 0, 0))
    oa, s_new = pl.pallas_call(
        functools.partial(_hgrn_body, layer=layer),
        grid=(bsz // nb, nc),
        in_specs=[row, row, row, row, st, _const_spec(lbp.shape), _const_spec(ng.shape)],
        out_specs=[row, st],
        out_shape=[jax.ShapeDtypeStruct((bsz, t, d), BF16),
                   jax.ShapeDtypeStruct((bsz, A_HEADS, A_DK, A_DV), F32)],
        scratch_shapes=[pltpu.VMEM((nb, A_HEADS, A_DV, A_DK), F32)],
        compiler_params=pltpu.CompilerParams(dimension_semantics=("arbitrary", "arbitrary"),
                                             vmem_limit_bytes=VMEM_LIMIT),
        name=f"hgrn_{bsz // nb}x{nc}",
    )(seq(aq), seq(af), seq(ai), seq(ag), s0, lbp, ng)
    return oa.reshape(bsz * t, d), s_new


def _chunk_of(pos):
    return jnp.right_shift(pos, CHUNK.bit_length() - 1)


def _transpose_wi(wi, tq):
    if tq % LANES:
        wi = jnp.concatenate([wi, jnp.zeros((LANES - tq % LANES, LANES), F32)], axis=0)
    return wi.T[:IDX_HEADS, :tq]


def _score_tile(ki2_tile, qim_ref, wt):
    acc = None
    for h in range(IDX_HEADS):
        s = _dot_nt(ki2_tile, qim_ref[:, h * LANES:(h + 1) * LANES])
        term = jnp.maximum(s, 0.0) * wt[h:h + 1, :]
        acc = term if acc is None else acc + term
    return acc


def _mask_scores(score, adm):
    score = jnp.where(score == 0.0, 0.0, score)
    return score if adm is None else jnp.where(adm, score, -jnp.inf)


def _sort_key(score):
    bits = lax.bitcast_convert_type(score, I32)
    return jnp.where(bits < 0, bits ^ 0x7FFFFFFF, bits)


def _key_to_score(key):
    return lax.bitcast_convert_type(jnp.where(key < 0, key ^ 0x7FFFFFFF, key), F32)


SUBLANES = 8
COUNT_ACCS = 4
KEY_BITS = 32
PLANE_KEYS = SUBLANES * KEY_BITS


def _store_bit_planes(planes_ref, key, blk0):
    u = key ^ INT_MIN
    tk = u.shape[0]
    for blk in range(tk // PLANE_KEYS):
        a = [u[blk * PLANE_KEYS + i * SUBLANES:blk * PLANE_KEYS + (i + 1) * SUBLANES, :]
             for i in range(KEY_BITS)]
        for j, m in ((16, 0x0000FFFF), (8, 0x00FF00FF), (4, 0x0F0F0F0F), (2, 0x33333333),
                     (1, 0x55555555)):
            k = 0
            while k < KEY_BITS:
                t = (a[k] ^ lax.shift_right_logical(a[k + j], j)) & m
                a[k] = a[k] ^ t
                a[k + j] = a[k + j] ^ lax.shift_left(t, j)
                k = (k + j + 1) & ~j
        row = pl.multiple_of((blk0 + blk) * SUBLANES, SUBLANES)
        for p in range(KEY_BITS):
            planes_ref[p, pl.ds(row, SUBLANES), :] = a[p]


def _select(sc_ref, planes_ref, bias_ref, cut_ref, stat_ref, nkt, krow, tk, tq, nbits):
    nrow = planes_ref.shape[1]
    nblk = nkt * (tk // PLANE_KEYS)
    blk_of_row = jnp.right_shift(lax.broadcasted_iota(I32, (nrow, tq), 0),
                                 SUBLANES.bit_length() - 1)

    def word_count(words):
        pc = lax.population_count(words).reshape(nrow // SUBLANES, SUBLANES, tq)
        return jnp.sum(jnp.sum(pc, axis=0).astype(F32), axis=0, keepdims=True)

    def bit_body(i, carry):
        alive, above, thr_u = carry
        ones = alive & planes_ref[i]
        c1 = word_count(ones)
        take = above + c1 >= krow
        alive = jnp.where(take, ones, alive ^ ones)
        above = jnp.where(take, above, above + c1)
        thr_u = jnp.where(take, thr_u | jnp.left_shift(jnp.int32(1), KEY_BITS - 1 - i), thr_u)
        return alive, above, thr_u

    alive0 = jnp.where(blk_of_row < nblk, -1, 0).astype(I32)
    _, _, thr_u = lax.fori_loop(
        0, KEY_BITS, bit_body, (alive0, jnp.zeros((1, tq), F32), jnp.zeros((1, tq), I32)))

    nchunk = tk // SUBLANES
    sub = lax.broadcasted_iota(I32, (SUBLANES, tq), 0)

    def count(pred_fn):
        def body(j, accs):
            t = sc_ref[j]
            accs = list(accs)
            for c in range(nchunk):
                a = c % COUNT_ACCS
                accs[a] = pred_fn(t[c * SUBLANES:(c + 1) * SUBLANES, :], j, c, accs[a])
            return tuple(accs)
        zero = jnp.zeros((SUBLANES, tq), F32)
        accs = lax.fori_loop(0, nkt, body, (zero,) * COUNT_ACCS)
        tot = accs[0]
        for a in accs[1:]:
            tot = tot + a
        return jnp.sum(tot, axis=0, keepdims=True)

    def count_ge(cand):
        c8 = jnp.broadcast_to(cand, (SUBLANES, tq))
        return count(lambda t, j, c, acc: jnp.where(t >= c8, acc + 1.0, acc))

    def count_gt(cand):
        c8 = jnp.broadcast_to(cand, (SUBLANES, tq))
        return count(lambda t, j, c, acc: jnp.where(t > c8, acc + 1.0, acc))

    def record(thr):
        stat_ref[0:1, :] = thr
        stat_ref[1:2, :] = count_ge(thr)
        stat_ref[2:3, :] = count_gt(thr)

    record(_key_to_score(thr_u ^ INT_MIN))
    good = (stat_ref[2:3, :] < krow) & (stat_ref[1:2, :] >= krow)

    @pl.when(jnp.max(jnp.where(good, 0.0, 1.0)) > 0.0)
    def _():
        def cmp_body(i, tu):
            cand_u = tu | jnp.left_shift(jnp.int32(1), KEY_BITS - 1 - i)
            return jnp.where(count_ge(_key_to_score(cand_u ^ INT_MIN)) >= krow, cand_u, tu)
        tu = lax.fori_loop(0, KEY_BITS, cmp_body, jnp.zeros((1, tq), I32))
        record(_key_to_score(tu ^ INT_MIN))

    thr = stat_ref[0:1, :]
    n_ge = stat_ref[1:2, :]
    need = krow - stat_ref[2:3, :]

    cut_ref[...] = jnp.full(cut_ref.shape, 2 ** 31 - 1, I32)

    @pl.when(jnp.max(jnp.where(n_ge > krow, 1.0, 0.0)) > 0.0)
    def _():
        thr8 = jnp.broadcast_to(thr, (SUBLANES, tq))

        def idx_body(i, cut):
            cand = cut | jnp.left_shift(jnp.int32(1), nbits - 1 - i)
            c8 = jnp.broadcast_to(cand, (SUBLANES, tq))
            below = count(lambda t, j, c, acc: jnp.where(
                t == thr8, jnp.where(j * tk + c * SUBLANES + sub < c8, acc + 1.0, acc), acc))
            return jnp.where(below < need, cand, cut)
        cut_ref[...] = lax.fori_loop(0, nbits, idx_body, jnp.zeros((1, tq), I32))

    cut = cut_ref[...]

    def write(j, carry):
        t = sc_ref[j]
        idx = j * tk + lax.broadcasted_iota(I32, (tk, tq), 0)
        tie = jnp.where(t == thr, jnp.where(idx <= cut, 0.0, NEG), NEG)
        bias_ref[j] = jnp.where(t > thr, 0.0, tie)
        return carry

    lax.fori_loop(0, nkt, write, 0)


def _stack_heads(q_ref, u, stack):
    heads = [q_ref[:, h * HEAD_DIM:(h + 1) * HEAD_DIM] for h in range(u * stack, (u + 1) * stack)]
    return heads[0] if stack == 1 else jnp.concatenate(heads, axis=0)


def _store_heads(o_ref, u, stack, o, tq):
    for r in range(stack):
        h = u * stack + r
        o_ref[:, h * HEAD_DIM:(h + 1) * HEAD_DIM] = o[r * tq:(r + 1) * tq].astype(o_ref.dtype)


Q_SCALE = HEAD_DIM ** -0.5 * 1.4426950408889634
SUM_ROWS = 16


def _attn_reset(m_ref, l_ref, acc_ref):
    m_ref[...] = jnp.full(m_ref.shape, NEG, F32)
    l_ref[...] = jnp.zeros(l_ref.shape, F32)
    acc_ref[...] = jnp.zeros(acc_ref.shape, F32)


def _attn_logits(q_ref, u, stack, kt, bias, m_ref):
    s = _dot_nt(kt, _stack_heads(q_ref, u, stack)) + jnp.concatenate([bias] * stack, axis=1)
    return s, jnp.maximum(m_ref[u], jnp.max(s, axis=0, keepdims=True))


def _attn_accumulate(u, s, m_new, m_ref, l_ref, acc_ref, pv):
    alpha = jnp.exp2(m_ref[u] - m_new)
    p = jnp.exp2(s - m_new)
    if l_ref is not None:
        l_ref[u] = alpha * l_ref[u] + jnp.sum(p, axis=0, keepdims=True)
    acc_ref[u] = alpha * acc_ref[u] + pv(p.astype(BF16))
    m_ref[u] = m_new


def _attn_step(q_ref, u, stack, kt, bias, m_ref, l_ref, acc_ref, pv):
    s, m_new = _attn_logits(q_ref, u, stack, kt, bias, m_ref)
    _attn_accumulate(u, s, m_new, m_ref, l_ref, acc_ref, pv)


def _attn_finish(o_ref, u, stack, l_ref, acc_ref, tq):
    acc = acc_ref[u]
    l = acc[HEAD_DIM:HEAD_DIM + 1] if l_ref is None else l_ref[u]
    _store_heads(o_ref, u, stack, (acc[:HEAD_DIM] / l).T, tq)


def _dsa_prompt_body(q_ref, qim_ref, wi_ref, kb_ref, vt_ref, ki2_ref, o_ref, sc_ref, planes_ref,
                     bias_ref, cut_ref, stat_ref, m_ref, l_ref, acc_ref,
                     *, tq, tk, topk, nbits):
    q0 = pl.program_id(1) * tq
    nkt = lax.div(q0 + tq + tk - 1, tk)
    wt = _transpose_wi(wi_ref[...], tq)
    qchunk = _chunk_of(q0 + lax.broadcasted_iota(I32, (1, tq), 1))

    @pl.when((pl.program_id(0) == 0) & (pl.program_id(1) == 0))
    def _():
        planes_ref[...] = jnp.zeros(planes_ref.shape, I32)

    def score_body(j, carry):
        ks = pl.multiple_of(j * tk, tk)
        score = _score_tile(ki2_ref[pl.ds(ks, tk), :], qim_ref, wt)
        kpos = ks + lax.broadcasted_iota(I32, (tk, tq), 0)
        score = _mask_scores(score, _chunk_of(kpos) <= qchunk)
        sc_ref[j] = score
        _store_bit_planes(planes_ref, _sort_key(score), j * (tk // PLANE_KEYS))
        return carry

    lax.fori_loop(0, nkt, score_body, 0)

    krow = jnp.minimum((qchunk + 1) * CHUNK, topk).astype(F32)
    _select(sc_ref, planes_ref, bias_ref, cut_ref, stat_ref, nkt, krow, tk, tq, nbits)

    _attn_reset(m_ref, l_ref, acc_ref)

    rep = B_HEADS // B_KV_HEADS
    ones = jnp.ones((SUM_ROWS, tk), BF16)

    def att_body(j, carry):
        ks = pl.multiple_of(j * tk, tk)
        bias = bias_ref[j]
        gsl = lambda g: slice(g * HEAD_DIM, (g + 1) * HEAD_DIM)
        logits = lambda g: _attn_logits(q_ref, g, rep, kb_ref[pl.ds(ks, tk), gsl(g)], bias, m_ref)
        pending = [logits(0), logits(1)]
        for g in range(B_KV_HEADS):
            if g + 2 < B_KV_HEADS:
                pending.append(logits(g + 2))
            vt = jnp.concatenate([vt_ref[j, gsl(g), :], ones], axis=0)
            _attn_accumulate(g, *pending[g], m_ref, None, acc_ref, lambda p, vt=vt: _dot(vt, p))
        return carry

    lax.fori_loop(0, nkt, att_body, 0)
    for g in range(B_KV_HEADS):
        _attn_finish(o_ref, g, rep, None, acc_ref, tq)


def _dsa_scratch(nt, tk, tq, stack, acc_rows):
    units = B_HEADS // stack
    return [pltpu.VMEM((nt, tk, tq), F32),
            pltpu.VMEM((KEY_BITS, nt * tk // KEY_BITS, tq), I32),
            pltpu.VMEM((nt, tk, tq), F32),
            pltpu.VMEM((1, tq), I32),
            pltpu.VMEM((SUBLANES, tq), F32),
            pltpu.VMEM((units, 1, stack * tq), F32),
            pltpu.VMEM((units, 1, stack * tq), F32),
            pltpu.VMEM((units, acc_rows, stack * tq), F32)]


def _dsa_prompt(q, qim, wi, kb, vt, ki2, bsz, t, tq, tk):
    nq = t // tq
    nt = t // tk
    topk = min(IDX_TOPK_MAX, t // 4)
    nbits = max(1, (t - 1).bit_length())
    qrow = lambda w: pl.BlockSpec((tq, w), lambda b, i: (b * nq + i, 0))
    seq = lambda w: pl.BlockSpec((t, w), lambda b, i: (b, 0))
    return pl.pallas_call(
        functools.partial(_dsa_prompt_body, tq=tq, tk=tk, topk=topk, nbits=nbits),
        grid=(bsz, nq),
        in_specs=[qrow(q.shape[1]), qrow(qim.shape[1]), qrow(LANES), seq(kb.shape[1]),
                  pl.BlockSpec((nt, vt.shape[1], tk), lambda b, i: (b, 0, 0)), seq(LANES)],
        out_specs=qrow(q.shape[1]),
        out_shape=jax.ShapeDtypeStruct(q.shape, BF16),
        scratch_shapes=_dsa_scratch(nt, tk, tq, B_HEADS // B_KV_HEADS, HEAD_DIM + SUM_ROWS),
        compiler_params=pltpu.CompilerParams(dimension_semantics=("arbitrary", "arbitrary"),
                                             vmem_limit_bytes=VMEM_LIMIT),
        name="dsa_prompt",
    )(q, qim, wi, kb, vt, ki2)


def _dsa_sample_body(q_ref, qim_ref, wi_ref, kn_ref, vn_ref, ki2n_ref, kidx2_ref, ck_ref, cv_ref,
                     o_ref, sc_ref, planes_ref, bias_ref, cut_ref, stat_ref, m_ref, l_ref, acc_ref,
                     *, tq, tk, nsub, npt, past, topk, nbits):
    j = pl.program_id(1)
    nj = pl.num_programs(1)
    pad = jnp.zeros((tk - tq, LANES), BF16)
    nb = tk // PLANE_KEYS

    @pl.when(j == 0)
    def _():
        wt = _transpose_wi(wi_ref[...], tq)

        def score_body(jt, carry):
            ks = pl.multiple_of(jt * tk, tk)
            score = _mask_scores(_score_tile(kidx2_ref[pl.ds(ks, tk), :], qim_ref, wt), None)
            sc_ref[jt] = score
            _store_bit_planes(planes_ref, _sort_key(score), jt * nb)
            return carry

        lax.fori_loop(0, npt, score_body, 0)
        ki2n = jnp.concatenate([ki2n_ref[...], pad], axis=0)
        krow_i = lax.broadcasted_iota(I32, (tk, tq), 0)
        qchunk = _chunk_of(past + lax.broadcasted_iota(I32, (1, tq), 1))
        adm = jnp.where(krow_i < tq, _chunk_of(past + krow_i), 2 ** 30) <= qchunk
        score = _mask_scores(_score_tile(ki2n, qim_ref, wt), adm)
        sc_ref[npt] = score
        _store_bit_planes(planes_ref, _sort_key(score), npt * nb)
        krow = jnp.minimum((qchunk + 1) * CHUNK, topk).astype(F32)
        _select(sc_ref, planes_ref, bias_ref, cut_ref, stat_ref, npt + 1, krow, tk, tq, nbits)
        _attn_reset(m_ref, l_ref, acc_ref)

    rep = B_HEADS // B_KV_HEADS

    def update(g, kt, vt, bias):
        _attn_step(q_ref, g, rep, kt, bias, m_ref, l_ref, acc_ref, lambda p: _dot_tn(vt, p))

    for su in range(nsub):
        bias = bias_ref[j * nsub + su]
        rows = lambda g: pl.ds(su * tk * B_KV_HEADS + g, tk, stride=B_KV_HEADS)
        logits = lambda g: _attn_logits(q_ref, g, rep, ck_ref[rows(g), :].astype(BF16), bias,
                                        m_ref)
        pending = [logits(g) for g in range(B_KV_HEADS)]
        for g in range(B_KV_HEADS):
            vt = cv_ref[rows(g), :].astype(BF16)
            _attn_accumulate(g, *pending[g], m_ref, l_ref, acc_ref,
                             lambda p, vt=vt: _dot_tn(vt, p))

    @pl.when(j == nj - 1)
    def _():
        bias = bias_ref[npt]
        for g in range(B_KV_HEADS):
            gs = slice(g * HEAD_DIM, (g + 1) * HEAD_DIM)
            kt = jnp.concatenate([kn_ref[:, gs], pad], axis=0)
            vt = jnp.concatenate([vn_ref[:, gs], pad], axis=0)
            update(g, kt, vt, bias)
            _attn_finish(o_ref, g, rep, l_ref, acc_ref, tq)


def _dsa_sample(q, qim, wi, kn, vn, ki2n, kidx2, ck, cv, bsz, t, past, tk, nsub, cache_off):
    tq = t
    npt = past // tk
    nj = npt // nsub
    topk = min(IDX_TOPK_MAX, (past + t) // 4)
    nbits = max(1, (past + t - 1).bit_length())
    rep = B_HEADS // B_KV_HEADS
    qrow = lambda w: pl.BlockSpec((tq, w), lambda b, j: (b, 0))
    kvw = kn.shape[1]
    cache = pl.BlockSpec((nsub * tk * B_KV_HEADS, HEAD_DIM),
                         lambda b, j: ((cache_off + b) * nj + j, 0))
    return pl.pallas_call(
        functools.partial(_dsa_sample_body, tq=tq, tk=tk, nsub=nsub, npt=npt, past=past,
                          topk=topk, nbits=nbits),
        grid=(bsz, nj),
        in_specs=[qrow(q.shape[1]), qrow(qim.shape[1]), qrow(LANES), qrow(kvw), qrow(kvw),
                  qrow(LANES), pl.BlockSpec((past, LANES), lambda b, j: (b, 0)), cache, cache],
        out_specs=qrow(q.shape[1]),
        out_shape=jax.ShapeDtypeStruct(q.shape, BF16),
        scratch_shapes=_dsa_scratch(npt + 1, tk, tq, rep, HEAD_DIM),
        compiler_params=pltpu.CompilerParams(dimension_semantics=("arbitrary", "arbitrary"),
                                             vmem_limit_bytes=VMEM_LIMIT),
        name="dsa_sample",
    )(q, qim, wi, kn, vn, ki2n, kidx2, ck, cv)


def _post_body(oa_ref, ob_ref, ga_ref, gb_ref, x_ref, wa_ref, wb_ref, wo_ref, g2_ref, wr_ref,
               br_ref, x1_o, t_o, eidx_o, prob_o, cnt_o):
    ya = _dot(oa_ref[...], wa_ref[...])
    yb = _dot(ob_ref[...], wb_ref[...])
    merged = (jax.nn.sigmoid(ga_ref[...].astype(F32)) * ya
              + jax.nn.sigmoid(gb_ref[...].astype(F32)) * yb)
    x1 = x_ref[...] + _dot(merged.astype(BF16), wo_ref[...])
    x1_o[...] = x1
    tok = _rms(x1) * g2_ref[...]
    t_o[...] = tok
    logits = _dot(tok.astype(BF16), wr_ref[...]) + br_ref[...]
    tm = logits.shape[0]
    lane = lax.broadcasted_iota(I32, (tm, LANES), 1).astype(F32)
    cur = jnp.where(lane < N_EXPERTS, logits, -jnp.inf)
    top = None
    den = jnp.zeros((tm, 1), F32)
    eidx = jnp.zeros((tm, LANES), F32)
    prob = jnp.zeros((tm, LANES), F32)
    chosen = jnp.zeros((tm, LANES), F32)
    for k in range(TOP_K):
        mx = jnp.max(cur, axis=1, keepdims=True)
        first = jnp.min(jnp.where(cur == mx, lane, float(LANES)), axis=1, keepdims=True)
        if top is None:
            top = mx
        e = jnp.exp(mx - top)
        den = den + e
        eidx = jnp.where(lane == k, first, eidx)
        prob = jnp.where(lane == k, e, prob)
        hit = lane == first
        chosen = jnp.where(hit, 1.0, chosen)
        cur = jnp.where(hit, -jnp.inf, cur)
    eidx_o[...] = eidx.astype(I32)
    prob_o[...] = prob / den

    @pl.when(pl.program_id(0) == 0)
    def _():
        cnt_o[...] = jnp.zeros(cnt_o.shape, F32)

    cnt_o[...] += jnp.sum(chosen, axis=0, keepdims=True)


def _post(oa, ob, ga, gb, x2, wa, wb, wo, g2, wr, br, tm):
    r, d = x2.shape
    row = lambda w: pl.BlockSpec((tm, w), lambda i: (i, 0))
    return pl.pallas_call(
        _post_body,
        grid=(r // tm,),
        in_specs=[row(d), row(d), row(d), row(d), row(d), _const_spec(wa.shape),
                  _const_spec(wb.shape), _const_spec(wo.shape), _const_spec(g2.shape),
                  _const_spec(wr.shape), _const_spec(br.shape)],
        out_specs=[row(d), row(d), row(LANES), row(LANES),
                   pl.BlockSpec((1, LANES), lambda i: (0, 0))],
        out_shape=[jax.ShapeDtypeStruct((r, d), F32), jax.ShapeDtypeStruct((r, d), F32),
                   jax.ShapeDtypeStruct((r, LANES), I32), jax.ShapeDtypeStruct((r, LANES), F32),
                   jax.ShapeDtypeStruct((1, LANES), F32)],
        compiler_params=pltpu.CompilerParams(dimension_semantics=("arbitrary",),
                                             vmem_limit_bytes=VMEM_LIMIT),
        name=f"post_{r // tm}",
    )(oa, ob, ga, gb, x2, wa, wb, wo, g2, wr, br)


MOE_TILE = 512
DMA_PRIORITIES = 2
ISSUE_UNROLL = 8


WIN_ROWS = 64
FLAG_LANE = LANES - 1


def _rank_body(eidx_ref, cin_ref, pos_o, col_o, win_o, carry_ref, *, max_start):
    @pl.when(pl.program_id(0) == 0)
    def _():
        carry_ref[...] = cin_ref[...]

    eidx = eidx_ref[...]
    tm = eidx.shape[0]
    lane = lax.broadcasted_iota(I32, (tm, LANES), 1)
    hits = [lane == eidx[:, k:k + 1] for k in range(TOP_K)]
    onehot = jnp.zeros((tm, LANES), F32)
    for hit in hits:
        onehot = onehot + jnp.where(hit, 1.0, 0.0)
    r_i = lax.broadcasted_iota(I32, (tm, tm), 0)
    c_i = lax.broadcasted_iota(I32, (tm, tm), 1)
    before = jnp.where(c_i < r_i, 1.0, 0.0).astype(BF16)
    carry = carry_ref[...]
    start = jnp.minimum((carry.astype(I32) >> 3) << 3, max_start)
    base = carry + _dot(before, onehot.astype(BF16))
    rank = jnp.zeros((tm, LANES), F32)
    col = jnp.zeros((tm, LANES), F32)
    worst = jnp.zeros((tm, 1), F32)
    for k, hit in enumerate(hits):
        rk = jnp.sum(jnp.where(hit, base, 0.0), axis=1, keepdims=True)
        off = jnp.sum(jnp.where(hit, base - start.astype(F32), 0.0), axis=1, keepdims=True)
        rank = jnp.where(lane == k, rk, rank)
        col = jnp.where(lane == k, eidx[:, k:k + 1].astype(F32) * WIN_ROWS + off, col)
        worst = jnp.maximum(worst, off)
    pos_o[...] = rank.T[:SUBLANES, :].astype(I32)
    col_o[...] = col.astype(I32)
    flag = (jnp.max(worst) >= WIN_ROWS).astype(I32)
    win_o[...] = jnp.where(lane[:1] == FLAG_LANE, flag, start).reshape(win_o.shape)
    carry_ref[...] += jnp.sum(onehot, axis=0, keepdims=True)


def _rank(eidx, cin, tm, max_start):
    r = eidx.shape[0]
    return pl.pallas_call(
        functools.partial(_rank_body, max_start=max_start),
        grid=(r // tm,),
        in_specs=[pl.BlockSpec((tm, LANES), lambda i: (i, 0)), _const_spec(cin.shape)],
        out_specs=[pl.BlockSpec((SUBLANES, tm), lambda i: (0, i)),
                   pl.BlockSpec((tm, LANES), lambda i: (i, 0)),
                   pl.BlockSpec((1, 1, LANES), lambda i: (i, 0, 0))],
        out_shape=[jax.ShapeDtypeStruct((SUBLANES, r), I32),
                   jax.ShapeDtypeStruct((r, LANES), I32),
                   jax.ShapeDtypeStruct((r // tm, 1, LANES), I32)],
        scratch_shapes=[pltpu.VMEM((1, LANES), F32)],
        compiler_params=pltpu.CompilerParams(dimension_semantics=("arbitrary",)),
        name=f"moe_rank_{r // tm}",
    )(eidx, cin)


def _wait_rows(ref, sem, times):
    for _ in range(times):
        pltpu.make_async_copy(ref, ref, sem).wait()


def _zero_pad_rows(pad_ref, xs_out, zeros, sem):
    zeros[...] = jnp.zeros(zeros.shape, zeros.dtype)
    row = zeros.at[pl.ds(0, 1)]
    tile = zeros.shape[0]

    def for_each_copy(fn):
        def per_group(g, carry):
            first = pad_ref[0, g]

            def per_row(t, c):
                fn(pltpu.make_async_copy(row, xs_out.at[pl.ds(first + t, 1)], sem))
                return c
            return lax.fori_loop(0, pad_ref[1, g], per_row, carry)
        lax.fori_loop(0, N_EXPERTS, per_group, 0)
        first = pad_ref[0, N_EXPERTS]

        def per_tile(t, c):
            fn(pltpu.make_async_copy(
                zeros, xs_out.at[pl.ds(pl.multiple_of(first + t * tile, tile), tile)], sem))
            return c
        lax.fori_loop(0, lax.div(pad_ref[1, N_EXPERTS], tile), per_tile, 0)

    for_each_copy(lambda copy: copy.start())
    for_each_copy(lambda copy: copy.wait())


def _dispatch_body(pad_ref, *refs, bounds):
    ns = len(bounds) - 1
    pos_refs, tok_refs = refs[:ns], refs[ns:2 * ns]
    xs_out, ring, sems, zeros, zsem = refs[2 * ns:]
    i = pl.program_id(0)
    tm = ring.shape[1]
    last = pl.num_programs(0) - 1

    pl.when(i == 0)(functools.partial(_zero_pad_rows, pad_ref, xs_out, zeros, zsem))

    for slot in range(2):
        mine = i % 2 == slot
        for s in range(ns):
            @pl.when(mine & (i >= bounds[s]) & (i < bounds[s + 1]))
            def _(slot=slot, s=s):
                ring[slot] = tok_refs[s][...]

                def issue(t, carry):
                    for k in range(TOP_K):
                        pltpu.make_async_copy(ring.at[slot, pl.ds(t, 1)],
                                              xs_out.at[pl.ds(pos_refs[s][t * TOP_K + k], 1)],
                                              sems.at[slot]).start(priority=k % DMA_PRIORITIES)
                    return carry

                lax.fori_loop(0, tm, issue, 0)

        @pl.when(mine & (i > 0))
        def _(slot=slot):
            _wait_rows(ring.at[1 - slot], sems.at[1 - slot], TOP_K)

        @pl.when(mine & (i == last))
        def _(slot=slot):
            _wait_rows(ring.at[slot], sems.at[slot], TOP_K)


def _dispatch(poss, pad, toks, rows, tm):
    d = toks[0].shape[1]
    bounds = [0]
    for tok in toks:
        bounds.append(bounds[-1] + tok.shape[0] // tm)

    def local(s):
        lo, n = bounds[s], bounds[s + 1] - bounds[s]
        return lambda i: jnp.clip(i - lo, 0, n - 1)

    in_specs = [pl.BlockSpec(memory_space=pltpu.SMEM)]
    in_specs += [pl.BlockSpec((tm * TOP_K,), lambda i, f=local(s): (f(i),),
                              memory_space=pltpu.SMEM) for s in range(len(toks))]
    in_specs += [pl.BlockSpec((tm, d), lambda i, f=local(s): (f(i), 0)) for s in range(len(toks))]
    return pl.pallas_call(
        functools.partial(_dispatch_body, bounds=tuple(bounds)),
        grid=(bounds[-1],),
        in_specs=in_specs,
        out_specs=pl.BlockSpec(memory_space=pl.ANY),
        out_shape=jax.ShapeDtypeStruct((rows, d), toks[0].dtype),
        scratch_shapes=[pltpu.VMEM((2, tm, d), toks[0].dtype), pltpu.SemaphoreType.DMA((2,)),
                        pltpu.VMEM((MOE_TILE, d), toks[0].dtype), pltpu.SemaphoreType.DMA],
        compiler_params=pltpu.CompilerParams(dimension_semantics=("arbitrary",),
                                             has_side_effects=True),
        name="moe_dispatch",
    )(pad, *poss, *toks)


def _experts_body(te_ref, na_ref, x_ref, wgu_ref, bgu_ref, wd_ref, bd_ref, y_ref, wgu_s, wd_s):
    r = pl.program_id(0)
    e = te_ref[r]
    prev = te_ref[jnp.maximum(r - 1, 0)]

    @pl.when((r == 0) | (e != prev))
    def _():
        rows = 128
        for c in range(wgu_s.shape[0] // rows):
            wgu_s[c * rows:(c + 1) * rows, :] = wgu_ref[0, c * rows:(c + 1) * rows, :].astype(BF16)
        for c in range(wd_s.shape[0] // rows):
            wd_s[c * rows:(c + 1) * rows, :] = wd_ref[0, c * rows:(c + 1) * rows, :].astype(BF16)

    @pl.when(r < na_ref[0])
    def _():
        gu = _dot(x_ref[...].astype(BF16), wgu_s[...]) + bgu_ref[0]
        dff = gu.shape[1] // 2
        gate = jnp.minimum(gu[:, :dff], SWIGLU_LIMIT)
        up = jnp.clip(gu[:, dff:], -SWIGLU_LIMIT, SWIGLU_LIMIT)
        act = (up + 1.0) * gate * jax.nn.sigmoid(SWIGLU_ALPHA * gate)
        y_ref[...] = _dot(act.astype(BF16), wd_s[...]) + bd_ref[0]

    @pl.when(r >= na_ref[0])
    def _():
        y_ref[...] = jnp.zeros(y_ref.shape, F32)


def _experts(tile_expert, n_active, xs, wgu, bgu, wd, bd):
    p, d = xs.shape
    _, _, dff2 = wgu.shape
    tm = MOE_TILE
    grid_spec = pltpu.PrefetchScalarGridSpec(
        num_scalar_prefetch=2,
        grid=(p // tm,),
        in_specs=[pl.BlockSpec((tm, d), lambda r, te, na: (r, 0)),
                  pl.BlockSpec((1, d, dff2), lambda r, te, na: (te[r], 0, 0)),
                  pl.BlockSpec((1, 1, dff2), lambda r, te, na: (te[r], 0, 0)),
                  pl.BlockSpec((1, dff2 // 2, d), lambda r, te, na: (te[r], 0, 0)),
                  pl.BlockSpec((1, 1, d), lambda r, te, na: (te[r], 0, 0))],
        out_specs=pl.BlockSpec((tm, d), lambda r, te, na: (r, 0)),
        scratch_shapes=[pltpu.VMEM((d, dff2), BF16), pltpu.VMEM((dff2 // 2, d), BF16)],
    )
    return pl.pallas_call(
        _experts_body,
        grid_spec=grid_spec,
        out_shape=jax.ShapeDtypeStruct((p, d), F32),
        compiler_params=pltpu.CompilerParams(dimension_semantics=("arbitrary",),
                                             vmem_limit_bytes=VMEM_LIMIT),
        name="moe_experts",
    )(tile_expert, n_active, xs, wgu, bgu, wd, bd)


def _combine_body(slow_ref, win_ref, pos_ref, col_ref, prob_ref, x1_ref, y_hbm, out_ref,
                  wbuf, rbuf, sems):
    i = pl.program_id(0)
    n = pl.num_programs(0) - 1
    tm = x1_ref.shape[0]
    slow = slow_ref[0] != 0
    nwin = wbuf.shape[1] // WIN_ROWS

    for slot in range(2):
        mine = i % 2 == slot

        @pl.when(mine & (i < n) & jnp.logical_not(slow))
        def _(slot=slot):
            for e in range(nwin):
                first = pl.multiple_of(win_ref[0, 0, e], SUBLANES)
                pltpu.make_async_copy(y_hbm.at[pl.ds(first, WIN_ROWS)],
                                      wbuf.at[slot, pl.ds(e * WIN_ROWS, WIN_ROWS)],
                                      sems.at[slot]).start()

        @pl.when(mine & (i > 0) & jnp.logical_not(slow))
        def _(slot=slot):
            _wait_rows(wbuf.at[1 - slot], sems.at[1 - slot], 1)
            col = col_ref[...]
            prob = prob_ref[...]
            lane = lax.broadcasted_iota(I32, (tm, LANES), 1)
            rel = [jnp.broadcast_to(col[:, k:k + 1], (tm, LANES)) - lane for k in range(TOP_K)]
            wgt = [jnp.broadcast_to(prob[:, k:k + 1], (tm, LANES)) for k in range(TOP_K)]
            pick = []
            for c in range(wbuf.shape[1] // LANES):
                g = jnp.zeros((tm, LANES), F32)
                for k in range(TOP_K):
                    g = jnp.where(rel[k] == c * LANES, wgt[k], g)
                pick.append(g.astype(BF16))
            ffn = _dot(jnp.concatenate(pick, axis=1), wbuf[1 - slot].astype(BF16))
            out_ref[...] = x1_ref[...] + ffn

        @pl.when(mine & (i < n) & slow)
        def _(slot=slot):
            def issue(t, carry):
                for k in range(TOP_K):
                    pltpu.make_async_copy(y_hbm.at[pl.ds(pos_ref[k, t], 1)],
                                          rbuf.at[slot, k, pl.ds(t, 1)],
                                          sems.at[slot]).start(priority=k % DMA_PRIORITIES)
                return carry
            lax.fori_loop(0, tm, issue, 0)

        @pl.when(mine & (i > 0) & slow)
        def _(slot=slot):
            _wait_rows(rbuf.at[1 - slot, 0], sems.at[1 - slot], TOP_K)
            prob = prob_ref[...]
            ffn = prob[:, 0:1] * rbuf[1 - slot, 0]
            for k in range(1, TOP_K):
                ffn = ffn + prob[:, k:k + 1] * rbuf[1 - slot, k]
            out_ref[...] = x1_ref[...] + ffn


def _combine(slow, win, pos, col, prob, x1, ys, tm):
    r, d = x1.shape
    n = r // tm
    prev = lambda w: pl.BlockSpec((tm, w), lambda i: (jnp.maximum(i - 1, 0), 0))
    cur = lambda i: jnp.minimum(i, n - 1)
    return pl.pallas_call(
        _combine_body,
        grid=(n + 1,),
        in_specs=[pl.BlockSpec(memory_space=pltpu.SMEM),
                  pl.BlockSpec((1, 1, LANES), lambda i: (cur(i), 0, 0), memory_space=pltpu.SMEM),
                  pl.BlockSpec((SUBLANES, tm), lambda i: (0, cur(i)), memory_space=pltpu.SMEM),
                  prev(LANES), prev(LANES), prev(d), pl.BlockSpec(memory_space=pl.ANY)],
        out_specs=prev(d),
        out_shape=jax.ShapeDtypeStruct((r, d), F32),
        scratch_shapes=[pltpu.VMEM((2, N_EXPERTS * WIN_ROWS, d), F32),
                        pltpu.VMEM((2, TOP_K, tm, d), F32), pltpu.SemaphoreType.DMA((2,))],
        compiler_params=pltpu.CompilerParams(dimension_semantics=("arbitrary",),
                                             vmem_limit_bytes=VMEM_LIMIT),
        name=f"moe_combine_{n}",
    )(slow, win, pos, col, prob, x1, ys)


def _moe(streams, wgu, bgu, wd, bd):
    tm = MOE_TILE
    counts = [s[4][0, :N_EXPERTS].astype(I32) for s in streams]
    cnt = sum(counts)
    padded = ((cnt + tm - 1) // tm) * tm
    ends = jnp.cumsum(padded)
    n_pairs = sum(s[0].shape[0] for s in streams) * TOP_K
    n_tiles = n_pairs // tm + N_EXPERTS
    tile_start = jnp.arange(n_tiles, dtype=I32) * tm
    tile_expert = jnp.minimum(jnp.sum((ends[None, :] <= tile_start[:, None]).astype(I32), axis=1),
                              N_EXPERTS - 1)
    n_active = (ends[-1:] // tm).astype(I32)
    start = ends - padded
    pad = jnp.stack([jnp.append(start + cnt, ends[-1]),
                     jnp.append(padded - cnt, n_tiles * tm - ends[-1])])
    routes = []
    for (x1, tok, eidx, prob, _), c in zip(streams, counts):
        cin = jnp.pad(start, (0, LANES - N_EXPERTS)).astype(F32)[None]
        routes.append(_rank(eidx, cin, _row_tile(eidx.shape[0], 256), n_tiles * tm - WIN_ROWS))
        start = start + c
    flat = [r[0][:TOP_K].T.reshape(-1) for r in routes]
    xs = _dispatch(flat, pad, [s[1] for s in streams], n_tiles * tm,
                   min(_row_tile(s[1].shape[0], 256) for s in streams))
    ys = _experts(tile_expert, n_active, xs, wgu, bgu, wd, bd)
    outs = []
    for s, (pos, col, win) in zip(streams, routes):
        slow = jnp.max(win[:, 0, FLAG_LANE]).reshape(1)
        outs.append(_combine(slow, win, pos, col, s[3], s[0], ys, _row_tile(s[0].shape[0], 256)))
    return outs


def _row_tile(r, want):
    tm = min(r, want)
    assert r % tm == 0, (r, tm)
    return tm


def _mixers(x2, bsz, t, pos, s0, caches, wts, layer):
    (g1, wm, ws, wg, qg, kg, lbp, ng) = wts
    r = x2.shape[0]
    tm = _row_tile(r, 256)
    assert t % tm == 0 or tm % t == 0
    tabs_a = _rope_tables(pos, HEAD_DIM, 1)
    tabs_b = _rope_tables(pos, IDX_DIM, LANES // IDX_DIM)
    tabs = tabs_a + tabs_b
    if tm > t:
        tabs = tuple(jnp.tile(tb, (tm // t, 1)) for tb in tabs)
    tk = max(tm, _row_tile(t, 512)) if caches is None else tm
    (aq, af, ai, ag, q, kf, kb, vf, vb, qim, kif, ki2, wi, ga, gb, vt) = _inproj(
        x2, g1, wm, ws, wg, qg, kg, tabs, tm, tk)
    oa, s_new = _hgrn(aq, af, ai, ag, s0, lbp, ng, bsz, t, layer)
    if caches is None:
        ob = _dsa_prompt(q, qim, wi, kb, vt, ki2, bsz, t, _row_tile(t, 256), tk)
    else:
        ck, cv, kidx2, past, cache_off = caches
        tk = _row_tile(past, 512)
        nsub = 2 if (past // tk) % 2 == 0 else 1
        ob = _dsa_sample(q, qim, wi, kb, vb, ki2, kidx2, ck, cv, bsz, t, past, tk, nsub, cache_off)
    return oa, ob, ga, gb, kf, vf, kif, s_new


def kernel(x_prompt, x_sample, cache_k, cache_v, cache_kidx, state_hgrn, norm1_g, w_in, lower_bounds, hgrn_norm_g, q_norm_g, k_norm_g, w_branch_a, w_branch_b, w_out, norm2_g, w_router, b_router, w_gate_up, b_gate_up, w_down, b_down):
    bp, tp, d = x_prompt.shape
    bs, ts, _ = x_sample.shape
    depth = w_in.shape[0]
    past = cache_k.shape[2]
    kvw = B_KV_HEADS * HEAD_DIM
    pos_p = jnp.arange(tp, dtype=I32)
    pos_s = past + jnp.arange(ts, dtype=I32)
    xp = x_prompt.reshape(bp * tp, d)
    xs = x_sample.reshape(bs * ts, d)
    n_main = 2 * A_HEADS * A_DK + 2 * A_HEADS * A_DV + B_HEADS * HEAD_DIM + 2 * kvw + IDX_HEADS * IDX_DIM
    n_small = n_main + IDX_DIM + IDX_HEADS
    outs = [[] for _ in range(8)]
    for l in range(depth):
        w = w_in[l]
        wm = w[:, :n_main].astype(BF16)
        w_ik = w[:, n_main:n_main + IDX_DIM]
        w_iw = w[:, n_main + IDX_DIM:n_small]
        ws = jnp.concatenate(
            [w_ik, w_ik, w_iw, jnp.zeros((d, LANES - IDX_HEADS), w.dtype)], axis=1).astype(BF16)
        wg = w[:, n_small:].astype(BF16)
        wts = (norm1_g[l][None], wm, ws, wg, q_norm_g[l][None], k_norm_g[l][None],
               lower_bounds, hgrn_norm_g[l][None])
        s0_p = jnp.zeros((bp, A_HEADS, A_DK, A_DV), F32)
        oa_p, ob_p, ga_p, gb_p, kp, vp, kip, sp = _mixers(xp, bp, tp, pos_p, s0_p, None, wts, l)
        kidx2 = jnp.concatenate([cache_kidx[l], cache_kidx[l]], axis=-1).astype(BF16)
        caches = (cache_k.reshape(-1, HEAD_DIM), cache_v.reshape(-1, HEAD_DIM),
                  kidx2.reshape(bs * past, LANES), past, l * bs)
        oa_s, ob_s, ga_s, gb_s, ks, vs, kis, ss = _mixers(
            xs, bs, ts, pos_s, state_hgrn[l], caches, wts, l)

        wa = w_branch_a[l].astype(BF16)
        wb = w_branch_b[l].astype(BF16)
        wo = w_out[l].astype(BF16)
        wr = jnp.pad(w_router[l], ((0, 0), (0, LANES - N_EXPERTS))).astype(BF16)
        br = jnp.pad(b_router[l], (0, LANES - N_EXPERTS))[None]
        g2 = norm2_g[l][None]
        bgu = b_gate_up[l][:, None, :]
        bd = b_down[l][:, None, :]
        streams = []
        for (x2, oa, ob, ga, gb) in ((xp, oa_p, ob_p, ga_p, gb_p), (xs, oa_s, ob_s, ga_s, gb_s)):
            r = x2.shape[0]
            streams.append(_post(oa, ob, ga, gb, x2, wa, wb, wo, g2, wr, br, _row_tile(r, 512)))
        xp, xs = _moe(streams, w_gate_up[l], bgu, w_down[l], bd)
        for lst, v in zip(outs, (kp.reshape(bp, tp, B_KV_HEADS, HEAD_DIM),
                                 vp.reshape(bp, tp, B_KV_HEADS, HEAD_DIM),
                                 kip.reshape(bp, tp, IDX_DIM), sp,
                                 ks.reshape(bs, ts, B_KV_HEADS, HEAD_DIM),
                                 vs.reshape(bs, ts, B_KV_HEADS, HEAD_DIM),
                                 kis.reshape(bs, ts, IDX_DIM), ss)):
            lst.append(v)
    return (xp.reshape(bp, tp, d), xs.reshape(bs, ts, d)) + tuple(jnp.stack(o) for o in outs)
```

```python
import functools

import jax
import jax.numpy as jnp
from jax import lax
from jax.experimental import pallas as pl
from jax.experimental.pallas import tpu as pltpu

F32 = jnp.float32
BF16 = jnp.bfloat16
I32 = jnp.int32

CHUNK = 64
A_HEADS = 8
A_DK = 128
A_DV = 128
B_HEADS = 8
B_KV_HEADS = 4
HEAD_DIM = 128
IDX_HEADS = 8
IDX_DIM = 64
IDX_TOPK_MAX = 256
ROPE_THETA = 10000.0
N_EXPERTS = 32
TOP_K = 4
SWIGLU_LIMIT = 7.0
SWIGLU_ALPHA = 1.702
EPS = 1e-6

LANES = 128
INT_MIN = -(2 ** 31)
NEG = -1e30
VMEM_LIMIT = 56 * 1024 * 1024


def _rms(x):
    return x * lax.rsqrt(jnp.mean(x * x, axis=-1, keepdims=True) + EPS)


def _silu(x):
    return x * jax.nn.sigmoid(x)


def _dot(a, b):
    return jnp.dot(a, b, preferred_element_type=F32)


def _dot_nt(a, b):
    return lax.dot_general(a, b, (((1,), (1,)), ((), ())), preferred_element_type=F32)


def _dot_tn(a, b):
    return lax.dot_general(a, b, (((0,), (0,)), ((), ())), preferred_element_type=F32)


def _const_spec(shape):
    zeros = (0,) * len(shape)
    return pl.BlockSpec(shape, lambda *_: zeros, pipeline_mode=pl.Buffered(1))


def _rope_tables(pos, d, reps):
    inv = 1.0 / (ROPE_THETA ** (jnp.arange(0, d, 2, dtype=F32) / d))
    ang = pos.astype(F32)[:, None] * inv[None, :]
    cos = jnp.cos(ang)
    sin = jnp.sin(ang)
    cos_t = jnp.concatenate([cos, cos] * reps, axis=-1)
    sin_t = jnp.concatenate([-sin, sin] * reps, axis=-1)
    return cos_t, sin_t


def _inproj_body(x_ref, g1_ref, wm_ref, ws_ref, wg_ref, qg_ref, kg_ref, ca_ref, sa_ref, cb_ref,
                 sb_ref, aq_o, af_o, ai_o, ag_o, q_o, kf_o, kb_o, vf_o, vb_o, qim_o, kif_o, ki2_o,
                 wi_o, ga_o, gb_o, vt_o):
    x = x_ref[...]
    tm = x.shape[0]
    hb = (_rms(x) * g1_ref[...]).astype(BF16)
    a_qk = A_HEADS * A_DK
    a_v = A_HEADS * A_DV
    b_q = B_HEADS * HEAD_DIM
    b_kv = B_KV_HEADS * HEAD_DIM
    iq_w = IDX_HEADS * IDX_DIM
    o = 0
    aq_o[...] = _dot(hb, wm_ref[:, o:o + a_qk]).astype(BF16)
    o += a_qk
    af_o[...] = _dot(hb, wm_ref[:, o:o + a_qk])
    o += a_qk
    ai_o[...] = _dot(hb, wm_ref[:, o:o + a_v]).astype(BF16)
    o += a_v
    ag_o[...] = _dot(hb, wm_ref[:, o:o + a_v]).astype(BF16)
    o += a_v

    ca = ca_ref[...]
    sa = sa_ref[...]

    def rope_head(y):
        return y * ca + pltpu.roll(y, HEAD_DIM // 2, 1) * sa

    zq = _dot(hb, wm_ref[:, o:o + b_q])
    o += b_q
    for h in range(B_HEADS):
        sl = slice(h * HEAD_DIM, (h + 1) * HEAD_DIM)
        q_o[:, sl] = (rope_head(_rms(zq[:, sl]) * qg_ref[...]) * Q_SCALE).astype(BF16)
    zk = _dot(hb, wm_ref[:, o:o + b_kv])
    o += b_kv
    for h in range(B_KV_HEADS):
        sl = slice(h * HEAD_DIM, (h + 1) * HEAD_DIM)
        y = rope_head(_rms(zk[:, sl]) * kg_ref[...])
        kf_o[pl.ds(h, tm, stride=B_KV_HEADS), :] = y
        kb_o[:, sl] = y.astype(BF16)
    zv = _dot(hb, wm_ref[:, o:o + b_kv])
    o += b_kv
    for h in range(B_KV_HEADS):
        vf_o[pl.ds(h, tm, stride=B_KV_HEADS), :] = zv[:, h * HEAD_DIM:(h + 1) * HEAD_DIM]
    vb_o[...] = zv.astype(BF16)
    vt_o[0] = zv.T.astype(BF16)

    cb = cb_ref[...]
    sb = sb_ref[...]
    lane = lax.broadcasted_iota(I32, (tm, LANES), 1)
    first_half = (lane & (IDX_DIM - 1)) < (IDX_DIM // 2)

    def rope_idx(y):
        partner = jnp.where(first_half, pltpu.roll(y, LANES - IDX_DIM // 2, 1),
                            pltpu.roll(y, IDX_DIM // 2, 1))
        return y * cb + partner * sb

    zi = _dot(hb, wm_ref[:, o:o + iq_w])
    for p in range(iq_w // LANES):
        y = rope_idx(zi[:, p * LANES:(p + 1) * LANES])
        qim_o[:, (2 * p) * LANES:(2 * p + 1) * LANES] = jnp.where(lane < IDX_DIM, y, 0.0).astype(BF16)
        qim_o[:, (2 * p + 1) * LANES:(2 * p + 2) * LANES] = jnp.where(lane >= IDX_DIM, y, 0.0).astype(BF16)
    zs = _dot(hb, ws_ref[...])
    y = rope_idx(zs[:, :LANES])
    ki2_o[...] = y.astype(BF16)
    kif_o[...] = y[:, :IDX_DIM]
    wi_o[...] = zs[:, LANES:] * (IDX_HEADS ** -0.5 * IDX_DIM ** -0.5)
    zg = _dot(hb, wg_ref[...])
    d = zg.shape[1] // 2
    ga_o[...] = zg[:, :d].astype(BF16)
    gb_o[...] = zg[:, d:].astype(BF16)


def _inproj(x2, g1, wm, ws, wg, qg, kg, tabs, tm, vt_tile):
    r, d = x2.shape
    ca, sa, cb, sb = tabs
    npos = ca.shape[0] // tm
    per = vt_tile // tm
    row = lambda w: pl.BlockSpec((tm, w), lambda i: (i, 0))
    tab = pl.BlockSpec((tm, LANES), lambda i: (i % npos, 0))
    a_qk = A_HEADS * A_DK
    b_q = B_HEADS * HEAD_DIM
    b_kv = B_KV_HEADS * HEAD_DIM
    kvh = B_KV_HEADS
    outs = [
        (1, a_qk, BF16), (1, a_qk, F32), (1, a_qk, BF16), (1, a_qk, BF16),
        (1, b_q, BF16), (kvh, HEAD_DIM, F32), (1, b_kv, BF16),
        (kvh, HEAD_DIM, F32), (1, b_kv, BF16),
        (1, IDX_HEADS * LANES, BF16), (1, IDX_DIM, F32), (1, LANES, BF16),
        (1, LANES, F32), (1, d, BF16), (1, d, BF16),
    ]
    return pl.pallas_call(
        _inproj_body,
        grid=(r // tm,),
        in_specs=[row(d), _const_spec(g1.shape), _const_spec(wm.shape), _const_spec(ws.shape),
                  _const_spec(wg.shape), _const_spec(qg.shape), _const_spec(kg.shape),
                  tab, tab, tab, tab],
        out_specs=[pl.BlockSpec((tm * k, w), lambda i: (i, 0)) for k, w, _ in outs] + [
            pl.BlockSpec((1, b_kv, tm), lambda i: (i // per, 0, i % per))],
        out_shape=[jax.ShapeDtypeStruct((r * k, w), dt) for k, w, dt in outs] + [
            jax.ShapeDtypeStruct((r // vt_tile, b_kv, vt_tile), BF16)],
        compiler_params=pltpu.CompilerParams(dimension_semantics=("arbitrary",),
                                             vmem_limit_bytes=VMEM_LIMIT),
        name=f"inproj_{r // tm}",
    )(x2, g1, wm, ws, wg, qg, kg, ca, sa, cb, sb)


HGRN_BATCH = 4


def _hgrn_body(aq_ref, af_ref, ai_ref, ag_ref, s0_ref, lbp_ref, ng_ref, oa_ref, sn_ref, st_ref,
               *, layer):
    c = pl.program_id(1)
    nc = pl.num_programs(1)
    nb = aq_ref.shape[0]

    @pl.when(c == 0)
    def _():
        for b in range(nb):
            for h in range(A_HEADS):
                st_ref[b, h] = s0_ref[b, h].T

    lbp = lbp_ref[...]
    e = jnp.exp(lbp - jnp.max(lbp, axis=0, keepdims=True))
    sm = e / jnp.sum(e, axis=0, keepdims=True)
    lb = jnp.sum(sm[:layer + 1], axis=0, keepdims=True)

    n = aq_ref.shape[1]
    r_i = lax.broadcasted_iota(I32, (n, n), 0)
    c_i = lax.broadcasted_iota(I32, (n, n), 1)
    causal = r_i >= c_i
    tri = jnp.where(causal, 1.0, 0.0).astype(BF16)
    prep = []
    for b in range(nb):
        f = lb + (1.0 - lb) * jax.nn.sigmoid(af_ref[b])
        lf = jnp.log(f)
        hi = lf.astype(BF16)
        r1 = lf - hi.astype(F32)
        mid = r1.astype(BF16)
        lo = (r1 - mid.astype(F32)).astype(BF16)
        cum = _dot(tri, hi) + _dot(tri, mid) + _dot(tri, lo)
        last = cum[n - 1:n, :]
        qd = _silu(aq_ref[b].astype(F32)) * jnp.exp(cum)
        k = 1.0 - f
        prep.append((qd, k * jnp.exp(-cum), k * jnp.exp(last - cum), jnp.exp(last),
                     _silu(ag_ref[b].astype(F32))))
    for h in range(A_HEADS):
        sl = slice(h * A_DK, (h + 1) * A_DK)
        for b, (qd, kd, kt, el, gate) in enumerate(prep):
            qd_h = qd[:, sl].astype(BF16)
            v_h = ai_ref[b, :, sl]
            att = jnp.where(causal, _dot_nt(qd_h, kd[:, sl].astype(BF16)), 0.0)
            st = st_ref[b, h]
            o = _dot(att.astype(BF16), v_h) + _dot_nt(qd_h, st.astype(BF16))
            st_ref[b, h] = st * el[:, sl] + _dot_tn(v_h, kt[:, sl].astype(BF16))
            oa_ref[b, :, sl] = (_rms(o) * ng_ref[...] * gate[:, sl]).astype(BF16)

    @pl.when(c == nc - 1)
    def _():
        for b in range(nb):
            for h in range(A_HEADS):
                sn_ref[b, h] = st_ref[b, h].T


def _hgrn(aq, af, ai, ag, s0, lbp, ng, bsz, t, layer):
    d = aq.shape[1]
    nc = t // CHUNK
    nb = _row_tile(bsz, HGRN_BATCH)
    seq = lambda a: a.reshape(bsz, t, d)
    row = pl.BlockSpec((nb, CHUNK, d), lambda g, c: (g, c, 0))
    st = pl.BlockSpec((nb, A_HEADS, A_DK, A_DV), lambda g, c: (g, 0, 0, 0))
    oa, s_new = pl.pallas_call(
        functools.partial(_hgrn_body, layer=layer),
        grid=(bsz // nb, nc),
        in_specs=[row, row, row, row, st, _const_spec(lbp.shape), _const_spec(ng.shape)],
        out_specs=[row, st],
        out_shape=[jax.ShapeDtypeStruct((bsz, t, d), BF16),
                   jax.ShapeDtypeStruct((bsz, A_HEADS, A_DK, A_DV), F32)],
        scratch_shapes=[pltpu.VMEM((nb, A_HEADS, A_DV, A_DK), F32)],
        compiler_params=pltpu.CompilerParams(dimension_semantics=("arbitrary", "arbitrary"),
                                             vmem_limit_bytes=VMEM_LIMIT),
        name=f"hgrn_{bsz // nb}x{nc}",
    )(seq(aq), seq(af), seq(ai), seq(ag), s0, lbp, ng)
    return oa.reshape(bsz * t, d), s_new


def _chunk_of(pos):
    return jnp.right_shift(pos, CHUNK.bit_length() - 1)


def _transpose_wi(wi, tq):
    if tq % LANES:
        wi = jnp.concatenate([wi, jnp.zeros((LANES - tq % LANES, LANES), F32)], axis=0)
    return wi.T[:IDX_HEADS, :tq]


def _score_tile(ki2_tile, qim_ref, wt):
    acc = None
    for h in range(IDX_HEADS):
        s = _dot_nt(ki2_tile, qim_ref[:, h * LANES:(h + 1) * LANES])
        term = jnp.maximum(s, 0.0) * wt[h:h + 1, :]
        acc = term if acc is None else acc + term
    return acc


def _mask_scores(score, adm):
    score = jnp.where(score == 0.0, 0.0, score)
    return score if adm is None else jnp.where(adm, score, -jnp.inf)


def _sort_key(score):
    bits = lax.bitcast_convert_type(score, I32)
    return jnp.where(bits < 0, bits ^ 0x7FFFFFFF, bits)


def _key_to_score(key):
    return lax.bitcast_convert_type(jnp.where(key < 0, key ^ 0x7FFFFFFF, key), F32)


SUBLANES = 8
COUNT_ACCS = 4
KEY_BITS = 32
PLANE_KEYS = SUBLANES * KEY_BITS


def _store_bit_planes(planes_ref, key, blk0):
    u = key ^ INT_MIN
    tk = u.shape[0]
    for blk in range(tk // PLANE_KEYS):
        a = [u[blk * PLANE_KEYS + i * SUBLANES:blk * PLANE_KEYS + (i + 1) * SUBLANES, :]
             for i in range(KEY_BITS)]
        for j, m in ((16, 0x0000FFFF), (8, 0x00FF00FF), (4, 0x0F0F0F0F), (2, 0x33333333),
                     (1, 0x55555555)):
            k = 0
            while k < KEY_BITS:
                t = (a[k] ^ lax.shift_right_logical(a[k + j], j)) & m
                a[k] = a[k] ^ t
                a[k + j] = a[k + j] ^ lax.shift_left(t, j)
                k = (k + j + 1) & ~j
        row = pl.multiple_of((blk0 + blk) * SUBLANES, SUBLANES)
        for p in range(KEY_BITS):
            planes_ref[p, pl.ds(row, SUBLANES), :] = a[p]


def _select(sc_ref, planes_ref, bias_ref, cut_ref, stat_ref, nkt, krow, tk, tq, nbits):
    nrow = planes_ref.shape[1]
    nblk = nkt * (tk // PLANE_KEYS)
    blk_of_row = jnp.right_shift(lax.broadcasted_iota(I32, (nrow, tq), 0),
                                 SUBLANES.bit_length() - 1)

    def word_count(words):
        pc = lax.population_count(words).reshape(nrow // SUBLANES, SUBLANES, tq)
        return jnp.sum(jnp.sum(pc, axis=0).astype(F32), axis=0, keepdims=True)

    def bit_body(i, carry):
        alive, above, thr_u = carry
        ones = alive & planes_ref[i]
        c1 = word_count(ones)
        take = above + c1 >= krow
        alive = jnp.where(take, ones, alive ^ ones)
        above = jnp.where(take, above, above + c1)
        thr_u = jnp.where(take, thr_u | jnp.left_shift(jnp.int32(1), KEY_BITS - 1 - i), thr_u)
        return alive, above, thr_u

    alive0 = jnp.where(blk_of_row < nblk, -1, 0).astype(I32)
    _, _, thr_u = lax.fori_loop(
        0, KEY_BITS, bit_body, (alive0, jnp.zeros((1, tq), F32), jnp.zeros((1, tq), I32)))

    nchunk = tk // SUBLANES
    sub = lax.broadcasted_iota(I32, (SUBLANES, tq), 0)

    def count(pred_fn):
        def body(j, accs):
            t = sc_ref[j]
            accs = list(accs)
            for c in range(nchunk):
                a = c % COUNT_ACCS
                accs[a] = pred_fn(t[c * SUBLANES:(c + 1) * SUBLANES, :], j, c, accs[a])
            return tuple(accs)
        zero = jnp.zeros((SUBLANES, tq), F32)
        accs = lax.fori_loop(0, nkt, body, (zero,) * COUNT_ACCS)
        tot = accs[0]
        for a in accs[1:]:
            tot = tot + a
        return jnp.sum(tot, axis=0, keepdims=True)

    def count_ge(cand):
        c8 = jnp.broadcast_to(cand, (SUBLANES, tq))
        return count(lambda t, j, c, acc: jnp.where(t >= c8, acc + 1.0, acc))

    def count_gt(cand):
        c8 = jnp.broadcast_to(cand, (SUBLANES, tq))
        return count(lambda t, j, c, acc: jnp.where(t > c8, acc + 1.0, acc))

    def record(thr):
        stat_ref[0:1, :] = thr
        stat_ref[1:2, :] = count_ge(thr)
        stat_ref[2:3, :] = count_gt(thr)

    record(_key_to_score(thr_u ^ INT_MIN))
    good = (stat_ref[2:3, :] < krow) & (stat_ref[1:2, :] >= krow)

    @pl.when(jnp.max(jnp.where(good, 0.0, 1.0)) > 0.0)
    def _():
        def cmp_body(i, tu):
            cand_u = tu | jnp.left_shift(jnp.int32(1), KEY_BITS - 1 - i)
            return jnp.where(count_ge(_key_to_score(cand_u ^ INT_MIN)) >= krow, cand_u, tu)
        tu = lax.fori_loop(0, KEY_BITS, cmp_body, jnp.zeros((1, tq), I32))
        record(_key_to_score(tu ^ INT_MIN))

    thr = stat_ref[0:1, :]
    n_ge = stat_ref[1:2, :]
    need = krow - stat_ref[2:3, :]

    cut_ref[...] = jnp.full(cut_ref.shape, 2 ** 31 - 1, I32)

    @pl.when(jnp.max(jnp.where(n_ge > krow, 1.0, 0.0)) > 0.0)
    def _():
        thr8 = jnp.broadcast_to(thr, (SUBLANES, tq))

        def idx_body(i, cut):
            cand = cut | jnp.left_shift(jnp.int32(1), nbits - 1 - i)
            c8 = jnp.broadcast_to(cand, (SUBLANES, tq))
            below = count(lambda t, j, c, acc: jnp.where(
                t == thr8, jnp.where(j * tk + c * SUBLANES + sub < c8, acc + 1.0, acc), acc))
            return jnp.where(below < need, cand, cut)
        cut_ref[...] = lax.fori_loop(0, nbits, idx_body, jnp.zeros((1, tq), I32))

    cut = cut_ref[...]

    def write(j, carry):
        t = sc_ref[j]
        idx = j * tk + lax.broadcasted_iota(I32, (tk, tq), 0)
        tie = jnp.where(t == thr, jnp.where(idx <= cut, 0.0, NEG), NEG)
        bias = jnp.where(t > thr, 0.0, tie)
        if callable(bias_ref):
            bias_ref(j, bias)
        else:
            bias_ref[j] = bias
        return carry

    lax.fori_loop(0, nkt, write, 0)


def _stack_heads(q_ref, u, stack):
    heads = [q_ref[:, h * HEAD_DIM:(h + 1) * HEAD_DIM] for h in range(u * stack, (u + 1) * stack)]
    return heads[0] if stack == 1 else jnp.concatenate(heads, axis=0)


def _store_heads(o_ref, u, stack, o, tq):
    for r in range(stack):
        h = u * stack + r
        o_ref[:, h * HEAD_DIM:(h + 1) * HEAD_DIM] = o[r * tq:(r + 1) * tq].astype(o_ref.dtype)


Q_SCALE = HEAD_DIM ** -0.5 * 1.4426950408889634
SUM_ROWS = 16


def _attn_reset(m_ref, l_ref, acc_ref):
    m_ref[...] = jnp.full(m_ref.shape, NEG, F32)
    l_ref[...] = jnp.zeros(l_ref.shape, F32)
    acc_ref[...] = jnp.zeros(acc_ref.shape, F32)


def _attn_logits(q_ref, u, stack, kt, bias, m_ref, state=None):
    state = u if state is None else state
    qg = _stack_heads(q_ref, u, stack)
    if bias.shape[1] != qg.shape[0]:
        bias = jnp.concatenate([bias] * stack, axis=1)
    s = _dot_nt(kt, qg) + bias
    return s, jnp.maximum(m_ref[state], jnp.max(s, axis=0, keepdims=True))


def _attn_accumulate(u, s, m_new, m_ref, l_ref, acc_ref, pv):
    alpha = jnp.exp2(m_ref[u] - m_new)
    p = jnp.exp2(s - m_new)
    if l_ref is not None:
        l_ref[u] = alpha * l_ref[u] + jnp.sum(p, axis=0, keepdims=True)
    acc_ref[u] = alpha * acc_ref[u] + pv(p.astype(BF16))
    m_ref[u] = m_new


def _attn_step(q_ref, u, stack, kt, bias, m_ref, l_ref, acc_ref, pv, state=None):
    state = u if state is None else state
    s, m_new = _attn_logits(q_ref, u, stack, kt, bias, m_ref, state)
    _attn_accumulate(state, s, m_new, m_ref, l_ref, acc_ref, pv)


def _attn_finish(o_ref, u, stack, l_ref, acc_ref, tq, state=None):
    state = u if state is None else state
    acc = acc_ref[state]
    l = acc[HEAD_DIM:HEAD_DIM + 1] if l_ref is None else l_ref[state]
    _store_heads(o_ref, u, stack, (acc[:HEAD_DIM] / l).T, tq)


def _dsa_prompt_body(q_ref, qim_ref, wi_ref, kb_ref, vt_ref, ki2_ref, o_ref, sc_ref, planes_ref,
                     bias_ref, cut_ref, stat_ref, m_ref, l_ref, acc_ref,
                     *, tq, tk, topk, nbits):
    q0 = pl.program_id(1) * tq
    nkt = lax.div(q0 + tq + tk - 1, tk)
    wt = _transpose_wi(wi_ref[...], tq)
    qchunk = _chunk_of(q0 + lax.broadcasted_iota(I32, (1, tq), 1))

    @pl.when((pl.program_id(0) == 0) & (pl.program_id(1) == 0))
    def _():
        planes_ref[...] = jnp.zeros(planes_ref.shape, I32)

    def score_body(j, carry):
        ks = pl.multiple_of(j * tk, tk)
        score = _score_tile(ki2_ref[pl.ds(ks, tk), :], qim_ref, wt)
        kpos = ks + lax.broadcasted_iota(I32, (tk, tq), 0)
        score = _mask_scores(score, _chunk_of(kpos) <= qchunk)
        sc_ref[j] = score
        _store_bit_planes(planes_ref, _sort_key(score), j * (tk // PLANE_KEYS))
        return carry

    lax.fori_loop(0, nkt, score_body, 0)

    krow = jnp.minimum((qchunk + 1) * CHUNK, topk).astype(F32)
    _select(sc_ref, planes_ref, bias_ref, cut_ref, stat_ref, nkt, krow, tk, tq, nbits)

    _attn_reset(m_ref, l_ref, acc_ref)

    rep = B_HEADS // B_KV_HEADS
    ones = jnp.ones((SUM_ROWS, tk), BF16)

    def att_body(j, carry):
        ks = pl.multiple_of(j * tk, tk)
        bias = bias_ref[j]
        gsl = lambda g: slice(g * HEAD_DIM, (g + 1) * HEAD_DIM)
        logits = lambda g: _attn_logits(q_ref, g, rep, kb_ref[pl.ds(ks, tk), gsl(g)], bias, m_ref)
        pending = [logits(0), logits(1)]
        for g in range(B_KV_HEADS):
            if g + 2 < B_KV_HEADS:
                pending.append(logits(g + 2))
            vt = jnp.concatenate([vt_ref[j, gsl(g), :], ones], axis=0)
            _attn_accumulate(g, *pending[g], m_ref, None, acc_ref, lambda p, vt=vt: _dot(vt, p))
        return carry

    lax.fori_loop(0, nkt, att_body, 0)
    for g in range(B_KV_HEADS):
        _attn_finish(o_ref, g, rep, None, acc_ref, tq)


def _dsa_scratch(nt, tk, tq, stack, acc_rows):
    units = B_HEADS // stack
    return [pltpu.VMEM((nt, tk, tq), F32),
            pltpu.VMEM((KEY_BITS, nt * tk // KEY_BITS, tq), I32),
            pltpu.VMEM((nt, tk, tq), F32),
            pltpu.VMEM((1, tq), I32),
            pltpu.VMEM((SUBLANES, tq), F32),
            pltpu.VMEM((units, 1, stack * tq), F32),
            pltpu.VMEM((units, 1, stack * tq), F32),
            pltpu.VMEM((units, acc_rows, stack * tq), F32)]


def _dsa_prompt(q, qim, wi, kb, vt, ki2, bsz, t, tq, tk):
    nq = t // tq
    nt = t // tk
    topk = min(IDX_TOPK_MAX, t // 4)
    nbits = max(1, (t - 1).bit_length())
    qrow = lambda w: pl.BlockSpec((tq, w), lambda b, i: (b * nq + i, 0))
    seq = lambda w: pl.BlockSpec((t, w), lambda b, i: (b, 0))
    return pl.pallas_call(
        functools.partial(_dsa_prompt_body, tq=tq, tk=tk, topk=topk, nbits=nbits),
        grid=(bsz, nq),
        in_specs=[qrow(q.shape[1]), qrow(qim.shape[1]), qrow(LANES), seq(kb.shape[1]),
                  pl.BlockSpec((nt, vt.shape[1], tk), lambda b, i: (b, 0, 0)), seq(LANES)],
        out_specs=qrow(q.shape[1]),
        out_shape=jax.ShapeDtypeStruct(q.shape, BF16),
        scratch_shapes=_dsa_scratch(nt, tk, tq, B_HEADS // B_KV_HEADS, HEAD_DIM + SUM_ROWS),
        compiler_params=pltpu.CompilerParams(dimension_semantics=("arbitrary", "arbitrary"),
                                             vmem_limit_bytes=VMEM_LIMIT),
        name="dsa_prompt",
    )(q, qim, wi, kb, vt, ki2)


def _dsa_sample_body(q_ref, qim_ref, wi_ref, kn_ref, vn_ref, ki2n_ref, kidx2_ref, *refs,
                     t, tk, nsub, npt, past, topk, nbits, nbat):
    caches, refs = refs[:2 * nbat], refs[2 * nbat:]
    o_ref, sc_ref, planes_ref, bias_ref, cut_ref, stat_ref, m_ref, l_ref, acc_ref = refs
    j = pl.program_id(1)
    nj = pl.num_programs(1)
    tq = nbat * t
    rep = B_HEADS // B_KV_HEADS
    pad = jnp.zeros((tk - t, LANES), BF16)
    nb = tk // PLANE_KEYS
    rows_of = lambda b: pl.ds(b * t, t)

    @pl.when(j == 0)
    def _():
        wt = _transpose_wi(wi_ref[...], tq)
        row = lax.broadcasted_iota(I32, (tq, 1), 0)
        qims = [jnp.where((row >= b * t) & (row < (b + 1) * t), qim_ref[...].astype(F32),
                          0.0).astype(BF16) for b in range(nbat)]

        def scores(tile_of):
            total = None
            for b in range(nbat):
                s = _score_tile(tile_of(b), qims[b], wt)
                total = s if total is None else total + s
            return total

        def score_body(jt, carry):
            ks = pl.multiple_of(jt * tk, tk)
            score = _mask_scores(scores(lambda b: kidx2_ref[pl.ds(b * past + ks, tk), :]), None)
            sc_ref[jt] = score
            _store_bit_planes(planes_ref, _sort_key(score), jt * nb)
            return carry

        lax.fori_loop(0, npt, score_body, 0)
        krow_i = lax.broadcasted_iota(I32, (tk, tq), 0)
        qchunk = _chunk_of(past + (lax.broadcasted_iota(I32, (1, tq), 1) & (t - 1)))
        adm = jnp.where(krow_i < t, _chunk_of(past + krow_i), 2 ** 30) <= qchunk
        score = _mask_scores(
            scores(lambda b: jnp.concatenate([ki2n_ref[rows_of(b), :], pad], axis=0)), adm)
        sc_ref[npt] = score
        _store_bit_planes(planes_ref, _sort_key(score), npt * nb)
        krow = jnp.minimum((qchunk + 1) * CHUNK, topk).astype(F32)

        def write_bias(jt, bias):
            for b in range(nbat):
                bias_ref[b, jt] = jnp.concatenate([bias[:, b * t:(b + 1) * t]] * rep, axis=1)

        _select(sc_ref, planes_ref, write_bias, cut_ref, stat_ref, npt + 1, krow, tk, tq, nbits)
        _attn_reset(m_ref, l_ref, acc_ref)

    units = [(b, g) for b in range(nbat) for g in range(B_KV_HEADS)]
    state = lambda b, g: b * B_KV_HEADS + g

    for su in range(nsub):
        rows = lambda g: pl.ds(su * tk * B_KV_HEADS + g, tk, stride=B_KV_HEADS)
        pending = [_attn_logits(q_ref.at[rows_of(b)], g, rep,
                                caches[2 * b][rows(g), :].astype(BF16),
                                bias_ref[b, j * nsub + su], m_ref, state(b, g))
                   for b, g in units]
        for (b, g), (s, m_new) in zip(units, pending):
            vt = caches[2 * b + 1][rows(g), :].astype(BF16)
            _attn_accumulate(state(b, g), s, m_new, m_ref, l_ref, acc_ref,
                             lambda p, vt=vt: _dot_tn(vt, p))

    @pl.when(j == nj - 1)
    def _():
        for b, g in units:
            gs = slice(g * HEAD_DIM, (g + 1) * HEAD_DIM)
            kt = jnp.concatenate([kn_ref[rows_of(b), gs], pad], axis=0)
            vt = jnp.concatenate([vn_ref[rows_of(b), gs], pad], axis=0)
            _attn_step(q_ref.at[rows_of(b)], g, rep, kt, bias_ref[b, npt], m_ref, l_ref, acc_ref,
                       lambda p, vt=vt: _dot_tn(vt, p), state(b, g))
            _attn_finish(o_ref.at[rows_of(b)], g, rep, l_ref, acc_ref, t, state(b, g))


def _dsa_sample(q, qim, wi, kn, vn, ki2n, kidx2, ck, cv, bsz, t, past, tk, nsub, cache_off):
    fits = t < LANES and LANES % t == 0 and t & (t - 1) == 0 and bsz % (LANES // t) == 0
    nbat = LANES // t if fits else 1
    tq = nbat * t
    npt = past // tk
    nj = npt // nsub
    nt = npt + 1
    topk = min(IDX_TOPK_MAX, (past + t) // 4)
    nbits = max(1, (past + t - 1).bit_length())
    rep = B_HEADS // B_KV_HEADS
    units = nbat * B_KV_HEADS
    qrow = lambda w: pl.BlockSpec((tq, w), lambda g, j: (g, 0))
    kvw = kn.shape[1]
    cache = lambda b: pl.BlockSpec((nsub * tk * B_KV_HEADS, HEAD_DIM),
                                   lambda g, j: ((cache_off + g * nbat + b) * nj + j, 0))
    caches = [spec for b in range(nbat) for spec in (cache(b), cache(b))]
    scratch = [pltpu.VMEM((nt, tk, tq), F32),
               pltpu.VMEM((KEY_BITS, nt * tk // KEY_BITS, tq), I32),
               pltpu.VMEM((nbat, nt, tk, rep * t), F32),
               pltpu.VMEM((1, tq), I32),
               pltpu.VMEM((SUBLANES, tq), F32),
               pltpu.VMEM((units, 1, rep * t), F32),
               pltpu.VMEM((units, 1, rep * t), F32),
               pltpu.VMEM((units, HEAD_DIM, rep * t), F32)]
    return pl.pallas_call(
        functools.partial(_dsa_sample_body, t=t, tk=tk, nsub=nsub, npt=npt, past=past,
                          topk=topk, nbits=nbits, nbat=nbat),
        grid=(bsz // nbat, nj),
        in_specs=[qrow(q.shape[1]), qrow(qim.shape[1]), qrow(LANES), qrow(kvw), qrow(kvw),
                  qrow(LANES), pl.BlockSpec((nbat * past, LANES), lambda g, j: (g, 0))] + caches,
        out_specs=qrow(q.shape[1]),
        out_shape=jax.ShapeDtypeStruct(q.shape, BF16),
        scratch_shapes=scratch,
        compiler_params=pltpu.CompilerParams(dimension_semantics=("arbitrary", "arbitrary"),
                                             vmem_limit_bytes=VMEM_LIMIT),
        name="dsa_sample",
    )(q, qim, wi, kn, vn, ki2n, kidx2, *([ck, cv] * nbat))


def _post_body(oa_ref, ob_ref, ga_ref, gb_ref, x_ref, wa_ref, wb_ref, wo_ref, g2_ref, wr_ref,
               br_ref, x1_o, t_o, eidx_o, prob_o, cnt_o):
    ya = _dot(oa_ref[...], wa_ref[...])
    yb = _dot(ob_ref[...], wb_ref[...])
    merged = (jax.nn.sigmoid(ga_ref[...].astype(F32)) * ya
              + jax.nn.sigmoid(gb_ref[...].astype(F32)) * yb)
    x1 = x_ref[...] + _dot(merged.astype(BF16), wo_ref[...])
    x1_o[...] = x1
    tok = _rms(x1) * g2_ref[...]
    t_o[...] = tok
    logits = _dot(tok.astype(BF16), wr_ref[...]) + br_ref[...]
    tm = logits.shape[0]
    lane = lax.broadcasted_iota(I32, (tm, LANES), 1).astype(F32)
    cur = jnp.where(lane < N_EXPERTS, logits, -jnp.inf)
    top = None
    den = jnp.zeros((tm, 1), F32)
    eidx = jnp.zeros((tm, LANES), F32)
    prob = jnp.zeros((tm, LANES), F32)
    chosen = jnp.zeros((tm, LANES), F32)
    for k in range(TOP_K):
        mx = jnp.max(cur, axis=1, keepdims=True)
        first = jnp.min(jnp.where(cur == mx, lane, float(LANES)), axis=1, keepdims=True)
        if top is None:
            top = mx
        e = jnp.exp(mx - top)
        den = den + e
        eidx = jnp.where(lane == k, first, eidx)
        prob = jnp.where(lane == k, e, prob)
        hit = lane == first
        chosen = jnp.where(hit, 1.0, chosen)
        cur = jnp.where(hit, -jnp.inf, cur)
    eidx_o[...] = eidx.astype(I32)
    prob_o[...] = prob / den

    @pl.when(pl.program_id(0) == 0)
    def _():
        cnt_o[...] = jnp.zeros(cnt_o.shape, F32)

    cnt_o[...] += jnp.sum(chosen, axis=0, keepdims=True)


def _post(oa, ob, ga, gb, x2, wa, wb, wo, g2, wr, br, tm):
    r, d = x2.shape
    row = lambda w: pl.BlockSpec((tm, w), lambda i: (i, 0))
    return pl.pallas_call(
        _post_body,
        grid=(r // tm,),
        in_specs=[row(d), row(d), row(d), row(d), row(d), _const_spec(wa.shape),
                  _const_spec(wb.shape), _const_spec(wo.shape), _const_spec(g2.shape),
                  _const_spec(wr.shape), _const_spec(br.shape)],
        out_specs=[row(d), row(d), row(LANES), row(LANES),
                   pl.BlockSpec((1, LANES), lambda i: (0, 0))],
        out_shape=[jax.ShapeDtypeStruct((r, d), F32), jax.ShapeDtypeStruct((r, d), F32),
                   jax.ShapeDtypeStruct((r, LANES), I32), jax.ShapeDtypeStruct((r, LANES), F32),
                   jax.ShapeDtypeStruct((1, LANES), F32)],
        compiler_params=pltpu.CompilerParams(dimension_semantics=("arbitrary",),
                                             vmem_limit_bytes=VMEM_LIMIT),
        name=f"post_{r // tm}",
    )(oa, ob, ga, gb, x2, wa, wb, wo, g2, wr, br)


MOE_TILE = 512
DMA_PRIORITIES = 2
ISSUE_UNROLL = 8


WIN_ROWS = 64
FLAG_LANE = LANES - 1


def _rank_body(eidx_ref, cin_ref, pos_o, col_o, win_o, carry_ref, *, max_start):
    @pl.when(pl.program_id(0) == 0)
    def _():
        carry_ref[...] = cin_ref[...]

    eidx = eidx_ref[...]
    tm = eidx.shape[0]
    lane = lax.broadcasted_iota(I32, (tm, LANES), 1)
    hits = [lane == eidx[:, k:k + 1] for k in range(TOP_K)]
    onehot = jnp.zeros((tm, LANES), F32)
    for hit in hits:
        onehot = onehot + jnp.where(hit, 1.0, 0.0)
    r_i = lax.broadcasted_iota(I32, (tm, tm), 0)
    c_i = lax.broadcasted_iota(I32, (tm, tm), 1)
    before = jnp.where(c_i < r_i, 1.0, 0.0).astype(BF16)
    carry = carry_ref[...]
    start = jnp.minimum((carry.astype(I32) >> 3) << 3, max_start)
    base = carry + _dot(before, onehot.astype(BF16))
    rank = jnp.zeros((tm, LANES), F32)
    col = jnp.zeros((tm, LANES), F32)
    worst = jnp.zeros((tm, 1), F32)
    for k, hit in enumerate(hits):
        rk = jnp.sum(jnp.where(hit, base, 0.0), axis=1, keepdims=True)
        off = jnp.sum(jnp.where(hit, base - start.astype(F32), 0.0), axis=1, keepdims=True)
        rank = jnp.where(lane == k, rk, rank)
        col = jnp.where(lane == k, eidx[:, k:k + 1].astype(F32) * WIN_ROWS + off, col)
        worst = jnp.maximum(worst, off)
    pos_o[...] = rank.T[:SUBLANES, :].astype(I32)
    col_o[...] = col.astype(I32)
    flag = (jnp.max(worst) >= WIN_ROWS).astype(I32)
    win_o[...] = jnp.where(lane[:1] == FLAG_LANE, flag, start).reshape(win_o.shape)
    carry_ref[...] += jnp.sum(onehot, axis=0, keepdims=True)


def _rank(eidx, cin, tm, max_start):
    r = eidx.shape[0]
    return pl.pallas_call(
        functools.partial(_rank_body, max_start=max_start),
        grid=(r // tm,),
        in_specs=[pl.BlockSpec((tm, LANES), lambda i: (i, 0)), _const_spec(cin.shape)],
        out_specs=[pl.BlockSpec((SUBLANES, tm), lambda i: (0, i)),
                   pl.BlockSpec((tm, LANES), lambda i: (i, 0)),
                   pl.BlockSpec((1, 1, LANES), lambda i: (i, 0, 0))],
        out_shape=[jax.ShapeDtypeStruct((SUBLANES, r), I32),
                   jax.ShapeDtypeStruct((r, LANES), I32),
                   jax.ShapeDtypeStruct((r // tm, 1, LANES), I32)],
        scratch_shapes=[pltpu.VMEM((1, LANES), F32)],
        compiler_params=pltpu.CompilerParams(dimension_semantics=("arbitrary",)),
        name=f"moe_rank_{r // tm}",
    )(eidx, cin)


def _wait_rows(ref, sem, times):
    for _ in range(times):
        pltpu.make_async_copy(ref, ref, sem).wait()


def _zero_pad_rows(pad_ref, xs_out, zeros, sem):
    zeros[...] = jnp.zeros(zeros.shape, zeros.dtype)
    row = zeros.at[pl.ds(0, 1)]
    rows8 = zeros.at[pl.ds(0, SUBLANES)]
    tile = zeros.shape[0]

    def for_each_copy(fn):
        def per_group(g, carry):
            first, n = pad_ref[0, g], pad_ref[1, g]
            head = jnp.minimum(n, (SUBLANES - (first & (SUBLANES - 1))) & (SUBLANES - 1))

            def per_row(t, c):
                fn(pltpu.make_async_copy(row, xs_out.at[pl.ds(first + t, 1)], sem))
                return c
            lax.fori_loop(0, head, per_row, 0)

            def per_block(t, c):
                at = pl.multiple_of(first + head + t * SUBLANES, SUBLANES)
                fn(pltpu.make_async_copy(rows8, xs_out.at[pl.ds(at, SUBLANES)], sem))
                return c
            lax.fori_loop(0, lax.div(n - head, SUBLANES), per_block, 0)
            return carry
        lax.fori_loop(0, N_EXPERTS, per_group, 0)
        first = pad_ref[0, N_EXPERTS]

        def per_tile(t, c):
            fn(pltpu.make_async_copy(
                zeros, xs_out.at[pl.ds(pl.multiple_of(first + t * tile, tile), tile)], sem))
            return c
        lax.fori_loop(0, lax.div(pad_ref[1, N_EXPERTS], tile), per_tile, 0)

    for_each_copy(lambda copy: copy.start())
    for_each_copy(lambda copy: copy.wait())


def _dispatch_body(pad_ref, *refs, bounds):
    ns = len(bounds) - 1
    pos_refs, tok_refs = refs[:ns], refs[ns:2 * ns]
    xs_out, ring, sems, zeros, zsem = refs[2 * ns:]
    i = pl.program_id(0)
    tm = ring.shape[1]
    last = pl.num_programs(0) - 1

    pl.when(i == 0)(functools.partial(_zero_pad_rows, pad_ref, xs_out, zeros, zsem))

    for slot in range(2):
        mine = i % 2 == slot
        for s in range(ns):
            @pl.when(mine & (i >= bounds[s]) & (i < bounds[s + 1]))
            def _(slot=slot, s=s):
                ring[slot] = tok_refs[s][...]

                def issue(t, carry):
                    for k in range(TOP_K):
                        pltpu.make_async_copy(ring.at[slot, pl.ds(t, 1)],
                                              xs_out.at[pl.ds(pos_refs[s][t * TOP_K + k], 1)],
                                              sems.at[slot]).start(priority=k % DMA_PRIORITIES)
                    return carry

                lax.fori_loop(0, tm, issue, 0)

        @pl.when(mine & (i > 0))
        def _(slot=slot):
            _wait_rows(ring.at[1 - slot], sems.at[1 - slot], TOP_K)

        @pl.when(mine & (i == last))
        def _(slot=slot):
            _wait_rows(ring.at[slot], sems.at[slot], TOP_K)


def _dispatch(poss, pad, toks, rows, tm):
    d = toks[0].shape[1]
    bounds = [0]
    for tok in toks:
        bounds.append(bounds[-1] + tok.shape[0] // tm)

    def local(s):
        lo, n = bounds[s], bounds[s + 1] - bounds[s]
        return lambda i: jnp.clip(i - lo, 0, n - 1)

    in_specs = [pl.BlockSpec(memory_space=pltpu.SMEM)]
    in_specs += [pl.BlockSpec((tm * TOP_K,), lambda i, f=local(s): (f(i),),
                              memory_space=pltpu.SMEM) for s in range(len(toks))]
    in_specs += [pl.BlockSpec((tm, d), lambda i, f=local(s): (f(i), 0)) for s in range(len(toks))]
    return pl.pallas_call(
        functools.partial(_dispatch_body, bounds=tuple(bounds)),
        grid=(bounds[-1],),
        in_specs=in_specs,
        out_specs=pl.BlockSpec(memory_space=pl.ANY),
        out_shape=jax.ShapeDtypeStruct((rows, d), toks[0].dtype),
        scratch_shapes=[pltpu.VMEM((2, tm, d), toks[0].dtype), pltpu.SemaphoreType.DMA((2,)),
                        pltpu.VMEM((MOE_TILE, d), toks[0].dtype), pltpu.SemaphoreType.DMA],
        compiler_params=pltpu.CompilerParams(dimension_semantics=("arbitrary",),
                                             has_side_effects=True),
        name="moe_dispatch",
    )(pad, *poss, *toks)


def _experts_body(te_ref, na_ref, x_ref, wgu_ref, bgu_ref, wd_ref, bd_ref, y_ref, wgu_s, wd_s):
    r = pl.program_id(0)
    e = te_ref[r]
    prev = te_ref[jnp.maximum(r - 1, 0)]

    @pl.when((r == 0) | (e != prev))
    def _():
        rows = 128
        for c in range(wgu_s.shape[0] // rows):
            wgu_s[c * rows:(c + 1) * rows, :] = wgu_ref[0, c * rows:(c + 1) * rows, :].astype(BF16)
        for c in range(wd_s.shape[0] // rows):
            wd_s[c * rows:(c + 1) * rows, :] = wd_ref[0, c * rows:(c + 1) * rows, :].astype(BF16)

    @pl.when(r < na_ref[0])
    def _():
        gu = _dot(x_ref[...].astype(BF16), wgu_s[...]) + bgu_ref[0]
        dff = gu.shape[1] // 2
        gate = jnp.minimum(gu[:, :dff], SWIGLU_LIMIT)
        up = jnp.clip(gu[:, dff:], -SWIGLU_LIMIT, SWIGLU_LIMIT)
        act = (up + 1.0) * gate * jax.nn.sigmoid(SWIGLU_ALPHA * gate)
        y_ref[...] = _dot(act.astype(BF16), wd_s[...]) + bd_ref[0]

    @pl.when(r >= na_ref[0])
    def _():
        y_ref[...] = jnp.zeros(y_ref.shape, F32)


def _experts(tile_expert, n_active, xs, wgu, bgu, wd, bd):
    p, d = xs.shape
    _, _, dff2 = wgu.shape
    tm = MOE_TILE
    grid_spec = pltpu.PrefetchScalarGridSpec(
        num_scalar_prefetch=2,
        grid=(p // tm,),
        in_specs=[pl.BlockSpec((tm, d), lambda r, te, na: (r, 0)),
                  pl.BlockSpec((1, d, dff2), lambda r, te, na: (te[r], 0, 0)),
                  pl.BlockSpec((1, 1, dff2), lambda r, te, na: (te[r], 0, 0)),
                  pl.BlockSpec((1, dff2 // 2, d), lambda r, te, na: (te[r], 0, 0)),
                  pl.BlockSpec((1, 1, d), lambda r, te, na: (te[r], 0, 0))],
        out_specs=pl.BlockSpec((tm, d), lambda r, te, na: (r, 0)),
        scratch_shapes=[pltpu.VMEM((d, dff2), BF16), pltpu.VMEM((dff2 // 2, d), BF16)],
    )
    return pl.pallas_call(
        _experts_body,
        grid_spec=grid_spec,
        out_shape=jax.ShapeDtypeStruct((p, d), F32),
        compiler_params=pltpu.CompilerParams(dimension_semantics=("arbitrary",),
                                             vmem_limit_bytes=VMEM_LIMIT),
        name="moe_experts",
    )(tile_expert, n_active, xs, wgu, bgu, wd, bd)


def _combine_body(slow_ref, win_ref, pos_ref, col_ref, prob_ref, x1_ref, y_hbm, out_ref,
                  wbuf, rbuf, sems):
    i = pl.program_id(0)
    n = pl.num_programs(0) - 1
    tm = x1_ref.shape[0]
    slow = slow_ref[0] != 0
    nwin = wbuf.shape[1] // WIN_ROWS

    for slot in range(2):
        mine = i % 2 == slot

        @pl.when(mine & (i < n) & jnp.logical_not(slow))
        def _(slot=slot):
            for e in range(nwin):
                first = pl.multiple_of(win_ref[0, 0, e], SUBLANES)
                pltpu.make_async_copy(y_hbm.at[pl.ds(first, WIN_ROWS)],
                                      wbuf.at[slot, pl.ds(e * WIN_ROWS, WIN_ROWS)],
                                      sems.at[slot]).start()

        @pl.when(mine & (i > 0) & jnp.logical_not(slow))
        def _(slot=slot):
            _wait_rows(wbuf.at[1 - slot], sems.at[1 - slot], 1)
            col = col_ref[...]
            prob = prob_ref[...]
            lane = lax.broadcasted_iota(I32, (tm, LANES), 1)
            rel = [jnp.broadcast_to(col[:, k:k + 1], (tm, LANES)) - lane for k in range(TOP_K)]
            wgt = [jnp.broadcast_to(prob[:, k:k + 1], (tm, LANES)) for k in range(TOP_K)]
            pick = []
            for c in range(wbuf.shape[1] // LANES):
                g = jnp.zeros((tm, LANES), F32)
                for k in range(TOP_K):
                    g = jnp.where(rel[k] == c * LANES, wgt[k], g)
                pick.append(g.astype(BF16))
            ffn = _dot(jnp.concatenate(pick, axis=1), wbuf[1 - slot].astype(BF16))
            out_ref[...] = x1_ref[...] + ffn

        @pl.when(mine & (i < n) & slow)
        def _(slot=slot):
            def issue(t, carry):
                for k in range(TOP_K):
                    pltpu.make_async_copy(y_hbm.at[pl.ds(pos_ref[k, t], 1)],
                                          rbuf.at[slot, k, pl.ds(t, 1)],
                                          sems.at[slot]).start(priority=k % DMA_PRIORITIES)
                return carry
            lax.fori_loop(0, tm, issue, 0)

        @pl.when(mine & (i > 0) & slow)
        def _(slot=slot):
            _wait_rows(rbuf.at[1 - slot, 0], sems.at[1 - slot], TOP_K)
            prob = prob_ref[...]
            ffn = prob[:, 0:1] * rbuf[1 - slot, 0]
            for k in range(1, TOP_K):
                ffn = ffn + prob[:, k:k + 1] * rbuf[1 - slot, k]
            out_ref[...] = x1_ref[...] + ffn


def _combine(slow, win, pos, col, prob, x1, ys, tm):
    r, d = x1.shape
    n = r // tm
    prev = lambda w: pl.BlockSpec((tm, w), lambda i: (jnp.maximum(i - 1, 0), 0))
    cur = lambda i: jnp.minimum(i, n - 1)
    return pl.pallas_call(
        _combine_body,
        grid=(n + 1,),
        in_specs=[pl.BlockSpec(memory_space=pltpu.SMEM),
                  pl.BlockSpec((1, 1, LANES), lambda i: (cur(i), 0, 0), memory_space=pltpu.SMEM),
                  pl.BlockSpec((SUBLANES, tm), lambda i: (0, cur(i)), memory_space=pltpu.SMEM),
                  prev(LANES), prev(LANES), prev(d), pl.BlockSpec(memory_space=pl.ANY)],
        out_specs=prev(d),
        out_shape=jax.ShapeDtypeStruct((r, d), F32),
        scratch_shapes=[pltpu.VMEM((2, N_EXPERTS * WIN_ROWS, d), F32),
                        pltpu.VMEM((2, TOP_K, tm, d), F32), pltpu.SemaphoreType.DMA((2,))],
        compiler_params=pltpu.CompilerParams(dimension_semantics=("arbitrary",),
                                             vmem_limit_bytes=VMEM_LIMIT),
        name=f"moe_combine_{n}",
    )(slow, win, pos, col, prob, x1, ys)


def _moe(streams, wgu, bgu, wd, bd):
    tm = MOE_TILE
    counts = [s[4][0, :N_EXPERTS].astype(I32) for s in streams]
    cnt = sum(counts)
    padded = ((cnt + tm - 1) // tm) * tm
    ends = jnp.cumsum(padded)
    n_pairs = sum(s[0].shape[0] for s in streams) * TOP_K
    n_tiles = n_pairs // tm + N_EXPERTS
    tile_start = jnp.arange(n_tiles, dtype=I32) * tm
    tile_expert = jnp.minimum(jnp.sum((ends[None, :] <= tile_start[:, None]).astype(I32), axis=1),
                              N_EXPERTS - 1)
    n_active = (ends[-1:] // tm).astype(I32)
    start = ends - padded
    pad = jnp.stack([jnp.append(start + cnt, ends[-1]),
                     jnp.append(padded - cnt, n_tiles * tm - ends[-1])])
    routes = []
    for (x1, tok, eidx, prob, _), c in zip(streams, counts):
        cin = jnp.pad(start, (0, LANES - N_EXPERTS)).astype(F32)[None]
        routes.append(_rank(eidx, cin, _row_tile(eidx.shape[0], 256), n_tiles * tm - WIN_ROWS))
        start = start + c
    flat = [r[0][:TOP_K].T.reshape(-1) for r in routes]
    xs = _dispatch(flat, pad, [s[1] for s in streams], n_tiles * tm,
                   min(_row_tile(s[1].shape[0], 256) for s in streams))
    ys = _experts(tile_expert, n_active, xs, wgu, bgu, wd, bd)
    outs = []
    for s, (pos, col, win) in zip(streams, routes):
        slow = jnp.max(win[:, 0, FLAG_LANE]).reshape(1)
        outs.append(_combine(slow, win, pos, col, s[3], s[0], ys, _row_tile(s[0].shape[0], 256)))
    return outs


def _row_tile(r, want):
    tm = min(r, want)
    assert r % tm == 0, (r, tm)
    return tm


def _mixers(x2, bsz, t, pos, s0, caches, wts, layer):
    (g1, wm, ws, wg, qg, kg, lbp, ng) = wts
    r = x2.shape[0]
    tm = _row_tile(r, 256)
    assert t % tm == 0 or tm % t == 0
    tabs_a = _rope_tables(pos, HEAD_DIM, 1)
    tabs_b = _rope_tables(pos, IDX_DIM, LANES // IDX_DIM)
    tabs = tabs_a + tabs_b
    if tm > t:
        tabs = tuple(jnp.tile(tb, (tm // t, 1)) for tb in tabs)
    tk = max(tm, _row_tile(t, 512)) if caches is None else tm
    (aq, af, ai, ag, q, kf, kb, vf, vb, qim, kif, ki2, wi, ga, gb, vt) = _inproj(
        x2, g1, wm, ws, wg, qg, kg, tabs, tm, tk)
    oa, s_new = _hgrn(aq, af, ai, ag, s0, lbp, ng, bsz, t, layer)
    if caches is None:
        ob = _dsa_prompt(q, qim, wi, kb, vt, ki2, bsz, t, _row_tile(t, 256), tk)
    else:
        ck, cv, kidx2, past, cache_off = caches
        tk = _row_tile(past, 512)
        nsub = 2 if (past // tk) % 2 == 0 else 1
        ob = _dsa_sample(q, qim, wi, kb, vb, ki2, kidx2, ck, cv, bsz, t, past, tk, nsub, cache_off)
    return oa, ob, ga, gb, kf, vf, kif, s_new


def kernel(x_prompt, x_sample, cache_k, cache_v, cache_kidx, state_hgrn, norm1_g, w_in, lower_bounds, hgrn_norm_g, q_norm_g, k_norm_g, w_branch_a, w_branch_b, w_out, norm2_g, w_router, b_router, w_gate_up, b_gate_up, w_down, b_down):
    bp, tp, d = x_prompt.shape
    bs, ts, _ = x_sample.shape
    depth = w_in.shape[0]
    past = cache_k.shape[2]
    kvw = B_KV_HEADS * HEAD_DIM
    pos_p = jnp.arange(tp, dtype=I32)
    pos_s = past + jnp.arange(ts, dtype=I32)
    xp = x_prompt.reshape(bp * tp, d)
    xs = x_sample.reshape(bs * ts, d)
    n_main = 2 * A_HEADS * A_DK + 2 * A_HEADS * A_DV + B_HEADS * HEAD_DIM + 2 * kvw + IDX_HEADS * IDX_DIM
    n_small = n_main + IDX_DIM + IDX_HEADS
    outs = [[] for _ in range(8)]
    for l in range(depth):
        w = w_in[l]
        wm = w[:, :n_main].astype(BF16)
        w_ik = w[:, n_main:n_main + IDX_DIM]
        w_iw = w[:, n_main + IDX_DIM:n_small]
        ws = jnp.concatenate(
            [w_ik, w_ik, w_iw, jnp.zeros((d, LANES - IDX_HEADS), w.dtype)], axis=1).astype(BF16)
        wg = w[:, n_small:].astype(BF16)
        wts = (norm1_g[l][None], wm, ws, wg, q_norm_g[l][None], k_norm_g[l][None],
               lower_bounds, hgrn_norm_g[l][None])
        s0_p = jnp.zeros((bp, A_HEADS, A_DK, A_DV), F32)
        oa_p, ob_p, ga_p, gb_p, kp, vp, kip, sp = _mixers(xp, bp, tp, pos_p, s0_p, None, wts, l)
        kidx2 = jnp.concatenate([cache_kidx[l], cache_kidx[l]], axis=-1).astype(BF16)
        caches = (cache_k.reshape(-1, HEAD_DIM), cache_v.reshape(-1, HEAD_DIM),
                  kidx2.reshape(bs * past, LANES), past, l * bs)
        oa_s, ob_s, ga_s, gb_s, ks, vs, kis, ss = _mixers(
            xs, bs, ts, pos_s, state_hgrn[l], caches, wts, l)

        wa = w_branch_a[l].astype(BF16)
        wb = w_branch_b[l].astype(BF16)
        wo = w_out[l].astype(BF16)
        wr = jnp.pad(w_router[l], ((0, 0), (0, LANES - N_EXPERTS))).astype(BF16)
        br = jnp.pad(b_router[l], (0, LANES - N_EXPERTS))[None]
        g2 = norm2_g[l][None]
        bgu = b_gate_up[l][:, None, :]
        bd = b_down[l][:, None, :]
        streams = []
        for (x2, oa, ob, ga, gb) in ((xp, oa_p, ob_p, ga_p, gb_p), (xs, oa_s, ob_s, ga_s, gb_s)):
            r = x2.shape[0]
            streams.append(_post(oa, ob, ga, gb, x2, wa, wb, wo, g2, wr, br, _row_tile(r, 512)))
        xp, xs = _moe(streams, w_gate_up[l], bgu, w_down[l], bd)
        for lst, v in zip(outs, (kp.reshape(bp, tp, B_KV_HEADS, HEAD_DIM),
                                 vp.reshape(bp, tp, B_KV_HEADS, HEAD_DIM),
                                 kip.reshape(bp, tp, IDX_DIM), sp,
                                 ks.reshape(bs, ts, B_KV_HEADS, HEAD_DIM),
                                 vs.reshape(bs, ts, B_KV_HEADS, HEAD_DIM),
                                 kis.reshape(bs, ts, IDX_DIM), ss)):
            lst.append(v)
    return (xp.reshape(bp, tp, d), xs.reshape(bs, ts, d)) + tuple(jnp.stack(o) for o in outs)
```

```python
import functools

import jax
import jax.numpy as jnp
from jax import lax
from jax.experimental import pallas as pl
from jax.experimental.pallas import tpu as pltpu

F32 = jnp.float32
BF16 = jnp.bfloat16
I32 = jnp.int32

CHUNK = 64
A_HEADS = 8
A_DK = 128
A_DV = 128
B_HEADS = 8
B_KV_HEADS = 4
HEAD_DIM = 128
IDX_HEADS = 8
IDX_DIM = 64
IDX_TOPK_MAX = 256
ROPE_THETA = 10000.0
N_EXPERTS = 32
TOP_K = 4
SWIGLU_LIMIT = 7.0
SWIGLU_ALPHA = 1.702
EPS = 1e-6

LANES = 128
INT_MIN = -(2 ** 31)
NEG = -1e30
VMEM_LIMIT = 56 * 1024 * 1024


def _rms(x):
    return x * lax.rsqrt(jnp.mean(x * x, axis=-1, keepdims=True) + EPS)


def _silu(x):
    return x * jax.nn.sigmoid(x)


def _dot(a, b):
    return jnp.dot(a, b, preferred_element_type=F32)


def _dot_nt(a, b):
    return lax.dot_general(a, b, (((1,), (1,)), ((), ())), preferred_element_type=F32)


def _dot_tn(a, b):
    return lax.dot_general(a, b, (((0,), (0,)), ((), ())), preferred_element_type=F32)


def _const_spec(shape):
    zeros = (0,) * len(shape)
    return pl.BlockSpec(shape, lambda *_: zeros, pipeline_mode=pl.Buffered(1))


def _rope_tables(pos, d, reps):
    inv = 1.0 / (ROPE_THETA ** (jnp.arange(0, d, 2, dtype=F32) / d))
    ang = pos.astype(F32)[:, None] * inv[None, :]
    cos = jnp.cos(ang)
    sin = jnp.sin(ang)
    cos_t = jnp.concatenate([cos, cos] * reps, axis=-1)
    sin_t = jnp.concatenate([-sin, sin] * reps, axis=-1)
    return cos_t, sin_t


def _inproj_body(x_ref, g1_ref, wm_ref, ws_ref, wg_ref, qg_ref, kg_ref, ca_ref, sa_ref, cb_ref,
                 sb_ref, aq_o, af_o, ai_o, ag_o, q_o, kf_o, kb_o, vf_o, vb_o, qim_o, kif_o, ki2_o,
                 wi_o, ga_o, gb_o, vt_o):
    x = x_ref[...]
    tm = x.shape[0]
    hb = (_rms(x) * g1_ref[...]).astype(BF16)
    a_qk = A_HEADS * A_DK
    a_v = A_HEADS * A_DV
    b_q = B_HEADS * HEAD_DIM
    b_kv = B_KV_HEADS * HEAD_DIM
    iq_w = IDX_HEADS * IDX_DIM
    o = 0
    aq_o[...] = _dot(hb, wm_ref[:, o:o + a_qk]).astype(BF16)
    o += a_qk
    af_o[...] = _dot(hb, wm_ref[:, o:o + a_qk])
    o += a_qk
    ai_o[...] = _dot(hb, wm_ref[:, o:o + a_v]).astype(BF16)
    o += a_v
    ag_o[...] = _dot(hb, wm_ref[:, o:o + a_v]).astype(BF16)
    o += a_v

    ca = ca_ref[...]
    sa = sa_ref[...]

    def rope_head(y):
        return y * ca + pltpu.roll(y, HEAD_DIM // 2, 1) * sa

    zq = _dot(hb, wm_ref[:, o:o + b_q])
    o += b_q
    for h in range(B_HEADS):
        sl = slice(h * HEAD_DIM, (h + 1) * HEAD_DIM)
        q_o[:, sl] = (rope_head(_rms(zq[:, sl]) * qg_ref[...]) * Q_SCALE).astype(BF16)
    zk = _dot(hb, wm_ref[:, o:o + b_kv])
    o += b_kv
    for h in range(B_KV_HEADS):
        sl = slice(h * HEAD_DIM, (h + 1) * HEAD_DIM)
        y = rope_head(_rms(zk[:, sl]) * kg_ref[...])
        kf_o[pl.ds(h, tm, stride=B_KV_HEADS), :] = y
        kb_o[:, sl] = y.astype(BF16)
    zv = _dot(hb, wm_ref[:, o:o + b_kv])
    o += b_kv
    for h in range(B_KV_HEADS):
        vf_o[pl.ds(h, tm, stride=B_KV_HEADS), :] = zv[:, h * HEAD_DIM:(h + 1) * HEAD_DIM]
    vb_o[...] = zv.astype(BF16)
    vt_o[0] = zv.T.astype(BF16)

    cb = cb_ref[...]
    sb = sb_ref[...]
    lane = lax.broadcasted_iota(I32, (tm, LANES), 1)
    first_half = (lane & (IDX_DIM - 1)) < (IDX_DIM // 2)

    def rope_idx(y):
        partner = jnp.where(first_half, pltpu.roll(y, LANES - IDX_DIM // 2, 1),
                            pltpu.roll(y, IDX_DIM // 2, 1))
        return y * cb + partner * sb

    zi = _dot(hb, wm_ref[:, o:o + iq_w])
    for p in range(iq_w // LANES):
        y = rope_idx(zi[:, p * LANES:(p + 1) * LANES])
        qim_o[:, (2 * p) * LANES:(2 * p + 1) * LANES] = jnp.where(lane < IDX_DIM, y, 0.0).astype(BF16)
        qim_o[:, (2 * p + 1) * LANES:(2 * p + 2) * LANES] = jnp.where(lane >= IDX_DIM, y, 0.0).astype(BF16)
    zs = _dot(hb, ws_ref[...])
    y = rope_idx(zs[:, :LANES])
    ki2_o[...] = y.astype(BF16)
    kif_o[...] = y[:, :IDX_DIM]
    wi_o[...] = zs[:, LANES:] * (IDX_HEADS ** -0.5 * IDX_DIM ** -0.5)
    zg = _dot(hb, wg_ref[...])
    d = zg.shape[1] // 2
    ga_o[...] = zg[:, :d].astype(BF16)
    gb_o[...] = zg[:, d:].astype(BF16)


def _inproj(x2, g1, wm, ws, wg, qg, kg, tabs, tm, vt_tile):
    r, d = x2.shape
    ca, sa, cb, sb = tabs
    npos = ca.shape[0] // tm
    per = vt_tile // tm
    row = lambda w: pl.BlockSpec((tm, w), lambda i: (i, 0))
    tab = pl.BlockSpec((tm, LANES), lambda i: (i % npos, 0))
    a_qk = A_HEADS * A_DK
    b_q = B_HEADS * HEAD_DIM
    b_kv = B_KV_HEADS * HEAD_DIM
    kvh = B_KV_HEADS
    outs = [
        (1, a_qk, BF16), (1, a_qk, F32), (1, a_qk, BF16), (1, a_qk, BF16),
        (1, b_q, BF16), (kvh, HEAD_DIM, F32), (1, b_kv, BF16),
        (kvh, HEAD_DIM, F32), (1, b_kv, BF16),
        (1, IDX_HEADS * LANES, BF16), (1, IDX_DIM, F32), (1, LANES, BF16),
        (1, LANES, F32), (1, d, BF16), (1, d, BF16),
    ]
    return pl.pallas_call(
        _inproj_body,
        grid=(r // tm,),
        in_specs=[row(d), _const_spec(g1.shape), _const_spec(wm.shape), _const_spec(ws.shape),
                  _const_spec(wg.shape), _const_spec(qg.shape), _const_spec(kg.shape),
                  tab, tab, tab, tab],
        out_specs=[pl.BlockSpec((tm * k, w), lambda i: (i, 0)) for k, w, _ in outs] + [
            pl.BlockSpec((1, b_kv, tm), lambda i: (i // per, 0, i % per))],
        out_shape=[jax.ShapeDtypeStruct((r * k, w), dt) for k, w, dt in outs] + [
            jax.ShapeDtypeStruct((r // vt_tile, b_kv, vt_tile), BF16)],
        compiler_params=pltpu.CompilerParams(dimension_semantics=("arbitrary",),
                                             vmem_limit_bytes=VMEM_LIMIT),
        name=f"inproj_{r // tm}",
    )(x2, g1, wm, ws, wg, qg, kg, ca, sa, cb, sb)


HGRN_BATCH = 4


def _hgrn_body(aq_ref, af_ref, ai_ref, ag_ref, s0_ref, lbp_ref, ng_ref, oa_ref, sn_ref, st_ref,
               *, layer):
    c = pl.program_id(1)
    nc = pl.num_programs(1)
    nb = aq_ref.shape[0]

    @pl.when(c == 0)
    def _():
        for b in range(nb):
            for h in range(A_HEADS):
                st_ref[b, h] = s0_ref[b, h].T

    lbp = lbp_ref[...]
    e = jnp.exp(lbp - jnp.max(lbp, axis=0, keepdims=True))
    sm = e / jnp.sum(e, axis=0, keepdims=True)
    lb = jnp.sum(sm[:layer + 1], axis=0, keepdims=True)

    n = aq_ref.shape[1]
    r_i = lax.broadcasted_iota(I32, (n, n), 0)
    c_i = lax.broadcasted_iota(I32, (n, n), 1)
    causal = r_i >= c_i
    tri = jnp.where(causal, 1.0, 0.0).astype(BF16)
    prep = []
    for b in range(nb):
        f = lb + (1.0 - lb) * jax.nn.sigmoid(af_ref[b])
        lf = jnp.log(f)
        hi = lf.astype(BF16)
        r1 = lf - hi.astype(F32)
        mid = r1.astype(BF16)
        lo = (r1 - mid.astype(F32)).astype(BF16)
        cum = _dot(tri, hi) + _dot(tri, mid) + _dot(tri, lo)
        last = cum[n - 1:n, :]
        qd = _silu(aq_ref[b].astype(F32)) * jnp.exp(cum)
        k = 1.0 - f
        prep.append((qd, k * jnp.exp(-cum), k * jnp.exp(last - cum), jnp.exp(last),
                     _silu(ag_ref[b].astype(F32))))
    for h in range(A_HEADS):
        sl = slice(h * A_DK, (h + 1) * A_DK)
        for b, (qd, kd, kt, el, gate) in enumerate(prep):
            qd_h = qd[:, sl].astype(BF16)
            v_h = ai_ref[b, :, sl]
            att = jnp.where(causal, _dot_nt(qd_h, kd[:, sl].astype(BF16)), 0.0)
            st = st_ref[b, h]
            o = _dot(att.astype(BF16), v_h) + _dot_nt(qd_h, st.astype(BF16))
            st_ref[b, h] = st * el[:, sl] + _dot_tn(v_h, kt[:, sl].astype(BF16))
            oa_ref[b, :, sl] = (_rms(o) * ng_ref[...] * gate[:, sl]).astype(BF16)

    @pl.when(c == nc - 1)
    def _():
        for b in range(nb):
            for h in range(A_HEADS):
                sn_ref[b, h] = st_ref[b, h].T


def _hgrn(aq, af, ai, ag, s0, lbp, ng, bsz, t, layer):
    d = aq.shape[1]
    nc = t // CHUNK
    nb = _row_tile(bsz, HGRN_BATCH)
    seq = lambda a: a.reshape(bsz, t, d)
    row = pl.BlockSpec((nb, CHUNK, d), lambda g, c: (g, c, 0))
    st = pl.BlockSpec((nb, A_HEADS, A_DK, A_DV), lambda g, c: (g, 0, 0, 0))
    oa, s_new = pl.pallas_call(
        functools.partial(_hgrn_body, layer=layer),
        grid=(bsz // nb, nc),
        in_specs=[row, row, row, row, st, _const_spec(lbp.shape), _const_spec(ng.shape)],
        out_specs=[row, st],
        out_shape=[jax.ShapeDtypeStruct((bsz, t, d), BF16),
                   jax.ShapeDtypeStruct((bsz, A_HEADS, A_DK, A_DV), F32)],
        scratch_shapes=[pltpu.VMEM((nb, A_HEADS, A_DV, A_DK), F32)],
        compiler_params=pltpu.CompilerParams(dimension_semantics=("arbitrary", "arbitrary"),
                                             vmem_limit_bytes=VMEM_LIMIT),
        name=f"hgrn_{bsz // nb}x{nc}",
    )(seq(aq), seq(af), seq(ai), seq(ag), s0, lbp, ng)
    return oa.reshape(bsz * t, d), s_new


def _chunk_of(pos):
    return jnp.right_shift(pos, CHUNK.bit_length() - 1)


def _transpose_wi(wi, tq):
    if tq % LANES:
        wi = jnp.concatenate([wi, jnp.zeros((LANES - tq % LANES, LANES), F32)], axis=0)
    return wi.T[:IDX_HEADS, :tq]


def _score_tile(ki2_tile, qim_ref, wt):
    acc = None
    for h in range(IDX_HEADS):
        s = _dot_nt(ki2_tile, qim_ref[:, h * LANES:(h + 1) * LANES])
        term = jnp.maximum(s, 0.0) * wt[h:h + 1, :]
        acc = term if acc is None else acc + term
    return acc


def _mask_scores(score, adm):
    score = jnp.where(score == 0.0, 0.0, score)
    return score if adm is None else jnp.where(adm, score, -jnp.inf)


def _sort_key(score):
    bits = lax.bitcast_convert_type(score, I32)
    return jnp.where(bits < 0, bits ^ 0x7FFFFFFF, bits)


def _key_to_score(key):
    return lax.bitcast_convert_type(jnp.where(key < 0, key ^ 0x7FFFFFFF, key), F32)


SUBLANES = 8
COUNT_ACCS = 4
KEY_BITS = 32
PLANE_KEYS = SUBLANES * KEY_BITS


def _store_bit_planes(planes_ref, key, blk0):
    u = key ^ INT_MIN
    tk = u.shape[0]
    for blk in range(tk // PLANE_KEYS):
        a = [u[blk * PLANE_KEYS + i * SUBLANES:blk * PLANE_KEYS + (i + 1) * SUBLANES, :]
             for i in range(KEY_BITS)]
        for j, m in ((16, 0x0000FFFF), (8, 0x00FF00FF), (4, 0x0F0F0F0F), (2, 0x33333333),
                     (1, 0x55555555)):
            k = 0
            while k < KEY_BITS:
                t = (a[k] ^ lax.shift_right_logical(a[k + j], j)) & m
                a[k] = a[k] ^ t
                a[k + j] = a[k + j] ^ lax.shift_left(t, j)
                k = (k + j + 1) & ~j
        row = pl.multiple_of((blk0 + blk) * SUBLANES, SUBLANES)
        for p in range(KEY_BITS):
            planes_ref[p, pl.ds(row, SUBLANES), :] = a[p]


def _select(sc_ref, planes_ref, bias_ref, cut_ref, stat_ref, nkt, krow, tk, tq, nbits):
    nrow = planes_ref.shape[1]
    nblk = nkt * (tk // PLANE_KEYS)
    blk_of_row = jnp.right_shift(lax.broadcasted_iota(I32, (nrow, tq), 0),
                                 SUBLANES.bit_length() - 1)

    def word_count(words):
        pc = lax.population_count(words).reshape(nrow // SUBLANES, SUBLANES, tq)
        return jnp.sum(jnp.sum(pc, axis=0).astype(F32), axis=0, keepdims=True)

    def bit_body(i, carry):
        alive, above, thr_u = carry
        ones = alive & planes_ref[i]
        c1 = word_count(ones)
        take = above + c1 >= krow
        alive = jnp.where(take, ones, alive ^ ones)
        above = jnp.where(take, above, above + c1)
        thr_u = jnp.where(take, thr_u | jnp.left_shift(jnp.int32(1), KEY_BITS - 1 - i), thr_u)
        return alive, above, thr_u

    alive0 = jnp.where(blk_of_row < nblk, -1, 0).astype(I32)
    _, _, thr_u = lax.fori_loop(
        0, KEY_BITS, bit_body, (alive0, jnp.zeros((1, tq), F32), jnp.zeros((1, tq), I32)))

    nchunk = tk // SUBLANES
    sub = lax.broadcasted_iota(I32, (SUBLANES, tq), 0)

    def count(pred_fn):
        def body(j, accs):
            t = sc_ref[j]
            accs = list(accs)
            for c in range(nchunk):
                a = c % COUNT_ACCS
                accs[a] = pred_fn(t[c * SUBLANES:(c + 1) * SUBLANES, :], j, c, accs[a])
            return tuple(accs)
        zero = jnp.zeros((SUBLANES, tq), F32)
        accs = lax.fori_loop(0, nkt, body, (zero,) * COUNT_ACCS)
        tot = accs[0]
        for a in accs[1:]:
            tot = tot + a
        return jnp.sum(tot, axis=0, keepdims=True)

    def count_ge(cand):
        c8 = jnp.broadcast_to(cand, (SUBLANES, tq))
        return count(lambda t, j, c, acc: jnp.where(t >= c8, acc + 1.0, acc))

    def count_gt(cand):
        c8 = jnp.broadcast_to(cand, (SUBLANES, tq))
        return count(lambda t, j, c, acc: jnp.where(t > c8, acc + 1.0, acc))

    def record(thr):
        stat_ref[0:1, :] = thr
        stat_ref[1:2, :] = count_ge(thr)
        stat_ref[2:3, :] = count_gt(thr)

    record(_key_to_score(thr_u ^ INT_MIN))
    good = (stat_ref[2:3, :] < krow) & (stat_ref[1:2, :] >= krow)

    @pl.when(jnp.max(jnp.where(good, 0.0, 1.0)) > 0.0)
    def _():
        def cmp_body(i, tu):
            cand_u = tu | jnp.left_shift(jnp.int32(1), KEY_BITS - 1 - i)
            return jnp.where(count_ge(_key_to_score(cand_u ^ INT_MIN)) >= krow, cand_u, tu)
        tu = lax.fori_loop(0, KEY_BITS, cmp_body, jnp.zeros((1, tq), I32))
        record(_key_to_score(tu ^ INT_MIN))

    thr = stat_ref[0:1, :]
    n_ge = stat_ref[1:2, :]
    need = krow - stat_ref[2:3, :]

    cut_ref[...] = jnp.full(cut_ref.shape, 2 ** 31 - 1, I32)

    @pl.when(jnp.max(jnp.where(n_ge > krow, 1.0, 0.0)) > 0.0)
    def _():
        thr8 = jnp.broadcast_to(thr, (SUBLANES, tq))

        def idx_body(i, cut):
            cand = cut | jnp.left_shift(jnp.int32(1), nbits - 1 - i)
            c8 = jnp.broadcast_to(cand, (SUBLANES, tq))
            below = count(lambda t, j, c, acc: jnp.where(
                t == thr8, jnp.where(j * tk + c * SUBLANES + sub < c8, acc + 1.0, acc), acc))
            return jnp.where(below < need, cand, cut)
        cut_ref[...] = lax.fori_loop(0, nbits, idx_body, jnp.zeros((1, tq), I32))

    cut = cut_ref[...]

    def write(j, carry):
        t = sc_ref[j]
        idx = j * tk + lax.broadcasted_iota(I32, (tk, tq), 0)
        tie = jnp.where(t == thr, jnp.where(idx <= cut, 0.0, NEG), NEG)
        bias = jnp.where(t > thr, 0.0, tie)
        if callable(bias_ref):
            bias_ref(j, bias)
        else:
            bias_ref[j] = bias
        return carry

    lax.fori_loop(0, nkt, write, 0)


def _stack_heads(q_ref, u, stack):
    heads = [q_ref[:, h * HEAD_DIM:(h + 1) * HEAD_DIM] for h in range(u * stack, (u + 1) * stack)]
    return heads[0] if stack == 1 else jnp.concatenate(heads, axis=0)


def _store_heads(o_ref, u, stack, o, tq):
    for r in range(stack):
        h = u * stack + r
        o_ref[:, h * HEAD_DIM:(h + 1) * HEAD_DIM] = o[r * tq:(r + 1) * tq].astype(o_ref.dtype)


Q_SCALE = HEAD_DIM ** -0.5 * 1.4426950408889634
SUM_ROWS = 16


def _attn_reset(m_ref, l_ref, acc_ref):
    m_ref[...] = jnp.full(m_ref.shape, NEG, F32)
    l_ref[...] = jnp.zeros(l_ref.shape, F32)
    acc_ref[...] = jnp.zeros(acc_ref.shape, F32)


def _attn_logits(q_ref, u, stack, kt, bias, m_ref, state=None):
    state = u if state is None else state
    qg = _stack_heads(q_ref, u, stack)
    if bias.shape[1] != qg.shape[0]:
        bias = jnp.concatenate([bias] * stack, axis=1)
    s = _dot_nt(kt, qg) + bias
    return s, jnp.maximum(m_ref[state], jnp.max(s, axis=0, keepdims=True))


def _attn_accumulate(u, s, m_new, m_ref, l_ref, acc_ref, pv):
    alpha = jnp.exp2(m_ref[u] - m_new)
    p = jnp.exp2(s - m_new)
    if l_ref is not None:
        l_ref[u] = alpha * l_ref[u] + jnp.sum(p, axis=0, keepdims=True)
    acc_ref[u] = alpha * acc_ref[u] + pv(p.astype(BF16))
    m_ref[u] = m_new


def _attn_step(q_ref, u, stack, kt, bias, m_ref, l_ref, acc_ref, pv, state=None):
    state = u if state is None else state
    s, m_new = _attn_logits(q_ref, u, stack, kt, bias, m_ref, state)
    _attn_accumulate(state, s, m_new, m_ref, l_ref, acc_ref, pv)


def _attn_finish(o_ref, u, stack, l_ref, acc_ref, tq, state=None):
    state = u if state is None else state
    acc = acc_ref[state]
    l = acc[HEAD_DIM:HEAD_DIM + 1] if l_ref is None else l_ref[state]
    _store_heads(o_ref, u, stack, (acc[:HEAD_DIM] / l).T, tq)


def _dsa_prompt_body(q_ref, qim_ref, wi_ref, kb_ref, vt_ref, ki2_ref, o_ref, sc_ref, planes_ref,
                     bias_ref, cut_ref, stat_ref, m_ref, l_ref, acc_ref,
                     *, tq, tk, topk, nbits):
    q0 = pl.program_id(1) * tq
    nkt = lax.div(q0 + tq + tk - 1, tk)
    wt = _transpose_wi(wi_ref[...], tq)
    qchunk = _chunk_of(q0 + lax.broadcasted_iota(I32, (1, tq), 1))

    @pl.when((pl.program_id(0) == 0) & (pl.program_id(1) == 0))
    def _():
        planes_ref[...] = jnp.zeros(planes_ref.shape, I32)

    def score_body(j, carry):
        ks = pl.multiple_of(j * tk, tk)
        score = _score_tile(ki2_ref[pl.ds(ks, tk), :], qim_ref, wt)
        kpos = ks + lax.broadcasted_iota(I32, (tk, tq), 0)
        score = _mask_scores(score, _chunk_of(kpos) <= qchunk)
        sc_ref[j] = score
        _store_bit_planes(planes_ref, _sort_key(score), j * (tk // PLANE_KEYS))
        return carry

    lax.fori_loop(0, nkt, score_body, 0)

    krow = jnp.minimum((qchunk + 1) * CHUNK, topk).astype(F32)
    _select(sc_ref, planes_ref, bias_ref, cut_ref, stat_ref, nkt, krow, tk, tq, nbits)

    _attn_reset(m_ref, l_ref, acc_ref)

    rep = B_HEADS // B_KV_HEADS
    ones = jnp.ones((SUM_ROWS, tk), BF16)

    def att_body(j, carry):
        ks = pl.multiple_of(j * tk, tk)
        bias = bias_ref[j]
        gsl = lambda g: slice(g * HEAD_DIM, (g + 1) * HEAD_DIM)
        logits = lambda g: _attn_logits(q_ref, g, rep, kb_ref[pl.ds(ks, tk), gsl(g)], bias, m_ref)
        pending = [logits(0), logits(1)]
        for g in range(B_KV_HEADS):
            if g + 2 < B_KV_HEADS:
                pending.append(logits(g + 2))
            vt = jnp.concatenate([vt_ref[j, gsl(g), :], ones], axis=0)
            _attn_accumulate(g, *pending[g], m_ref, None, acc_ref, lambda p, vt=vt: _dot(vt, p))
        return carry

    lax.fori_loop(0, nkt, att_body, 0)
    for g in range(B_KV_HEADS):
        _attn_finish(o_ref, g, rep, None, acc_ref, tq)


def _dsa_scratch(nt, tk, tq, stack, acc_rows):
    units = B_HEADS // stack
    return [pltpu.VMEM((nt, tk, tq), F32),
            pltpu.VMEM((KEY_BITS, nt * tk // KEY_BITS, tq), I32),
            pltpu.VMEM((nt, tk, tq), F32),
            pltpu.VMEM((1, tq), I32),
            pltpu.VMEM((SUBLANES, tq), F32),
            pltpu.VMEM((units, 1, stack * tq), F32),
            pltpu.VMEM((units, 1, stack * tq), F32),
            pltpu.VMEM((units, acc_rows, stack * tq), F32)]


def _dsa_prompt(q, qim, wi, kb, vt, ki2, bsz, t, tq, tk):
    nq = t // tq
    nt = t // tk
    topk = min(IDX_TOPK_MAX, t // 4)
    nbits = max(1, (t - 1).bit_length())
    qrow = lambda w: pl.BlockSpec((tq, w), lambda b, i: (b * nq + i, 0))
    seq = lambda w: pl.BlockSpec((t, w), lambda b, i: (b, 0))
    return pl.pallas_call(
        functools.partial(_dsa_prompt_body, tq=tq, tk=tk, topk=topk, nbits=nbits),
        grid=(bsz, nq),
        in_specs=[qrow(q.shape[1]), qrow(qim.shape[1]), qrow(LANES), seq(kb.shape[1]),
                  pl.BlockSpec((nt, vt.shape[1], tk), lambda b, i: (b, 0, 0)), seq(LANES)],
        out_specs=qrow(q.shape[1]),
        out_shape=jax.ShapeDtypeStruct(q.shape, BF16),
        scratch_shapes=_dsa_scratch(nt, tk, tq, B_HEADS // B_KV_HEADS, HEAD_DIM + SUM_ROWS),
        compiler_params=pltpu.CompilerParams(dimension_semantics=("arbitrary", "arbitrary"),
                                             vmem_limit_bytes=VMEM_LIMIT),
        name="dsa_prompt",
    )(q, qim, wi, kb, vt, ki2)


def _dsa_sample_body(q_ref, qim_ref, wi_ref, kn_ref, vn_ref, ki2n_ref, kidx2_ref, *refs,
                     t, tk, nsub, npt, past, topk, nbits, nbat):
    caches, refs = refs[:2 * nbat], refs[2 * nbat:]
    o_ref, sc_ref, planes_ref, bias_ref, cut_ref, stat_ref, m_ref, l_ref, acc_ref = refs
    j = pl.program_id(1)
    nj = pl.num_programs(1)
    tq = nbat * t
    rep = B_HEADS // B_KV_HEADS
    pad = jnp.zeros((tk - t, LANES), BF16)
    nb = tk // PLANE_KEYS
    rows_of = lambda b: pl.ds(b * t, t)

    @pl.when(j == 0)
    def _():
        wt = _transpose_wi(wi_ref[...], tq)
        row = lax.broadcasted_iota(I32, (tq, 1), 0)
        qims = [jnp.where((row >= b * t) & (row < (b + 1) * t), qim_ref[...].astype(F32),
                          0.0).astype(BF16) for b in range(nbat)]

        def scores(tile_of):
            total = None
            for b in range(nbat):
                s = _score_tile(tile_of(b), qims[b], wt)
                total = s if total is None else total + s
            return total

        def score_body(jt, carry):
            ks = pl.multiple_of(jt * tk, tk)
            score = _mask_scores(scores(lambda b: kidx2_ref[pl.ds(b * past + ks, tk), :]), None)
            sc_ref[jt] = score
            _store_bit_planes(planes_ref, _sort_key(score), jt * nb)
            return carry

        lax.fori_loop(0, npt, score_body, 0)
        krow_i = lax.broadcasted_iota(I32, (tk, tq), 0)
        qchunk = _chunk_of(past + (lax.broadcasted_iota(I32, (1, tq), 1) & (t - 1)))
        adm = jnp.where(krow_i < t, _chunk_of(past + krow_i), 2 ** 30) <= qchunk
        score = _mask_scores(
            scores(lambda b: jnp.concatenate([ki2n_ref[rows_of(b), :], pad], axis=0)), adm)
        sc_ref[npt] = score
        _store_bit_planes(planes_ref, _sort_key(score), npt * nb)
        krow = jnp.minimum((qchunk + 1) * CHUNK, topk).astype(F32)

        def write_bias(jt, bias):
            for b in range(nbat):
                bias_ref[b, jt] = jnp.concatenate([bias[:, b * t:(b + 1) * t]] * rep, axis=1)

        _select(sc_ref, planes_ref, write_bias, cut_ref, stat_ref, npt + 1, krow, tk, tq, nbits)
        _attn_reset(m_ref, l_ref, acc_ref)

    units = [(b, g) for b in range(nbat) for g in range(B_KV_HEADS)]
    state = lambda b, g: b * B_KV_HEADS + g

    for su in range(nsub):
        rows = lambda g: pl.ds(su * tk * B_KV_HEADS + g, tk, stride=B_KV_HEADS)
        pending = [_attn_logits(q_ref.at[rows_of(b)], g, rep,
                                caches[2 * b][rows(g), :].astype(BF16),
                                bias_ref[b, j * nsub + su], m_ref, state(b, g))
                   for b, g in units]
        for (b, g), (s, m_new) in zip(units, pending):
            vt = caches[2 * b + 1][rows(g), :].astype(BF16)
            _attn_accumulate(state(b, g), s, m_new, m_ref, l_ref, acc_ref,
                             lambda p, vt=vt: _dot_tn(vt, p))

    @pl.when(j == nj - 1)
    def _():
        for b, g in units:
            gs = slice(g * HEAD_DIM, (g + 1) * HEAD_DIM)
            kt = jnp.concatenate([kn_ref[rows_of(b), gs], pad], axis=0)
            vt = jnp.concatenate([vn_ref[rows_of(b), gs], pad], axis=0)
            _attn_step(q_ref.at[rows_of(b)], g, rep, kt, bias_ref[b, npt], m_ref, l_ref, acc_ref,
                       lambda p, vt=vt: _dot_tn(vt, p), state(b, g))
            _attn_finish(o_ref.at[rows_of(b)], g, rep, l_ref, acc_ref, t, state(b, g))


def _dsa_sample(q, qim, wi, kn, vn, ki2n, kidx2, ck, cv, bsz, t, past, tk, nsub, cache_off):
    fits = t < LANES and LANES % t == 0 and t & (t - 1) == 0 and bsz % (LANES // t) == 0
    nbat = LANES // t if fits else 1
    tq = nbat * t
    npt = past // tk
    nj = npt // nsub
    nt = npt + 1
    topk = min(IDX_TOPK_MAX, (past + t) // 4)
    nbits = max(1, (past + t - 1).bit_length())
    rep = B_HEADS // B_KV_HEADS
    units = nbat * B_KV_HEADS
    qrow = lambda w: pl.BlockSpec((tq, w), lambda g, j: (g, 0))
    kvw = kn.shape[1]
    cache = lambda b: pl.BlockSpec((nsub * tk * B_KV_HEADS, HEAD_DIM),
                                   lambda g, j: ((cache_off + g * nbat + b) * nj + j, 0))
    caches = [spec for b in range(nbat) for spec in (cache(b), cache(b))]
    scratch = [pltpu.VMEM((nt, tk, tq), F32),
               pltpu.VMEM((KEY_BITS, nt * tk // KEY_BITS, tq), I32),
               pltpu.VMEM((nbat, nt, tk, rep * t), F32),
               pltpu.VMEM((1, tq), I32),
               pltpu.VMEM((SUBLANES, tq), F32),
               pltpu.VMEM((units, 1, rep * t), F32),
               pltpu.VMEM((units, 1, rep * t), F32),
               pltpu.VMEM((units, HEAD_DIM, rep * t), F32)]
    return pl.pallas_call(
        functools.partial(_dsa_sample_body, t=t, tk=tk, nsub=nsub, npt=npt, past=past,
                          topk=topk, nbits=nbits, nbat=nbat),
        grid=(bsz // nbat, nj),
        in_specs=[qrow(q.shape[1]), qrow(qim.shape[1]), qrow(LANES), qrow(kvw), qrow(kvw),
                  qrow(LANES), pl.BlockSpec((nbat * past, LANES), lambda g, j: (g, 0))] + caches,
        out_specs=qrow(q.shape[1]),
        out_shape=jax.ShapeDtypeStruct(q.shape, BF16),
        scratch_shapes=scratch,
        compiler_params=pltpu.CompilerParams(dimension_semantics=("arbitrary", "arbitrary"),
                                             vmem_limit_bytes=VMEM_LIMIT),
        name="dsa_sample",
    )(q, qim, wi, kn, vn, ki2n, kidx2, *([ck, cv] * nbat))


def _post_body(oa_ref, ob_ref, ga_ref, gb_ref, x_ref, wa_ref, wb_ref, wo_ref, g2_ref, wr_ref,
               br_ref, x1_o, t_o, eidx_o, prob_o, cnt_o):
    ya = _dot(oa_ref[...], wa_ref[...])
    yb = _dot(ob_ref[...], wb_ref[...])
    merged = (jax.nn.sigmoid(ga_ref[...].astype(F32)) * ya
              + jax.nn.sigmoid(gb_ref[...].astype(F32)) * yb)
    x1 = x_ref[...] + _dot(merged.astype(BF16), wo_ref[...])
    x1_o[...] = x1
    tok = _rms(x1) * g2_ref[...]
    t_o[...] = tok
    logits = _dot(tok.astype(BF16), wr_ref[...]) + br_ref[...]
    tm = logits.shape[0]
    lane = lax.broadcasted_iota(I32, (tm, LANES), 1).astype(F32)
    cur = jnp.where(lane < N_EXPERTS, logits, -jnp.inf)
    top = None
    den = jnp.zeros((tm, 1), F32)
    eidx = jnp.zeros((tm, LANES), F32)
    prob = jnp.zeros((tm, LANES), F32)
    chosen = jnp.zeros((tm, LANES), F32)
    for k in range(TOP_K):
        mx = jnp.max(cur, axis=1, keepdims=True)
        first = jnp.min(jnp.where(cur == mx, lane, float(LANES)), axis=1, keepdims=True)
        if top is None:
            top = mx
        e = jnp.exp(mx - top)
        den = den + e
        eidx = jnp.where(lane == k, first, eidx)
        prob = jnp.where(lane == k, e, prob)
        hit = lane == first
        chosen = jnp.where(hit, 1.0, chosen)
        cur = jnp.where(hit, -jnp.inf, cur)
    eidx_o[...] = eidx.astype(I32)
    prob_o[...] = prob / den

    @pl.when(pl.program_id(0) == 0)
    def _():
        cnt_o[...] = jnp.zeros(cnt_o.shape, F32)

    cnt_o[...] += jnp.sum(chosen, axis=0, keepdims=True)


def _post(oa, ob, ga, gb, x2, wa, wb, wo, g2, wr, br, tm):
    r, d = x2.shape
    row = lambda w: pl.BlockSpec((tm, w), lambda i: (i, 0))
    return pl.pallas_call(
        _post_body,
        grid=(r // tm,),
        in_specs=[row(d), row(d), row(d), row(d), row(d), _const_spec(wa.shape),
                  _const_spec(wb.shape), _const_spec(wo.shape), _const_spec(g2.shape),
                  _const_spec(wr.shape), _const_spec(br.shape)],
        out_specs=[row(d), row(d), row(LANES), row(LANES),
                   pl.BlockSpec((1, LANES), lambda i: (0, 0))],
        out_shape=[jax.ShapeDtypeStruct((r, d), F32), jax.ShapeDtypeStruct((r, d), F32),
                   jax.ShapeDtypeStruct((r, LANES), I32), jax.ShapeDtypeStruct((r, LANES), F32),
                   jax.ShapeDtypeStruct((1, LANES), F32)],
        compiler_params=pltpu.CompilerParams(dimension_semantics=("arbitrary",),
                                             vmem_limit_bytes=VMEM_LIMIT),
        name=f"post_{r // tm}",
    )(oa, ob, ga, gb, x2, wa, wb, wo, g2, wr, br)


MOE_TILE = 512
DMA_PRIORITIES = 2
ISSUE_UNROLL = 8


WIN_ROWS = 64
FLAG_LANE = LANES - 1


def _rank_body(eidx_ref, cin_ref, pos_o, col_o, win_o, carry_ref, *, max_start):
    @pl.when(pl.program_id(0) == 0)
    def _():
        carry_ref[...] = cin_ref[...]

    eidx = eidx_ref[...]
    tm = eidx.shape[0]
    lane = lax.broadcasted_iota(I32, (tm, LANES), 1)
    hits = [lane == eidx[:, k:k + 1] for k in range(TOP_K)]
    onehot = jnp.zeros((tm, LANES), F32)
    for hit in hits:
        onehot = onehot + jnp.where(hit, 1.0, 0.0)
    r_i = lax.broadcasted_iota(I32, (tm, tm), 0)
    c_i = lax.broadcasted_iota(I32, (tm, tm), 1)
    before = jnp.where(c_i < r_i, 1.0, 0.0).astype(BF16)
    carry = carry_ref[...]
    start = jnp.minimum((carry.astype(I32) >> 3) << 3, max_start)
    base = carry + _dot(before, onehot.astype(BF16))
    rank = jnp.zeros((tm, LANES), F32)
    col = jnp.zeros((tm, LANES), F32)
    worst = jnp.zeros((tm, 1), F32)
    for k, hit in enumerate(hits):
        rk = jnp.sum(jnp.where(hit, base, 0.0), axis=1, keepdims=True)
        off = jnp.sum(jnp.where(hit, base - start.astype(F32), 0.0), axis=1, keepdims=True)
        rank = jnp.where(lane == k, rk, rank)
        col = jnp.where(lane == k, eidx[:, k:k + 1].astype(F32) * WIN_ROWS + off, col)
        worst = jnp.maximum(worst, off)
    pos_o[...] = rank.T[:SUBLANES, :].astype(I32)
    col_o[...] = col.astype(I32)
    flag = (jnp.max(worst) >= WIN_ROWS).astype(I32)
    win_o[...] = jnp.where(lane[:1] == FLAG_LANE, flag, start).reshape(win_o.shape)
    carry_ref[...] += jnp.sum(onehot, axis=0, keepdims=True)


def _rank(eidx, cin, tm, max_start):
    r = eidx.shape[0]
    return pl.pallas_call(
        functools.partial(_rank_body, max_start=max_start),
        grid=(r // tm,),
        in_specs=[pl.BlockSpec((tm, LANES), lambda i: (i, 0)), _const_spec(cin.shape)],
        out_specs=[pl.BlockSpec((SUBLANES, tm), lambda i: (0, i)),
                   pl.BlockSpec((tm, LANES), lambda i: (i, 0)),
                   pl.BlockSpec((1, 1, LANES), lambda i: (i, 0, 0))],
        out_shape=[jax.ShapeDtypeStruct((SUBLANES, r), I32),
                   jax.ShapeDtypeStruct((r, LANES), I32),
                   jax.ShapeDtypeStruct((r // tm, 1, LANES), I32)],
        scratch_shapes=[pltpu.VMEM((1, LANES), F32)],
        compiler_params=pltpu.CompilerParams(dimension_semantics=("arbitrary",)),
        name=f"moe_rank_{r // tm}",
    )(eidx, cin)


def _wait_rows(ref, sem, times):
    for _ in range(times):
        pltpu.make_async_copy(ref, ref, sem).wait()


def _zero_pad_rows(pad_ref, xs_out, zeros, sem):
    zeros[...] = jnp.zeros(zeros.shape, zeros.dtype)
    row = zeros.at[pl.ds(0, 1)]
    rows8 = zeros.at[pl.ds(0, SUBLANES)]
    tile = zeros.shape[0]

    def for_each_copy(fn):
        def per_group(g, carry):
            first, n = pad_ref[0, g], pad_ref[1, g]
            head = jnp.minimum(n, (SUBLANES - (first & (SUBLANES - 1))) & (SUBLANES - 1))

            def per_row(t, c):
                fn(pltpu.make_async_copy(row, xs_out.at[pl.ds(first + t, 1)], sem))
                return c
            lax.fori_loop(0, head, per_row, 0)

            def per_block(t, c):
                at = pl.multiple_of(first + head + t * SUBLANES, SUBLANES)
                fn(pltpu.make_async_copy(rows8, xs_out.at[pl.ds(at, SUBLANES)], sem))
                return c
            lax.fori_loop(0, lax.div(n - head, SUBLANES), per_block, 0)
            return carry
        lax.fori_loop(0, N_EXPERTS, per_group, 0)
        first = pad_ref[0, N_EXPERTS]

        def per_tile(t, c):
            fn(pltpu.make_async_copy(
                zeros, xs_out.at[pl.ds(pl.multiple_of(first + t * tile, tile), tile)], sem))
            return c
        lax.fori_loop(0, lax.div(pad_ref[1, N_EXPERTS], tile), per_tile, 0)

    for_each_copy(lambda copy: copy.start())
    for_each_copy(lambda copy: copy.wait())


def _dispatch_body(pad_ref, *refs, bounds):
    ns = len(bounds) - 1
    pos_refs, tok_refs = refs[:ns], refs[ns:2 * ns]
    xs_out, ring, sems, zeros, zsem = refs[2 * ns:]
    i = pl.program_id(0)
    tm = ring.shape[1]
    last = pl.num_programs(0) - 1

    pl.when(i == 0)(functools.partial(_zero_pad_rows, pad_ref, xs_out, zeros, zsem))

    for slot in range(2):
        mine = i % 2 == slot
        for s in range(ns):
            @pl.when(mine & (i >= bounds[s]) & (i < bounds[s + 1]))
            def _(slot=slot, s=s):
                ring[slot] = tok_refs[s][...]

                def issue(t, carry):
                    for k in range(TOP_K):
                        pltpu.make_async_copy(ring.at[slot, pl.ds(t, 1)],
                                              xs_out.at[pl.ds(pos_refs[s][t * TOP_K + k], 1)],
                                              sems.at[slot]).start(priority=k % DMA_PRIORITIES)
                    return carry

                lax.fori_loop(0, tm, issue, 0)

        @pl.when(mine & (i > 0))
        def _(slot=slot):
            _wait_rows(ring.at[1 - slot], sems.at[1 - slot], TOP_K)

        @pl.when(mine & (i == last))
        def _(slot=slot):
            _wait_rows(ring.at[slot], sems.at[slot], TOP_K)


def _dispatch(poss, pad, toks, rows, tm):
    d = toks[0].shape[1]
    bounds = [0]
    for tok in toks:
        bounds.append(bounds[-1] + tok.shape[0] // tm)

    def local(s):
        lo, n = bounds[s], bounds[s + 1] - bounds[s]
        return lambda i: jnp.clip(i - lo, 0, n - 1)

    in_specs = [pl.BlockSpec(memory_space=pltpu.SMEM)]
    in_specs += [pl.BlockSpec((tm * TOP_K,), lambda i, f=local(s): (f(i),),
                              memory_space=pltpu.SMEM) for s in range(len(toks))]
    in_specs += [pl.BlockSpec((tm, d), lambda i, f=local(s): (f(i), 0)) for s in range(len(toks))]
    return pl.pallas_call(
        functools.partial(_dispatch_body, bounds=tuple(bounds)),
        grid=(bounds[-1],),
        in_specs=in_specs,
        out_specs=pl.BlockSpec(memory_space=pl.ANY),
        out_shape=jax.ShapeDtypeStruct((rows, d), toks[0].dtype),
        scratch_shapes=[pltpu.VMEM((2, tm, d), toks[0].dtype), pltpu.SemaphoreType.DMA((2,)),
                        pltpu.VMEM((MOE_TILE, d), toks[0].dtype), pltpu.SemaphoreType.DMA],
        compiler_params=pltpu.CompilerParams(dimension_semantics=("arbitrary",),
                                             has_side_effects=True),
        name="moe_dispatch",
    )(pad, *poss, *toks)


def _experts_body(te_ref, na_ref, first_ref, next_ref, slot_ref, x_ref, wgu_hbm, bgu_ref, wd_hbm,
                  bd_ref, y_ref, wgu_f, wd_f, wgu_s, wd_s, sems):
    r = pl.program_id(0)

    def fetch(e, slot):
        return (pltpu.make_async_copy(wgu_hbm.at[e], wgu_f.at[slot], sems.at[0, slot]),
                pltpu.make_async_copy(wd_hbm.at[e], wd_f.at[slot], sems.at[1, slot]))

    @pl.when(r == 0)
    def _():
        for copy in fetch(te_ref[0], slot_ref[0]):
            copy.start()

    @pl.when(first_ref[r] != 0)
    def _():
        slot = slot_ref[r]
        for copy in fetch(te_ref[r], slot):
            copy.wait()
        rows = 128
        for c in range(wgu_s.shape[0] // rows):
            wgu_s[c * rows:(c + 1) * rows, :] = wgu_f[slot, c * rows:(c + 1) * rows, :].astype(BF16)
        for c in range(wd_s.shape[0] // rows):
            wd_s[c * rows:(c + 1) * rows, :] = wd_f[slot, c * rows:(c + 1) * rows, :].astype(BF16)

        @pl.when(next_ref[r] >= 0)
        def _():
            for copy in fetch(next_ref[r], 1 - slot):
                copy.start()

    @pl.when(r < na_ref[0])
    def _():
        gu = _dot(x_ref[...].astype(BF16), wgu_s[...]) + bgu_ref[0]
        dff = gu.shape[1] // 2
        gate = jnp.minimum(gu[:, :dff], SWIGLU_LIMIT)
        up = jnp.clip(gu[:, dff:], -SWIGLU_LIMIT, SWIGLU_LIMIT)
        act = (up + 1.0) * gate * jax.nn.sigmoid(SWIGLU_ALPHA * gate)
        y_ref[...] = _dot(act.astype(BF16), wd_s[...]) + bd_ref[0]

    @pl.when(r >= na_ref[0])
    def _():
        y_ref[...] = jnp.zeros(y_ref.shape, F32)


def _experts(tile_expert, n_active, xs, wgu, bgu, wd, bd):
    p, d = xs.shape
    _, _, dff2 = wgu.shape
    tm = MOE_TILE
    n = p // tm
    r = jnp.arange(n, dtype=I32)
    active = r < n_active[0]
    change = jnp.concatenate([jnp.ones((1,), bool), tile_expert[1:] != tile_expert[:-1]])
    first = (change & active).astype(I32)
    group = jnp.cumsum(first) - 1
    slot = jnp.maximum(group, 0) % 2
    opens = jnp.where(first > 0, r, n)
    next_open = jnp.min(jnp.where(opens[None, :] > r[:, None], opens[None, :], n), axis=1)
    nxt = jnp.where(next_open < n, tile_expert[jnp.minimum(next_open, n - 1)], -1).astype(I32)
    grid_spec = pltpu.PrefetchScalarGridSpec(
        num_scalar_prefetch=5,
        grid=(n,),
        in_specs=[pl.BlockSpec((tm, d), lambda r, te, *_: (r, 0)),
                  pl.BlockSpec(memory_space=pl.ANY),
                  pl.BlockSpec((1, 1, dff2), lambda r, te, *_: (te[r], 0, 0)),
                  pl.BlockSpec(memory_space=pl.ANY),
                  pl.BlockSpec((1, 1, d), lambda r, te, *_: (te[r], 0, 0))],
        out_specs=pl.BlockSpec((tm, d), lambda r, te, *_: (r, 0)),
        scratch_shapes=[pltpu.VMEM((2, d, dff2), F32), pltpu.VMEM((2, dff2 // 2, d), F32),
                        pltpu.VMEM((d, dff2), BF16), pltpu.VMEM((dff2 // 2, d), BF16),
                        pltpu.SemaphoreType.DMA((2, 2))],
    )
    return pl.pallas_call(
        _experts_body,
        grid_spec=grid_spec,
        out_shape=jax.ShapeDtypeStruct((p, d), F32),
        compiler_params=pltpu.CompilerParams(dimension_semantics=("arbitrary",),
                                             vmem_limit_bytes=VMEM_LIMIT),
        name="moe_experts",
    )(tile_expert, n_active, first, nxt, slot.astype(I32), xs, wgu, bgu, wd, bd)


def _combine_body(slow_ref, win_ref, pos_ref, col_ref, prob_ref, x1_ref, y_hbm, out_ref,
                  wbuf, rbuf, sems):
    i = pl.program_id(0)
    n = pl.num_programs(0) - 1
    tm = x1_ref.shape[0]
    slow = slow_ref[0] != 0
    nwin = wbuf.shape[1] // WIN_ROWS

    for slot in range(2):
        mine = i % 2 == slot

        @pl.when(mine & (i < n) & jnp.logical_not(slow))
        def _(slot=slot):
            for e in range(nwin):
                first = pl.multiple_of(win_ref[0, 0, e], SUBLANES)
                pltpu.make_async_copy(y_hbm.at[pl.ds(first, WIN_ROWS)],
                                      wbuf.at[slot, pl.ds(e * WIN_ROWS, WIN_ROWS)],
                                      sems.at[slot]).start()

        @pl.when(mine & (i > 0) & jnp.logical_not(slow))
        def _(slot=slot):
            _wait_rows(wbuf.at[1 - slot], sems.at[1 - slot], 1)
            col = col_ref[...]
            prob = prob_ref[...]
            lane = lax.broadcasted_iota(I32, (tm, LANES), 1)
            rel = [jnp.broadcast_to(col[:, k:k + 1], (tm, LANES)) - lane for k in range(TOP_K)]
            wgt = [jnp.broadcast_to(prob[:, k:k + 1], (tm, LANES)) for k in range(TOP_K)]
            pick = []
            for c in range(wbuf.shape[1] // LANES):
                g = jnp.zeros((tm, LANES), F32)
                for k in range(TOP_K):
                    g = jnp.where(rel[k] == c * LANES, wgt[k], g)
                pick.append(g.astype(BF16))
            ffn = _dot(jnp.concatenate(pick, axis=1), wbuf[1 - slot].astype(BF16))
            out_ref[...] = x1_ref[...] + ffn

        @pl.when(mine & (i < n) & slow)
        def _(slot=slot):
            def issue(t, carry):
                for k in range(TOP_K):
                    pltpu.make_async_copy(y_hbm.at[pl.ds(pos_ref[k, t], 1)],
                                          rbuf.at[slot, k, pl.ds(t, 1)],
                                          sems.at[slot]).start(priority=k % DMA_PRIORITIES)
                return carry
            lax.fori_loop(0, tm, issue, 0)

        @pl.when(mine & (i > 0) & slow)
        def _(slot=slot):
            _wait_rows(rbuf.at[1 - slot, 0], sems.at[1 - slot], TOP_K)
            prob = prob_ref[...]
            ffn = prob[:, 0:1] * rbuf[1 - slot, 0]
            for k in range(1, TOP_K):
                ffn = ffn + prob[:, k:k + 1] * rbuf[1 - slot, k]
            out_ref[...] = x1_ref[...] + ffn


def _combine(slow, win, pos, col, prob, x1, ys, tm):
    r, d = x1.shape
    n = r // tm
    prev = lambda w: pl.BlockSpec((tm, w), lambda i: (jnp.maximum(i - 1, 0), 0))
    cur = lambda i: jnp.minimum(i, n - 1)
    return pl.pallas_call(
        _combine_body,
        grid=(n + 1,),
        in_specs=[pl.BlockSpec(memory_space=pltpu.SMEM),
                  pl.BlockSpec((1, 1, LANES), lambda i: (cur(i), 0, 0), memory_space=pltpu.SMEM),
                  pl.BlockSpec((SUBLANES, tm), lambda i: (0, cur(i)), memory_space=pltpu.SMEM),
                  prev(LANES), prev(LANES), prev(d), pl.BlockSpec(memory_space=pl.ANY)],
        out_specs=prev(d),
        out_shape=jax.ShapeDtypeStruct((r, d), F32),
        scratch_shapes=[pltpu.VMEM((2, N_EXPERTS * WIN_ROWS, d), F32),
                        pltpu.VMEM((2, TOP_K, tm, d), F32), pltpu.SemaphoreType.DMA((2,))],
        compiler_params=pltpu.CompilerParams(dimension_semantics=("arbitrary",),
                                             vmem_limit_bytes=VMEM_LIMIT),
        name=f"moe_combine_{n}",
    )(slow, win, pos, col, prob, x1, ys)


def _moe(streams, wgu, bgu, wd, bd):
    tm = MOE_TILE
    counts = [s[4][0, :N_EXPERTS].astype(I32) for s in streams]
    cnt = sum(counts)
    padded = ((cnt + tm - 1) // tm) * tm
    ends = jnp.cumsum(padded)
    n_pairs = sum(s[0].shape[0] for s in streams) * TOP_K
    n_tiles = n_pairs // tm + N_EXPERTS
    tile_start = jnp.arange(n_tiles, dtype=I32) * tm
    tile_expert = jnp.minimum(jnp.sum((ends[None, :] <= tile_start[:, None]).astype(I32), axis=1),
                              N_EXPERTS - 1)
    n_active = (ends[-1:] // tm).astype(I32)
    start = ends - padded
    pad = jnp.stack([jnp.append(start + cnt, ends[-1]),
                     jnp.append(padded - cnt, n_tiles * tm - ends[-1])])
    routes = []
    for (x1, tok, eidx, prob, _), c in zip(streams, counts):
        cin = jnp.pad(start, (0, LANES - N_EXPERTS)).astype(F32)[None]
        routes.append(_rank(eidx, cin, _row_tile(eidx.shape[0], 256), n_tiles * tm - WIN_ROWS))
        start = start + c
    flat = [r[0][:TOP_K].T.reshape(-1) for r in routes]
    xs = _dispatch(flat, pad, [s[1] for s in streams], n_tiles * tm,
                   min(_row_tile(s[1].shape[0], 256) for s in streams))
    ys = _experts(tile_expert, n_active, xs, wgu, bgu, wd, bd)
    outs = []
    for s, (pos, col, win) in zip(streams, routes):
        slow = jnp.max(win[:, 0, FLAG_LANE]).reshape(1)
        outs.append(_combine(slow, win, pos, col, s[3], s[0], ys, _row_tile(s[0].shape[0], 256)))
    return outs


def _row_tile(r, want):
    tm = min(r, want)
    assert r % tm == 0, (r, tm)
    return tm


def _mixers(x2, bsz, t, pos, s0, caches, wts, layer):
    (g1, wm, ws, wg, qg, kg, lbp, ng) = wts
    r = x2.shape[0]
    tm = _row_tile(r, 256)
    assert t % tm == 0 or tm % t == 0
    tabs_a = _rope_tables(pos, HEAD_DIM, 1)
    tabs_b = _rope_tables(pos, IDX_DIM, LANES // IDX_DIM)
    tabs = tabs_a + tabs_b
    if tm > t:
        tabs = tuple(jnp.tile(tb, (tm // t, 1)) for tb in tabs)
    tk = max(tm, _row_tile(t, 512)) if caches is None else tm
    (aq, af, ai, ag, q, kf, kb, vf, vb, qim, kif, ki2, wi, ga, gb, vt) = _inproj(
        x2, g1, wm, ws, wg, qg, kg, tabs, tm, tk)
    oa, s_new = _hgrn(aq, af, ai, ag, s0, lbp, ng, bsz, t, layer)
    if caches is None:
        ob = _dsa_prompt(q, qim, wi, kb, vt, ki2, bsz, t, _row_tile(t, 256), tk)
    else:
        ck, cv, kidx2, past, cache_off = caches
        tk = _row_tile(past, 512)
        nsub = 2 if (past // tk) % 2 == 0 else 1
        ob = _dsa_sample(q, qim, wi, kb, vb, ki2, kidx2, ck, cv, bsz, t, past, tk, nsub, cache_off)
    return oa, ob, ga, gb, kf, vf, kif, s_new


def kernel(x_prompt, x_sample, cache_k, cache_v, cache_kidx, state_hgrn, norm1_g, w_in, lower_bounds, hgrn_norm_g, q_norm_g, k_norm_g, w_branch_a, w_branch_b, w_out, norm2_g, w_router, b_router, w_gate_up, b_gate_up, w_down, b_down):
    bp, tp, d = x_prompt.shape
    bs, ts, _ = x_sample.shape
    depth = w_in.shape[0]
    past = cache_k.shape[2]
    kvw = B_KV_HEADS * HEAD_DIM
    pos_p = jnp.arange(tp, dtype=I32)
    pos_s = past + jnp.arange(ts, dtype=I32)
    xp = x_prompt.reshape(bp * tp, d)
    xs = x_sample.reshape(bs * ts, d)
    n_main = 2 * A_HEADS * A_DK + 2 * A_HEADS * A_DV + B_HEADS * HEAD_DIM + 2 * kvw + IDX_HEADS * IDX_DIM
    n_small = n_main + IDX_DIM + IDX_HEADS
    outs = [[] for _ in range(8)]
    for l in range(depth):
        w = w_in[l]
        wm = w[:, :n_main].astype(BF16)
        w_ik = w[:, n_main:n_main + IDX_DIM]
        w_iw = w[:, n_main + IDX_DIM:n_small]
        ws = jnp.concatenate(
            [w_ik, w_ik, w_iw, jnp.zeros((d, LANES - IDX_HEADS), w.dtype)], axis=1).astype(BF16)
        wg = w[:, n_small:].astype(BF16)
        wts = (norm1_g[l][None], wm, ws, wg, q_norm_g[l][None], k_norm_g[l][None],
               lower_bounds, hgrn_norm_g[l][None])
        s0_p = jnp.zeros((bp, A_HEADS, A_DK, A_DV), F32)
        oa_p, ob_p, ga_p, gb_p, kp, vp, kip, sp = _mixers(xp, bp, tp, pos_p, s0_p, None, wts, l)
        kidx2 = jnp.concatenate([cache_kidx[l], cache_kidx[l]], axis=-1).astype(BF16)
        caches = (cache_k.reshape(-1, HEAD_DIM), cache_v.reshape(-1, HEAD_DIM),
                  kidx2.reshape(bs * past, LANES), past, l * bs)
        oa_s, ob_s, ga_s, gb_s, ks, vs, kis, ss = _mixers(
            xs, bs, ts, pos_s, state_hgrn[l], caches, wts, l)

        wa = w_branch_a[l].astype(BF16)
        wb = w_branch_b[l].astype(BF16)
        wo = w_out[l].astype(BF16)
        wr = jnp.pad(w_router[l], ((0, 0), (0, LANES - N_EXPERTS))).astype(BF16)
        br = jnp.pad(b_router[l], (0, LANES - N_EXPERTS))[None]
        g2 = norm2_g[l][None]
        bgu = b_gate_up[l][:, None, :]
        bd = b_down[l][:, None, :]
        streams = []
        for (x2, oa, ob, ga, gb) in ((xp, oa_p, ob_p, ga_p, gb_p), (xs, oa_s, ob_s, ga_s, gb_s)):
            r = x2.shape[0]
            streams.append(_post(oa, ob, ga, gb, x2, wa, wb, wo, g2, wr, br, _row_tile(r, 512)))
        xp, xs = _moe(streams, w_gate_up[l], bgu, w_down[l], bd)
        for lst, v in zip(outs, (kp.reshape(bp, tp, B_KV_HEADS, HEAD_DIM),
                                 vp.reshape(bp, tp, B_KV_HEADS, HEAD_DIM),
                                 kip.reshape(bp, tp, IDX_DIM), sp,
                                 ks.reshape(bs, ts, B_KV_HEADS, HEAD_DIM),
                                 vs.reshape(bs, ts, B_KV_HEADS, HEAD_DIM),
                                 kis.reshape(bs, ts, IDX_DIM), ss)):
            lst.append(v)
    return (xp.reshape(bp, tp, d), xs.reshape(bs, ts, d)) + tuple(jnp.stack(o) for o in outs)
```

```python
import functools

import jax
import jax.numpy as jnp
from jax import lax
from jax.experimental import pallas as pl
from jax.experimental.pallas import tpu as pltpu

F32 = jnp.float32
BF16 = jnp.bfloat16
I32 = jnp.int32

CHUNK = 64
A_HEADS = 8
A_DK = 128
A_DV = 128
B_HEADS = 8
B_KV_HEADS = 4
HEAD_DIM = 128
IDX_HEADS = 8
IDX_DIM = 64
IDX_TOPK_MAX = 256
ROPE_THETA = 10000.0
N_EXPERTS = 32
TOP_K = 4
SWIGLU_LIMIT = 7.0
SWIGLU_ALPHA = 1.702
EPS = 1e-6

LANES = 128
INT_MIN = -(2 ** 31)
NEG = -1e30
VMEM_LIMIT = 56 * 1024 * 1024


def _rms(x):
    return x * lax.rsqrt(jnp.mean(x * x, axis=-1, keepdims=True) + EPS)


def _silu(x):
    return x * jax.nn.sigmoid(x)


def _dot(a, b):
    return jnp.dot(a, b, preferred_element_type=F32)


def _dot_nt(a, b):
    return lax.dot_general(a, b, (((1,), (1,)), ((), ())), preferred_element_type=F32)


def _dot_tn(a, b):
    return lax.dot_general(a, b, (((0,), (0,)), ((), ())), preferred_element_type=F32)


def _const_spec(shape):
    zeros = (0,) * len(shape)
    return pl.BlockSpec(shape, lambda *_: zeros, pipeline_mode=pl.Buffered(1))


def _rope_tables(pos, d, reps):
    inv = 1.0 / (ROPE_THETA ** (jnp.arange(0, d, 2, dtype=F32) / d))
    ang = pos.astype(F32)[:, None] * inv[None, :]
    cos = jnp.cos(ang)
    sin = jnp.sin(ang)
    cos_t = jnp.concatenate([cos, cos] * reps, axis=-1)
    sin_t = jnp.concatenate([-sin, sin] * reps, axis=-1)
    return cos_t, sin_t


def _inproj_body(x_ref, g1_ref, wm_ref, ws_ref, wg_ref, qg_ref, kg_ref, ca_ref, sa_ref, cb_ref,
                 sb_ref, aq_o, af_o, ai_o, ag_o, q_o, kf_o, kb_o, vf_o, vb_o, qim_o, kif_o, ki2_o,
                 wi_o, ga_o, gb_o, vt_o):
    x = x_ref[...]
    tm = x.shape[0]
    hb = (_rms(x) * g1_ref[...]).astype(BF16)
    a_qk = A_HEADS * A_DK
    a_v = A_HEADS * A_DV
    b_q = B_HEADS * HEAD_DIM
    b_kv = B_KV_HEADS * HEAD_DIM
    iq_w = IDX_HEADS * IDX_DIM
    o = 0
    aq_o[...] = _dot(hb, wm_ref[:, o:o + a_qk]).astype(BF16)
    o += a_qk
    af_o[...] = _dot(hb, wm_ref[:, o:o + a_qk])
    o += a_qk
    ai_o[...] = _dot(hb, wm_ref[:, o:o + a_v]).astype(BF16)
    o += a_v
    ag_o[...] = _dot(hb, wm_ref[:, o:o + a_v]).astype(BF16)
    o += a_v

    ca = ca_ref[...]
    sa = sa_ref[...]

    def rope_head(y):
        return y * ca + pltpu.roll(y, HEAD_DIM // 2, 1) * sa

    zq = _dot(hb, wm_ref[:, o:o + b_q])
    o += b_q
    for h in range(B_HEADS):
        sl = slice(h * HEAD_DIM, (h + 1) * HEAD_DIM)
        q_o[:, sl] = (rope_head(_rms(zq[:, sl]) * qg_ref[...]) * Q_SCALE).astype(BF16)
    zk = _dot(hb, wm_ref[:, o:o + b_kv])
    o += b_kv
    for h in range(B_KV_HEADS):
        sl = slice(h * HEAD_DIM, (h + 1) * HEAD_DIM)
        y = rope_head(_rms(zk[:, sl]) * kg_ref[...])
        kf_o[pl.ds(h, tm, stride=B_KV_HEADS), :] = y
        kb_o[:, sl] = y.astype(BF16)
    zv = _dot(hb, wm_ref[:, o:o + b_kv])
    o += b_kv
    for h in range(B_KV_HEADS):
        vf_o[pl.ds(h, tm, stride=B_KV_HEADS), :] = zv[:, h * HEAD_DIM:(h + 1) * HEAD_DIM]
    vb_o[...] = zv.astype(BF16)
    vt_o[0] = zv.T.astype(BF16)

    cb = cb_ref[...]
    sb = sb_ref[...]
    lane = lax.broadcasted_iota(I32, (tm, LANES), 1)
    first_half = (lane & (IDX_DIM - 1)) < (IDX_DIM // 2)

    def rope_idx(y):
        partner = jnp.where(first_half, pltpu.roll(y, LANES - IDX_DIM // 2, 1),
                            pltpu.roll(y, IDX_DIM // 2, 1))
        return y * cb + partner * sb

    zi = _dot(hb, wm_ref[:, o:o + iq_w])
    for p in range(iq_w // LANES):
        y = rope_idx(zi[:, p * LANES:(p + 1) * LANES])
        qim_o[:, (2 * p) * LANES:(2 * p + 1) * LANES] = jnp.where(lane < IDX_DIM, y, 0.0).astype(BF16)
        qim_o[:, (2 * p + 1) * LANES:(2 * p + 2) * LANES] = jnp.where(lane >= IDX_DIM, y, 0.0).astype(BF16)
    zs = _dot(hb, ws_ref[...])
    y = rope_idx(zs[:, :LANES])
    ki2_o[...] = y.astype(BF16)
    kif_o[...] = y[:, :IDX_DIM]
    wi_o[...] = zs[:, LANES:] * (IDX_HEADS ** -0.5 * IDX_DIM ** -0.5)
    zg = _dot(hb, wg_ref[...])
    d = zg.shape[1] // 2
    ga_o[...] = zg[:, :d].astype(BF16)
    gb_o[...] = zg[:, d:].astype(BF16)


def _inproj(x2, g1, wm, ws, wg, qg, kg, tabs, tm, vt_tile):
    r, d = x2.shape
    ca, sa, cb, sb = tabs
    npos = ca.shape[0] // tm
    per = vt_tile // tm
    row = lambda w: pl.BlockSpec((tm, w), lambda i: (i, 0))
    tab = pl.BlockSpec((tm, LANES), lambda i: (i % npos, 0))
    a_qk = A_HEADS * A_DK
    b_q = B_HEADS * HEAD_DIM
    b_kv = B_KV_HEADS * HEAD_DIM
    kvh = B_KV_HEADS
    outs = [
        (1, a_qk, BF16), (1, a_qk, F32), (1, a_qk, BF16), (1, a_qk, BF16),
        (1, b_q, BF16), (kvh, HEAD_DIM, F32), (1, b_kv, BF16),
        (kvh, HEAD_DIM, F32), (1, b_kv, BF16),
        (1, IDX_HEADS * LANES, BF16), (1, IDX_DIM, F32), (1, LANES, BF16),
        (1, LANES, F32), (1, d, BF16), (1, d, BF16),
    ]
    return pl.pallas_call(
        _inproj_body,
        grid=(r // tm,),
        in_specs=[row(d), _const_spec(g1.shape), _const_spec(wm.shape), _const_spec(ws.shape),
                  _const_spec(wg.shape), _const_spec(qg.shape), _const_spec(kg.shape),
                  tab, tab, tab, tab],
        out_specs=[pl.BlockSpec((tm * k, w), lambda i: (i, 0)) for k, w, _ in outs] + [
            pl.BlockSpec((1, b_kv, tm), lambda i: (i // per, 0, i % per))],
        out_shape=[jax.ShapeDtypeStruct((r * k, w), dt) for k, w, dt in outs] + [
            jax.ShapeDtypeStruct((r // vt_tile, b_kv, vt_tile), BF16)],
        compiler_params=pltpu.CompilerParams(dimension_semantics=("arbitrary",),
                                             vmem_limit_bytes=VMEM_LIMIT),
        name=f"inproj_{r // tm}",
    )(x2, g1, wm, ws, wg, qg, kg, ca, sa, cb, sb)


HGRN_BATCH = 4


def _hgrn_body(aq_ref, af_ref, ai_ref, ag_ref, s0_ref, lbp_ref, ng_ref, oa_ref, sn_ref, st_ref,
               *, layer):
    c = pl.program_id(1)
    nc = pl.num_programs(1)
    nb = aq_ref.shape[0]

    @pl.when(c == 0)
    def _():
        for b in range(nb):
            for h in range(A_HEADS):
                st_ref[b, h] = s0_ref[b, h].T

    lbp = lbp_ref[...]
    e = jnp.exp(lbp - jnp.max(lbp, axis=0, keepdims=True))
    sm = e / jnp.sum(e, axis=0, keepdims=True)
    lb = jnp.sum(sm[:layer + 1], axis=0, keepdims=True)

    n = aq_ref.shape[1]
    r_i = lax.broadcasted_iota(I32, (n, n), 0)
    c_i = lax.broadcasted_iota(I32, (n, n), 1)
    causal = r_i >= c_i
    tri = jnp.where(causal, 1.0, 0.0).astype(BF16)
    prep = []
    for b in range(nb):
        f = lb + (1.0 - lb) * jax.nn.sigmoid(af_ref[b])
        lf = jnp.log(f)
        hi = lf.astype(BF16)
        r1 = lf - hi.astype(F32)
        mid = r1.astype(BF16)
        lo = (r1 - mid.astype(F32)).astype(BF16)
        cum = _dot(tri, hi) + _dot(tri, mid) + _dot(tri, lo)
        last = cum[n - 1:n, :]
        qd = _silu(aq_ref[b].astype(F32)) * jnp.exp(cum)
        k = 1.0 - f
        prep.append((qd, k * jnp.exp(-cum), k * jnp.exp(last - cum), jnp.exp(last),
                     _silu(ag_ref[b].astype(F32))))
    for h in range(A_HEADS):
        sl = slice(h * A_DK, (h + 1) * A_DK)
        for b, (qd, kd, kt, el, gate) in enumerate(prep):
            qd_h = qd[:, sl].astype(BF16)
            v_h = ai_ref[b, :, sl]
            att = jnp.where(causal, _dot_nt(qd_h, kd[:, sl].astype(BF16)), 0.0)
            st = st_ref[b, h]
            o = _dot(att.astype(BF16), v_h) + _dot_nt(qd_h, st.astype(BF16))
            st_ref[b, h] = st * el[:, sl] + _dot_tn(v_h, kt[:, sl].astype(BF16))
            oa_ref[b, :, sl] = (_rms(o) * ng_ref[...] * gate[:, sl]).astype(BF16)

    @pl.when(c == nc - 1)
    def _():
        for b in range(nb):
            for h in range(A_HEADS):
                sn_ref[b, h] = st_ref[b, h].T


def _hgrn(aq, af, ai, ag, s0, lbp, ng, bsz, t, layer):
    d = aq.shape[1]
    nc = t // CHUNK
    nb = _row_tile(bsz, HGRN_BATCH)
    seq = lambda a: a.reshape(bsz, t, d)
    row = pl.BlockSpec((nb, CHUNK, d), lambda g, c: (g, c, 0))
    st = pl.BlockSpec((nb, A_HEADS, A_DK, A_DV), lambda g, c: (g, 0, 0, 0))
    oa, s_new = pl.pallas_call(
        functools.partial(_hgrn_body, layer=layer),
        grid=(bsz // nb, nc),
        in_specs=[row, row, row, row, st, _const_spec(lbp.shape), _const_spec(ng.shape)],
        out_specs=[row, st],
        out_shape=[jax.ShapeDtypeStruct((bsz, t, d), BF16),
                   jax.ShapeDtypeStruct((bsz, A_HEADS, A_DK, A_DV), F32)],
        scratch_shapes=[pltpu.VMEM((nb, A_HEADS, A_DV, A_DK), F32)],
        compiler_params=pltpu.CompilerParams(dimension_semantics=("arbitrary", "arbitrary"),
                                             vmem_limit_bytes=VMEM_LIMIT),
        name=f"hgrn_{bsz // nb}x{nc}",
    )(seq(aq), seq(af), seq(ai), seq(ag), s0, lbp, ng)
    return oa.reshape(bsz * t, d), s_new


def _chunk_of(pos):
    return jnp.right_shift(pos, CHUNK.bit_length() - 1)


def _transpose_wi(wi, tq):
    if tq % LANES:
        wi = jnp.concatenate([wi, jnp.zeros((LANES - tq % LANES, LANES), F32)], axis=0)
    return wi.T[:IDX_HEADS, :tq]


def _score_tile(ki2_tile, qim_ref, wt):
    acc = None
    for h in range(IDX_HEADS):
        s = _dot_nt(ki2_tile, qim_ref[:, h * LANES:(h + 1) * LANES])
        term = jnp.maximum(s, 0.0) * wt[h:h + 1, :]
        acc = term if acc is None else acc + term
    return acc


def _mask_scores(score, adm):
    score = jnp.where(score == 0.0, 0.0, score)
    return score if adm is None else jnp.where(adm, score, -jnp.inf)


def _sort_key(score):
    bits = lax.bitcast_convert_type(score, I32)
    return jnp.where(bits < 0, bits ^ 0x7FFFFFFF, bits)


def _key_to_score(key):
    return lax.bitcast_convert_type(jnp.where(key < 0, key ^ 0x7FFFFFFF, key), F32)


SUBLANES = 8
COUNT_ACCS = 4
KEY_BITS = 32
PLANE_KEYS = SUBLANES * KEY_BITS


def _store_bit_planes(planes_ref, key, blk0):
    u = key ^ INT_MIN
    tk = u.shape[0]
    for blk in range(tk // PLANE_KEYS):
        a = [u[blk * PLANE_KEYS + i * SUBLANES:blk * PLANE_KEYS + (i + 1) * SUBLANES, :]
             for i in range(KEY_BITS)]
        for j, m in ((16, 0x0000FFFF), (8, 0x00FF00FF), (4, 0x0F0F0F0F), (2, 0x33333333),
                     (1, 0x55555555)):
            k = 0
            while k < KEY_BITS:
                t = (a[k] ^ lax.shift_right_logical(a[k + j], j)) & m
                a[k] = a[k] ^ t
                a[k + j] = a[k + j] ^ lax.shift_left(t, j)
                k = (k + j + 1) & ~j
        row = pl.multiple_of((blk0 + blk) * SUBLANES, SUBLANES)
        for p in range(KEY_BITS):
            planes_ref[p, pl.ds(row, SUBLANES), :] = a[p]


def _select(sc_ref, planes_ref, bias_ref, cut_ref, stat_ref, nkt, krow, tk, tq, nbits):
    nrow = planes_ref.shape[1]
    nblk = nkt * (tk // PLANE_KEYS)
    blk_of_row = jnp.right_shift(lax.broadcasted_iota(I32, (nrow, tq), 0),
                                 SUBLANES.bit_length() - 1)

    def word_count(words):
        pc = lax.population_count(words).reshape(nrow // SUBLANES, SUBLANES, tq)
        return jnp.sum(jnp.sum(pc, axis=0).astype(F32), axis=0, keepdims=True)

    def bit_body(i, carry):
        alive, above, thr_u = carry
        ones = alive & planes_ref[i]
        c1 = word_count(ones)
        take = above + c1 >= krow
        alive = jnp.where(take, ones, alive ^ ones)
        above = jnp.where(take, above, above + c1)
        thr_u = jnp.where(take, thr_u | jnp.left_shift(jnp.int32(1), KEY_BITS - 1 - i), thr_u)
        return alive, above, thr_u

    alive0 = jnp.where(blk_of_row < nblk, -1, 0).astype(I32)
    _, _, thr_u = lax.fori_loop(
        0, KEY_BITS, bit_body, (alive0, jnp.zeros((1, tq), F32), jnp.zeros((1, tq), I32)),
        unroll=True)

    nchunk = tk // SUBLANES
    sub = lax.broadcasted_iota(I32, (SUBLANES, tq), 0)

    def count(pred_fn):
        def body(j, accs):
            t = sc_ref[j]
            accs = list(accs)
            for c in range(nchunk):
                a = c % COUNT_ACCS
                accs[a] = pred_fn(t[c * SUBLANES:(c + 1) * SUBLANES, :], j, c, accs[a])
            return tuple(accs)
        zero = jnp.zeros((SUBLANES, tq), F32)
        accs = lax.fori_loop(0, nkt, body, (zero,) * COUNT_ACCS)
        tot = accs[0]
        for a in accs[1:]:
            tot = tot + a
        return jnp.sum(tot, axis=0, keepdims=True)

    def count_ge(cand):
        c8 = jnp.broadcast_to(cand, (SUBLANES, tq))
        return count(lambda t, j, c, acc: jnp.where(t >= c8, acc + 1.0, acc))

    def count_gt(cand):
        c8 = jnp.broadcast_to(cand, (SUBLANES, tq))
        return count(lambda t, j, c, acc: jnp.where(t > c8, acc + 1.0, acc))

    def record(thr):
        stat_ref[0:1, :] = thr
        stat_ref[1:2, :] = count_ge(thr)
        stat_ref[2:3, :] = count_gt(thr)

    record(_key_to_score(thr_u ^ INT_MIN))
    good = (stat_ref[2:3, :] < krow) & (stat_ref[1:2, :] >= krow)

    @pl.when(jnp.max(jnp.where(good, 0.0, 1.0)) > 0.0)
    def _():
        def cmp_body(i, tu):
            cand_u = tu | jnp.left_shift(jnp.int32(1), KEY_BITS - 1 - i)
            return jnp.where(count_ge(_key_to_score(cand_u ^ INT_MIN)) >= krow, cand_u, tu)
        tu = lax.fori_loop(0, KEY_BITS, cmp_body, jnp.zeros((1, tq), I32))
        record(_key_to_score(tu ^ INT_MIN))

    thr = stat_ref[0:1, :]
    n_ge = stat_ref[1:2, :]
    need = krow - stat_ref[2:3, :]

    cut_ref[...] = jnp.full(cut_ref.shape, 2 ** 31 - 1, I32)

    @pl.when(jnp.max(jnp.where(n_ge > krow, 1.0, 0.0)) > 0.0)
    def _():
        thr8 = jnp.broadcast_to(thr, (SUBLANES, tq))

        def idx_body(i, cut):
            cand = cut | jnp.left_shift(jnp.int32(1), nbits - 1 - i)
            c8 = jnp.broadcast_to(cand, (SUBLANES, tq))
            below = count(lambda t, j, c, acc: jnp.where(
                t == thr8, jnp.where(j * tk + c * SUBLANES + sub < c8, acc + 1.0, acc), acc))
            return jnp.where(below < need, cand, cut)
        cut_ref[...] = lax.fori_loop(0, nbits, idx_body, jnp.zeros((1, tq), I32))

    cut = cut_ref[...]

    def write(j, carry):
        t = sc_ref[j]
        idx = j * tk + lax.broadcasted_iota(I32, (tk, tq), 0)
        tie = jnp.where(t == thr, jnp.where(idx <= cut, 0.0, NEG), NEG)
        bias = jnp.where(t > thr, 0.0, tie)
        if callable(bias_ref):
            bias_ref(j, bias)
        else:
            bias_ref[j] = bias
        return carry

    lax.fori_loop(0, nkt, write, 0)


def _stack_heads(q_ref, u, stack):
    heads = [q_ref[:, h * HEAD_DIM:(h + 1) * HEAD_DIM] for h in range(u * stack, (u + 1) * stack)]
    return heads[0] if stack == 1 else jnp.concatenate(heads, axis=0)


def _store_heads(o_ref, u, stack, o, tq):
    for r in range(stack):
        h = u * stack + r
        o_ref[:, h * HEAD_DIM:(h + 1) * HEAD_DIM] = o[r * tq:(r + 1) * tq].astype(o_ref.dtype)


Q_SCALE = HEAD_DIM ** -0.5 * 1.4426950408889634
SUM_ROWS = 16


def _attn_reset(m_ref, l_ref, acc_ref):
    m_ref[...] = jnp.full(m_ref.shape, NEG, F32)
    l_ref[...] = jnp.zeros(l_ref.shape, F32)
    acc_ref[...] = jnp.zeros(acc_ref.shape, F32)


def _attn_logits(q_ref, u, stack, kt, bias, m_ref, state=None):
    state = u if state is None else state
    qg = _stack_heads(q_ref, u, stack)
    if bias.shape[1] != qg.shape[0]:
        bias = jnp.concatenate([bias] * stack, axis=1)
    s = _dot_nt(kt, qg) + bias
    return s, jnp.maximum(m_ref[state], jnp.max(s, axis=0, keepdims=True))


def _attn_accumulate(u, s, m_new, m_ref, l_ref, acc_ref, pv):
    alpha = jnp.exp2(m_ref[u] - m_new)
    p = jnp.exp2(s - m_new)
    if l_ref is not None:
        l_ref[u] = alpha * l_ref[u] + jnp.sum(p, axis=0, keepdims=True)
    acc_ref[u] = alpha * acc_ref[u] + pv(p.astype(BF16))
    m_ref[u] = m_new


def _attn_step(q_ref, u, stack, kt, bias, m_ref, l_ref, acc_ref, pv, state=None):
    state = u if state is None else state
    s, m_new = _attn_logits(q_ref, u, stack, kt, bias, m_ref, state)
    _attn_accumulate(state, s, m_new, m_ref, l_ref, acc_ref, pv)


def _attn_finish(o_ref, u, stack, l_ref, acc_ref, tq, state=None):
    state = u if state is None else state
    acc = acc_ref[state]
    l = acc[HEAD_DIM:HEAD_DIM + 1] if l_ref is None else l_ref[state]
    _store_heads(o_ref, u, stack, (acc[:HEAD_DIM] / l).T, tq)


def _dsa_prompt_body(q_ref, qim_ref, wi_ref, kb_ref, vt_ref, ki2_ref, o_ref, sc_ref, planes_ref,
                     bias_ref, cut_ref, stat_ref, m_ref, l_ref, acc_ref,
                     *, tq, tk, topk, nbits):
    q0 = pl.program_id(1) * tq
    nkt = lax.div(q0 + tq + tk - 1, tk)
    wt = _transpose_wi(wi_ref[...], tq)
    qchunk = _chunk_of(q0 + lax.broadcasted_iota(I32, (1, tq), 1))

    @pl.when((pl.program_id(0) == 0) & (pl.program_id(1) == 0))
    def _():
        planes_ref[...] = jnp.zeros(planes_ref.shape, I32)

    def score_body(j, carry):
        ks = pl.multiple_of(j * tk, tk)
        score = _score_tile(ki2_ref[pl.ds(ks, tk), :], qim_ref, wt)
        kpos = ks + lax.broadcasted_iota(I32, (tk, tq), 0)
        score = _mask_scores(score, _chunk_of(kpos) <= qchunk)
        sc_ref[j] = score
        _store_bit_planes(planes_ref, _sort_key(score), j * (tk // PLANE_KEYS))
        return carry

    lax.fori_loop(0, nkt, score_body, 0)

    krow = jnp.minimum((qchunk + 1) * CHUNK, topk).astype(F32)
    _select(sc_ref, planes_ref, bias_ref, cut_ref, stat_ref, nkt, krow, tk, tq, nbits)

    _attn_reset(m_ref, l_ref, acc_ref)

    rep = B_HEADS // B_KV_HEADS
    ones = jnp.ones((SUM_ROWS, tk), BF16)

    def att_body(j, carry):
        ks = pl.multiple_of(j * tk, tk)
        bias = bias_ref[j]
        gsl = lambda g: slice(g * HEAD_DIM, (g + 1) * HEAD_DIM)
        logits = lambda g: _attn_logits(q_ref, g, rep, kb_ref[pl.ds(ks, tk), gsl(g)], bias, m_ref)
        pending = [logits(0), logits(1)]
        for g in range(B_KV_HEADS):
            if g + 2 < B_KV_HEADS:
                pending.append(logits(g + 2))
            vt = jnp.concatenate([vt_ref[j, gsl(g), :], ones], axis=0)
            _attn_accumulate(g, *pending[g], m_ref, None, acc_ref, lambda p, vt=vt: _dot(vt, p))
        return carry

    lax.fori_loop(0, nkt, att_body, 0)
    for g in range(B_KV_HEADS):
        _attn_finish(o_ref, g, rep, None, acc_ref, tq)


def _dsa_scratch(nt, tk, tq, stack, acc_rows):
    units = B_HEADS // stack
    return [pltpu.VMEM((nt, tk, tq), F32),
            pltpu.VMEM((KEY_BITS, nt * tk // KEY_BITS, tq), I32),
            pltpu.VMEM((nt, tk, tq), F32),
            pltpu.VMEM((1, tq), I32),
            pltpu.VMEM((SUBLANES, tq), F32),
            pltpu.VMEM((units, 1, stack * tq), F32),
            pltpu.VMEM((units, 1, stack * tq), F32),
            pltpu.VMEM((units, acc_rows, stack * tq), F32)]


def _dsa_prompt(q, qim, wi, kb, vt, ki2, bsz, t, tq, tk):
    nq = t // tq
    nt = t // tk
    topk = min(IDX_TOPK_MAX, t // 4)
    nbits = max(1, (t - 1).bit_length())
    qrow = lambda w: pl.BlockSpec((tq, w), lambda b, i: (b * nq + i, 0))
    seq = lambda w: pl.BlockSpec((t, w), lambda b, i: (b, 0))
    return pl.pallas_call(
        functools.partial(_dsa_prompt_body, tq=tq, tk=tk, topk=topk, nbits=nbits),
        grid=(bsz, nq),
        in_specs=[qrow(q.shape[1]), qrow(qim.shape[1]), qrow(LANES), seq(kb.shape[1]),
                  pl.BlockSpec((nt, vt.shape[1], tk), lambda b, i: (b, 0, 0)), seq(LANES)],
        out_specs=qrow(q.shape[1]),
        out_shape=jax.ShapeDtypeStruct(q.shape, BF16),
        scratch_shapes=_dsa_scratch(nt, tk, tq, B_HEADS // B_KV_HEADS, HEAD_DIM + SUM_ROWS),
        compiler_params=pltpu.CompilerParams(dimension_semantics=("arbitrary", "arbitrary"),
                                             vmem_limit_bytes=VMEM_LIMIT),
        name="dsa_prompt",
    )(q, qim, wi, kb, vt, ki2)


def _dsa_sample_body(q_ref, qim_ref, wi_ref, kn_ref, vn_ref, ki2n_ref, kidx2_ref, *refs,
                     t, tk, nsub, npt, past, topk, nbits, nbat):
    caches, refs = refs[:2 * nbat], refs[2 * nbat:]
    o_ref, sc_ref, planes_ref, bias_ref, cut_ref, stat_ref, m_ref, l_ref, acc_ref = refs
    j = pl.program_id(1)
    nj = pl.num_programs(1)
    tq = nbat * t
    rep = B_HEADS // B_KV_HEADS
    pad = jnp.zeros((tk - t, LANES), BF16)
    nb = tk // PLANE_KEYS
    rows_of = lambda b: pl.ds(b * t, t)

    @pl.when(j == 0)
    def _():
        wt = _transpose_wi(wi_ref[...], tq)
        row = lax.broadcasted_iota(I32, (tq, 1), 0)
        qims = [jnp.where((row >= b * t) & (row < (b + 1) * t), qim_ref[...].astype(F32),
                          0.0).astype(BF16) for b in range(nbat)]

        def scores(tile_of):
            total = None
            for b in range(nbat):
                s = _score_tile(tile_of(b), qims[b], wt)
                total = s if total is None else total + s
            return total

        def score_body(jt, carry):
            ks = pl.multiple_of(jt * tk, tk)
            score = _mask_scores(scores(lambda b: kidx2_ref[pl.ds(b * past + ks, tk), :]), None)
            sc_ref[jt] = score
            _store_bit_planes(planes_ref, _sort_key(score), jt * nb)
            return carry

        lax.fori_loop(0, npt, score_body, 0)
        krow_i = lax.broadcasted_iota(I32, (tk, tq), 0)
        qchunk = _chunk_of(past + (lax.broadcasted_iota(I32, (1, tq), 1) & (t - 1)))
        adm = jnp.where(krow_i < t, _chunk_of(past + krow_i), 2 ** 30) <= qchunk
        score = _mask_scores(
            scores(lambda b: jnp.concatenate([ki2n_ref[rows_of(b), :], pad], axis=0)), adm)
        sc_ref[npt] = score
        _store_bit_planes(planes_ref, _sort_key(score), npt * nb)
        krow = jnp.minimum((qchunk + 1) * CHUNK, topk).astype(F32)

        def write_bias(jt, bias):
            for b in range(nbat):
                bias_ref[b, jt] = jnp.concatenate([bias[:, b * t:(b + 1) * t]] * rep, axis=1)

        _select(sc_ref, planes_ref, write_bias, cut_ref, stat_ref, npt + 1, krow, tk, tq, nbits)
        _attn_reset(m_ref, l_ref, acc_ref)

    units = [(b, g) for b in range(nbat) for g in range(B_KV_HEADS)]
    state = lambda b, g: b * B_KV_HEADS + g

    for su in range(nsub):
        rows = lambda g: pl.ds(su * tk * B_KV_HEADS + g, tk, stride=B_KV_HEADS)
        pending = [_attn_logits(q_ref.at[rows_of(b)], g, rep,
                                caches[2 * b][rows(g), :].astype(BF16),
                                bias_ref[b, j * nsub + su], m_ref, state(b, g))
                   for b, g in units]
        for (b, g), (s, m_new) in zip(units, pending):
            vt = caches[2 * b + 1][rows(g), :].astype(BF16)
            _attn_accumulate(state(b, g), s, m_new, m_ref, l_ref, acc_ref,
                             lambda p, vt=vt: _dot_tn(vt, p))

    @pl.when(j == nj - 1)
    def _():
        for b, g in units:
            gs = slice(g * HEAD_DIM, (g + 1) * HEAD_DIM)
            kt = jnp.concatenate([kn_ref[rows_of(b), gs], pad], axis=0)
            vt = jnp.concatenate([vn_ref[rows_of(b), gs], pad], axis=0)
            _attn_step(q_ref.at[rows_of(b)], g, rep, kt, bias_ref[b, npt], m_ref, l_ref, acc_ref,
                       lambda p, vt=vt: _dot_tn(vt, p), state(b, g))
            _attn_finish(o_ref.at[rows_of(b)], g, rep, l_ref, acc_ref, t, state(b, g))


def _dsa_sample(q, qim, wi, kn, vn, ki2n, kidx2, ck, cv, bsz, t, past, tk, nsub, cache_off):
    fits = t < LANES and LANES % t == 0 and t & (t - 1) == 0 and bsz % (LANES // t) == 0
    nbat = LANES // t if fits else 1
    tq = nbat * t
    npt = past // tk
    nj = npt // nsub
    nt = npt + 1
    topk = min(IDX_TOPK_MAX, (past + t) // 4)
    nbits = max(1, (past + t - 1).bit_length())
    rep = B_HEADS // B_KV_HEADS
    units = nbat * B_KV_HEADS
    qrow = lambda w: pl.BlockSpec((tq, w), lambda g, j: (g, 0))
    kvw = kn.shape[1]
    cache = lambda b: pl.BlockSpec((nsub * tk * B_KV_HEADS, HEAD_DIM),
                                   lambda g, j: ((cache_off + g * nbat + b) * nj + j, 0))
    caches = [spec for b in range(nbat) for spec in (cache(b), cache(b))]
    scratch = [pltpu.VMEM((nt, tk, tq), F32),
               pltpu.VMEM((KEY_BITS, nt * tk // KEY_BITS, tq), I32),
               pltpu.VMEM((nbat, nt, tk, rep * t), F32),
               pltpu.VMEM((1, tq), I32),
               pltpu.VMEM((SUBLANES, tq), F32),
               pltpu.VMEM((units, 1, rep * t), F32),
               pltpu.VMEM((units, 1, rep * t), F32),
               pltpu.VMEM((units, HEAD_DIM, rep * t), F32)]
    return pl.pallas_call(
        functools.partial(_dsa_sample_body, t=t, tk=tk, nsub=nsub, npt=npt, past=past,
                          topk=topk, nbits=nbits, nbat=nbat),
        grid=(bsz // nbat, nj),
        in_specs=[qrow(q.shape[1]), qrow(qim.shape[1]), qrow(LANES), qrow(kvw), qrow(kvw),
                  qrow(LANES), pl.BlockSpec((nbat * past, LANES), lambda g, j: (g, 0))] + caches,
        out_specs=qrow(q.shape[1]),
        out_shape=jax.ShapeDtypeStruct(q.shape, BF16),
        scratch_shapes=scratch,
        compiler_params=pltpu.CompilerParams(dimension_semantics=("arbitrary", "arbitrary"),
                                             vmem_limit_bytes=VMEM_LIMIT),
        name="dsa_sample",
    )(q, qim, wi, kn, vn, ki2n, kidx2, *([ck, cv] * nbat))


def _post_body(oa_ref, ob_ref, ga_ref, gb_ref, x_ref, wa_ref, wb_ref, wo_ref, g2_ref, wr_ref,
               br_ref, x1_o, t_o, eidx_o, prob_o, cnt_o):
    ya = _dot(oa_ref[...], wa_ref[...])
    yb = _dot(ob_ref[...], wb_ref[...])
    merged = (jax.nn.sigmoid(ga_ref[...].astype(F32)) * ya
              + jax.nn.sigmoid(gb_ref[...].astype(F32)) * yb)
    x1 = x_ref[...] + _dot(merged.astype(BF16), wo_ref[...])
    x1_o[...] = x1
    tok = _rms(x1) * g2_ref[...]
    t_o[...] = tok
    logits = _dot(tok.astype(BF16), wr_ref[...]) + br_ref[...]
    tm = logits.shape[0]
    lane = lax.broadcasted_iota(I32, (tm, LANES), 1).astype(F32)
    cur = jnp.where(lane < N_EXPERTS, logits, -jnp.inf)
    top = None
    den = jnp.zeros((tm, 1), F32)
    eidx = jnp.zeros((tm, LANES), F32)
    prob = jnp.zeros((tm, LANES), F32)
    chosen = jnp.zeros((tm, LANES), F32)
    for k in range(TOP_K):
        mx = jnp.max(cur, axis=1, keepdims=True)
        first = jnp.min(jnp.where(cur == mx, lane, float(LANES)), axis=1, keepdims=True)
        if top is None:
            top = mx
        e = jnp.exp(mx - top)
        den = den + e
        eidx = jnp.where(lane == k, first, eidx)
        prob = jnp.where(lane == k, e, prob)
        hit = lane == first
        chosen = jnp.where(hit, 1.0, chosen)
        cur = jnp.where(hit, -jnp.inf, cur)
    eidx_o[...] = eidx.astype(I32)
    prob_o[...] = prob / den

    @pl.when(pl.program_id(0) == 0)
    def _():
        cnt_o[...] = jnp.zeros(cnt_o.shape, F32)

    cnt_o[...] += jnp.sum(chosen, axis=0, keepdims=True)


def _post(oa, ob, ga, gb, x2, wa, wb, wo, g2, wr, br, tm):
    r, d = x2.shape
    row = lambda w: pl.BlockSpec((tm, w), lambda i: (i, 0))
    return pl.pallas_call(
        _post_body,
        grid=(r // tm,),
        in_specs=[row(d), row(d), row(d), row(d), row(d), _const_spec(wa.shape),
                  _const_spec(wb.shape), _const_spec(wo.shape), _const_spec(g2.shape),
                  _const_spec(wr.shape), _const_spec(br.shape)],
        out_specs=[row(d), row(d), row(LANES), row(LANES),
                   pl.BlockSpec((1, LANES), lambda i: (0, 0))],
        out_shape=[jax.ShapeDtypeStruct((r, d), F32), jax.ShapeDtypeStruct((r, d), F32),
                   jax.ShapeDtypeStruct((r, LANES), I32), jax.ShapeDtypeStruct((r, LANES), F32),
                   jax.ShapeDtypeStruct((1, LANES), F32)],
        compiler_params=pltpu.CompilerParams(dimension_semantics=("arbitrary",),
                                             vmem_limit_bytes=VMEM_LIMIT),
        name=f"post_{r // tm}",
    )(oa, ob, ga, gb, x2, wa, wb, wo, g2, wr, br)


MOE_TILE = 512
DMA_PRIORITIES = 2
ISSUE_UNROLL = 8


WIN_ROWS = 64
FLAG_LANE = LANES - 1


def _rank_body(eidx_ref, cin_ref, pos_o, col_o, win_o, carry_ref, *, max_start):
    @pl.when(pl.program_id(0) == 0)
    def _():
        carry_ref[...] = cin_ref[...]

    eidx = eidx_ref[...]
    tm = eidx.shape[0]
    lane = lax.broadcasted_iota(I32, (tm, LANES), 1)
    hits = [lane == eidx[:, k:k + 1] for k in range(TOP_K)]
    onehot = jnp.zeros((tm, LANES), F32)
    for hit in hits:
        onehot = onehot + jnp.where(hit, 1.0, 0.0)
    r_i = lax.broadcasted_iota(I32, (tm, tm), 0)
    c_i = lax.broadcasted_iota(I32, (tm, tm), 1)
    before = jnp.where(c_i < r_i, 1.0, 0.0).astype(BF16)
    carry = carry_ref[...]
    start = jnp.minimum((carry.astype(I32) >> 3) << 3, max_start)
    base = carry + _dot(before, onehot.astype(BF16))
    rank = jnp.zeros((tm, LANES), F32)
    col = jnp.zeros((tm, LANES), F32)
    worst = jnp.zeros((tm, 1), F32)
    for k, hit in enumerate(hits):
        rk = jnp.sum(jnp.where(hit, base, 0.0), axis=1, keepdims=True)
        off = jnp.sum(jnp.where(hit, base - start.astype(F32), 0.0), axis=1, keepdims=True)
        rank = jnp.where(lane == k, rk, rank)
        col = jnp.where(lane == k, eidx[:, k:k + 1].astype(F32) * WIN_ROWS + off, col)
        worst = jnp.maximum(worst, off)
    pos_o[...] = rank.T[:SUBLANES, :].astype(I32)
    col_o[...] = col.astype(I32)
    flag = (jnp.max(worst) >= WIN_ROWS).astype(I32)
    win_o[...] = jnp.where(lane[:1] == FLAG_LANE, flag, start).reshape(win_o.shape)
    carry_ref[...] += jnp.sum(onehot, axis=0, keepdims=True)


def _rank(eidx, cin, tm, max_start):
    r = eidx.shape[0]
    return pl.pallas_call(
        functools.partial(_rank_body, max_start=max_start),
        grid=(r // tm,),
        in_specs=[pl.BlockSpec((tm, LANES), lambda i: (i, 0)), _const_spec(cin.shape)],
        out_specs=[pl.BlockSpec((SUBLANES, tm), lambda i: (0, i)),
                   pl.BlockSpec((tm, LANES), lambda i: (i, 0)),
                   pl.BlockSpec((1, 1, LANES), lambda i: (i, 0, 0))],
        out_shape=[jax.ShapeDtypeStruct((SUBLANES, r), I32),
                   jax.ShapeDtypeStruct((r, LANES), I32),
                   jax.ShapeDtypeStruct((r // tm, 1, LANES), I32)],
        scratch_shapes=[pltpu.VMEM((1, LANES), F32)],
        compiler_params=pltpu.CompilerParams(dimension_semantics=("arbitrary",)),
        name=f"moe_rank_{r // tm}",
    )(eidx, cin)


def _wait_rows(ref, sem, times):
    for _ in range(times):
        pltpu.make_async_copy(ref, ref, sem).wait()


def _zero_pad_rows(pad_ref, xs_out, zeros, sem):
    zeros[...] = jnp.zeros(zeros.shape, zeros.dtype)
    row = zeros.at[pl.ds(0, 1)]
    rows8 = zeros.at[pl.ds(0, SUBLANES)]
    tile = zeros.shape[0]

    def for_each_copy(fn):
        def per_group(g, carry):
            first, n = pad_ref[0, g], pad_ref[1, g]
            head = jnp.minimum(n, (SUBLANES - (first & (SUBLANES - 1))) & (SUBLANES - 1))

            def per_row(t, c):
                fn(pltpu.make_async_copy(row, xs_out.at[pl.ds(first + t, 1)], sem))
                return c
            lax.fori_loop(0, head, per_row, 0)

            def per_block(t, c):
                at = pl.multiple_of(first + head + t * SUBLANES, SUBLANES)
                fn(pltpu.make_async_copy(rows8, xs_out.at[pl.ds(at, SUBLANES)], sem))
                return c
            lax.fori_loop(0, lax.div(n - head, SUBLANES), per_block, 0)
            return carry
        lax.fori_loop(0, N_EXPERTS, per_group, 0)
        first = pad_ref[0, N_EXPERTS]

        def per_tile(t, c):
            fn(pltpu.make_async_copy(
                zeros, xs_out.at[pl.ds(pl.multiple_of(first + t * tile, tile), tile)], sem))
            return c
        lax.fori_loop(0, lax.div(pad_ref[1, N_EXPERTS], tile), per_tile, 0)

    for_each_copy(lambda copy: copy.start())
    for_each_copy(lambda copy: copy.wait())


def _dispatch_body(pad_ref, *refs, bounds):
    ns = len(bounds) - 1
    pos_refs, tok_refs = refs[:ns], refs[ns:2 * ns]
    xs_out, ring, sems, zeros, zsem = refs[2 * ns:]
    i = pl.program_id(0)
    tm = ring.shape[1]
    last = pl.num_programs(0) - 1

    pl.when(i == 0)(functools.partial(_zero_pad_rows, pad_ref, xs_out, zeros, zsem))

    for slot in range(2):
        mine = i % 2 == slot
        for s in range(ns):
            @pl.when(mine & (i >= bounds[s]) & (i < bounds[s + 1]))
            def _(slot=slot, s=s):
                ring[slot] = tok_refs[s][...]

                def issue(t, carry):
                    for k in range(TOP_K):
                        pltpu.make_async_copy(ring.at[slot, pl.ds(t, 1)],
                                              xs_out.at[pl.ds(pos_refs[s][t * TOP_K + k], 1)],
                                              sems.at[slot]).start(priority=k % DMA_PRIORITIES)
                    return carry

                lax.fori_loop(0, tm, issue, 0)

        @pl.when(mine & (i > 0))
        def _(slot=slot):
            _wait_rows(ring.at[1 - slot], sems.at[1 - slot], TOP_K)

        @pl.when(mine & (i == last))
        def _(slot=slot):
            _wait_rows(ring.at[slot], sems.at[slot], TOP_K)


def _dispatch(poss, pad, toks, rows, tm):
    d = toks[0].shape[1]
    bounds = [0]
    for tok in toks:
        bounds.append(bounds[-1] + tok.shape[0] // tm)

    def local(s):
        lo, n = bounds[s], bounds[s + 1] - bounds[s]
        return lambda i: jnp.clip(i - lo, 0, n - 1)

    in_specs = [pl.BlockSpec(memory_space=pltpu.SMEM)]
    in_specs += [pl.BlockSpec((tm * TOP_K,), lambda i, f=local(s): (f(i),),
                              memory_space=pltpu.SMEM) for s in range(len(toks))]
    in_specs += [pl.BlockSpec((tm, d), lambda i, f=local(s): (f(i), 0)) for s in range(len(toks))]
    return pl.pallas_call(
        functools.partial(_dispatch_body, bounds=tuple(bounds)),
        grid=(bounds[-1],),
        in_specs=in_specs,
        out_specs=pl.BlockSpec(memory_space=pl.ANY),
        out_shape=jax.ShapeDtypeStruct((rows, d), toks[0].dtype),
        scratch_shapes=[pltpu.VMEM((2, tm, d), toks[0].dtype), pltpu.SemaphoreType.DMA((2,)),
                        pltpu.VMEM((MOE_TILE, d), toks[0].dtype), pltpu.SemaphoreType.DMA],
        compiler_params=pltpu.CompilerParams(dimension_semantics=("arbitrary",),
                                             has_side_effects=True),
        name="moe_dispatch",
    )(pad, *poss, *toks)


def _experts_body(te_ref, na_ref, first_ref, next_ref, slot_ref, x_ref, wgu_hbm, bgu_ref, wd_hbm,
                  bd_ref, y_ref, wgu_f, wd_f, wgu_s, wd_s, sems):
    r = pl.program_id(0)

    def fetch(e, slot):
        return (pltpu.make_async_copy(wgu_hbm.at[e], wgu_f.at[slot], sems.at[0, slot]),
                pltpu.make_async_copy(wd_hbm.at[e], wd_f.at[slot], sems.at[1, slot]))

    @pl.when(r == 0)
    def _():
        for copy in fetch(te_ref[0], slot_ref[0]):
            copy.start()

    @pl.when(first_ref[r] != 0)
    def _():
        slot = slot_ref[r]
        for copy in fetch(te_ref[r], slot):
            copy.wait()
        rows = 128
        for c in range(wgu_s.shape[0] // rows):
            wgu_s[c * rows:(c + 1) * rows, :] = wgu_f[slot, c * rows:(c + 1) * rows, :].astype(BF16)
        for c in range(wd_s.shape[0] // rows):
            wd_s[c * rows:(c + 1) * rows, :] = wd_f[slot, c * rows:(c + 1) * rows, :].astype(BF16)

        @pl.when(next_ref[r] >= 0)
        def _():
            for copy in fetch(next_ref[r], 1 - slot):
                copy.start()

    @pl.when(r < na_ref[0])
    def _():
        gu = _dot(x_ref[...].astype(BF16), wgu_s[...]) + bgu_ref[0]
        dff = gu.shape[1] // 2
        gate = jnp.minimum(gu[:, :dff], SWIGLU_LIMIT)
        up = jnp.clip(gu[:, dff:], -SWIGLU_LIMIT, SWIGLU_LIMIT)
        act = (up + 1.0) * gate * jax.nn.sigmoid(SWIGLU_ALPHA * gate)
        y_ref[...] = _dot(act.astype(BF16), wd_s[...]) + bd_ref[0]

    @pl.when(r >= na_ref[0])
    def _():
        y_ref[...] = jnp.zeros(y_ref.shape, F32)


def _experts(tile_expert, n_active, xs, wgu, bgu, wd, bd):
    p, d = xs.shape
    _, _, dff2 = wgu.shape
    tm = MOE_TILE
    n = p // tm
    r = jnp.arange(n, dtype=I32)
    active = r < n_active[0]
    change = jnp.concatenate([jnp.ones((1,), bool), tile_expert[1:] != tile_expert[:-1]])
    first = (change & active).astype(I32)
    group = jnp.cumsum(first) - 1
    slot = jnp.maximum(group, 0) % 2
    opens = jnp.where(first > 0, r, n)
    next_open = jnp.min(jnp.where(opens[None, :] > r[:, None], opens[None, :], n), axis=1)
    nxt = jnp.where(next_open < n, tile_expert[jnp.minimum(next_open, n - 1)], -1).astype(I32)
    grid_spec = pltpu.PrefetchScalarGridSpec(
        num_scalar_prefetch=5,
        grid=(n,),
        in_specs=[pl.BlockSpec((tm, d), lambda r, te, *_: (r, 0)),
                  pl.BlockSpec(memory_space=pl.ANY),
                  pl.BlockSpec((1, 1, dff2), lambda r, te, *_: (te[r], 0, 0)),
                  pl.BlockSpec(memory_space=pl.ANY),
                  pl.BlockSpec((1, 1, d), lambda r, te, *_: (te[r], 0, 0))],
        out_specs=pl.BlockSpec((tm, d), lambda r, te, *_: (r, 0)),
        scratch_shapes=[pltpu.VMEM((2, d, dff2), F32), pltpu.VMEM((2, dff2 // 2, d), F32),
                        pltpu.VMEM((d, dff2), BF16), pltpu.VMEM((dff2 // 2, d), BF16),
                        pltpu.SemaphoreType.DMA((2, 2))],
    )
    return pl.pallas_call(
        _experts_body,
        grid_spec=grid_spec,
        out_shape=jax.ShapeDtypeStruct((p, d), F32),
        compiler_params=pltpu.CompilerParams(dimension_semantics=("arbitrary",),
                                             vmem_limit_bytes=VMEM_LIMIT),
        name="moe_experts",
    )(tile_expert, n_active, first, nxt, slot.astype(I32), xs, wgu, bgu, wd, bd)


def _combine_body(slow_ref, win_ref, pos_ref, col_ref, prob_ref, x1_ref, y_hbm, out_ref,
                  wbuf, rbuf, sems):
    i = pl.program_id(0)
    n = pl.num_programs(0) - 1
    tm = x1_ref.shape[0]
    slow = slow_ref[0] != 0
    nwin = wbuf.shape[1] // WIN_ROWS

    for slot in range(2):
        mine = i % 2 == slot

        @pl.when(mine & (i < n) & jnp.logical_not(slow))
        def _(slot=slot):
            for e in range(nwin):
                first = pl.multiple_of(win_ref[0, 0, e], SUBLANES)
                pltpu.make_async_copy(y_hbm.at[pl.ds(first, WIN_ROWS)],
                                      wbuf.at[slot, pl.ds(e * WIN_ROWS, WIN_ROWS)],
                                      sems.at[slot]).start()

        @pl.when(mine & (i > 0) & jnp.logical_not(slow))
        def _(slot=slot):
            _wait_rows(wbuf.at[1 - slot], sems.at[1 - slot], 1)
            col = col_ref[...]
            prob = prob_ref[...]
            lane = lax.broadcasted_iota(I32, (tm, LANES), 1)
            rel = [jnp.broadcast_to(col[:, k:k + 1], (tm, LANES)) - lane for k in range(TOP_K)]
            wgt = [jnp.broadcast_to(prob[:, k:k + 1], (tm, LANES)) for k in range(TOP_K)]
            pick = []
            for c in range(wbuf.shape[1] // LANES):
                g = jnp.zeros((tm, LANES), F32)
                for k in range(TOP_K):
                    g = jnp.where(rel[k] == c * LANES, wgt[k], g)
                pick.append(g.astype(BF16))
            ffn = _dot(jnp.concatenate(pick, axis=1), wbuf[1 - slot].astype(BF16))
            out_ref[...] = x1_ref[...] + ffn

        @pl.when(mine & (i < n) & slow)
        def _(slot=slot):
            def issue(t, carry):
                for k in range(TOP_K):
                    pltpu.make_async_copy(y_hbm.at[pl.ds(pos_ref[k, t], 1)],
                                          rbuf.at[slot, k, pl.ds(t, 1)],
                                          sems.at[slot]).start(priority=k % DMA_PRIORITIES)
                return carry
            lax.fori_loop(0, tm, issue, 0)

        @pl.when(mine & (i > 0) & slow)
        def _(slot=slot):
            _wait_rows(rbuf.at[1 - slot, 0], sems.at[1 - slot], TOP_K)
            prob = prob_ref[...]
            ffn = prob[:, 0:1] * rbuf[1 - slot, 0]
            for k in range(1, TOP_K):
                ffn = ffn + prob[:, k:k + 1] * rbuf[1 - slot, k]
            out_ref[...] = x1_ref[...] + ffn


def _combine(slow, win, pos, col, prob, x1, ys, tm):
    r, d = x1.shape
    n = r // tm
    prev = lambda w: pl.BlockSpec((tm, w), lambda i: (jnp.maximum(i - 1, 0), 0))
    cur = lambda i: jnp.minimum(i, n - 1)
    return pl.pallas_call(
        _combine_body,
        grid=(n + 1,),
        in_specs=[pl.BlockSpec(memory_space=pltpu.SMEM),
                  pl.BlockSpec((1, 1, LANES), lambda i: (cur(i), 0, 0), memory_space=pltpu.SMEM),
                  pl.BlockSpec((SUBLANES, tm), lambda i: (0, cur(i)), memory_space=pltpu.SMEM),
                  prev(LANES), prev(LANES), prev(d), pl.BlockSpec(memory_space=pl.ANY)],
        out_specs=prev(d),
        out_shape=jax.ShapeDtypeStruct((r, d), F32),
        scratch_shapes=[pltpu.VMEM((2, N_EXPERTS * WIN_ROWS, d), F32),
                        pltpu.VMEM((2, TOP_K, tm, d), F32), pltpu.SemaphoreType.DMA((2,))],
        compiler_params=pltpu.CompilerParams(dimension_semantics=("arbitrary",),
                                             vmem_limit_bytes=VMEM_LIMIT),
        name=f"moe_combine_{n}",
    )(slow, win, pos, col, prob, x1, ys)


def _moe(streams, wgu, bgu, wd, bd):
    tm = MOE_TILE
    counts = [s[4][0, :N_EXPERTS].astype(I32) for s in streams]
    cnt = sum(counts)
    padded = ((cnt + tm - 1) // tm) * tm
    ends = jnp.cumsum(padded)
    n_pairs = sum(s[0].shape[0] for s in streams) * TOP_K
    n_tiles = n_pairs // tm + N_EXPERTS
    tile_start = jnp.arange(n_tiles, dtype=I32) * tm
    tile_expert = jnp.minimum(jnp.sum((ends[None, :] <= tile_start[:, None]).astype(I32), axis=1),
                              N_EXPERTS - 1)
    n_active = (ends[-1:] // tm).astype(I32)
    start = ends - padded
    pad = jnp.stack([jnp.append(start + cnt, ends[-1]),
                     jnp.append(padded - cnt, n_tiles * tm - ends[-1])])
    routes = []
    for (x1, tok, eidx, prob, _), c in zip(streams, counts):
        cin = jnp.pad(start, (0, LANES - N_EXPERTS)).astype(F32)[None]
        routes.append(_rank(eidx, cin, _row_tile(eidx.shape[0], 256), n_tiles * tm - WIN_ROWS))
        start = start + c
    flat = [r[0][:TOP_K].T.reshape(-1) for r in routes]
    xs = _dispatch(flat, pad, [s[1] for s in streams], n_tiles * tm,
                   min(_row_tile(s[1].shape[0], 256) for s in streams))
    ys = _experts(tile_expert, n_active, xs, wgu, bgu, wd, bd)
    outs = []
    for s, (pos, col, win) in zip(streams, routes):
        slow = jnp.max(win[:, 0, FLAG_LANE]).reshape(1)
        outs.append(_combine(slow, win, pos, col, s[3], s[0], ys, _row_tile(s[0].shape[0], 256)))
    return outs


def _row_tile(r, want):
    tm = min(r, want)
    assert r % tm == 0, (r, tm)
    return tm


def _mixers(x2, bsz, t, pos, s0, caches, wts, layer):
    (g1, wm, ws, wg, qg, kg, lbp, ng) = wts
    r = x2.shape[0]
    tm = _row_tile(r, 256)
    assert t % tm == 0 or tm % t == 0
    tabs_a = _rope_tables(pos, HEAD_DIM, 1)
    tabs_b = _rope_tables(pos, IDX_DIM, LANES // IDX_DIM)
    tabs = tabs_a + tabs_b
    if tm > t:
        tabs = tuple(jnp.tile(tb, (tm // t, 1)) for tb in tabs)
    tk = max(tm, _row_tile(t, 512)) if caches is None else tm
    (aq, af, ai, ag, q, kf, kb, vf, vb, qim, kif, ki2, wi, ga, gb, vt) = _inproj(
        x2, g1, wm, ws, wg, qg, kg, tabs, tm, tk)
    oa, s_new = _hgrn(aq, af, ai, ag, s0, lbp, ng, bsz, t, layer)
    if caches is None:
        ob = _dsa_prompt(q, qim, wi, kb, vt, ki2, bsz, t, _row_tile(t, 256), tk)
    else:
        ck, cv, kidx2, past, cache_off = caches
        tk = _row_tile(past, 512)
        nsub = 2 if (past // tk) % 2 == 0 else 1
        ob = _dsa_sample(q, qim, wi, kb, vb, ki2, kidx2, ck, cv, bsz, t, past, tk, nsub, cache_off)
    return oa, ob, ga, gb, kf, vf, kif, s_new


def kernel(x_prompt, x_sample, cache_k, cache_v, cache_kidx, state_hgrn, norm1_g, w_in, lower_bounds, hgrn_norm_g, q_norm_g, k_norm_g, w_branch_a, w_branch_b, w_out, norm2_g, w_router, b_router, w_gate_up, b_gate_up, w_down, b_down):
    bp, tp, d = x_prompt.shape
    bs, ts, _ = x_sample.shape
    depth = w_in.shape[0]
    past = cache_k.shape[2]
    kvw = B_KV_HEADS * HEAD_DIM
    pos_p = jnp.arange(tp, dtype=I32)
    pos_s = past + jnp.arange(ts, dtype=I32)
    xp = x_prompt.reshape(bp * tp, d)
    xs = x_sample.reshape(bs * ts, d)
    n_main = 2 * A_HEADS * A_DK + 2 * A_HEADS * A_DV + B_HEADS * HEAD_DIM + 2 * kvw + IDX_HEADS * IDX_DIM
    n_small = n_main + IDX_DIM + IDX_HEADS
    outs = [[] for _ in range(8)]
    for l in range(depth):
        w = w_in[l]
        wm = w[:, :n_main].astype(BF16)
        w_ik = w[:, n_main:n_main + IDX_DIM]
        w_iw = w[:, n_main + IDX_DIM:n_small]
        ws = jnp.concatenate(
            [w_ik, w_ik, w_iw, jnp.zeros((d, LANES - IDX_HEADS), w.dtype)], axis=1).astype(BF16)
        wg = w[:, n_small:].astype(BF16)
        wts = (norm1_g[l][None], wm, ws, wg, q_norm_g[l][None], k_norm_g[l][None],
               lower_bounds, hgrn_norm_g[l][None])
        s0_p = jnp.zeros((bp, A_HEADS, A_DK, A_DV), F32)
        oa_p, ob_p, ga_p, gb_p, kp, vp, kip, sp = _mixers(xp, bp, tp, pos_p, s0_p, None, wts, l)
        kidx2 = jnp.concatenate([cache_kidx[l], cache_kidx[l]], axis=-1).astype(BF16)
        caches = (cache_k.reshape(-1, HEAD_DIM), cache_v.reshape(-1, HEAD_DIM),
                  kidx2.reshape(bs * past, LANES), past, l * bs)
        oa_s, ob_s, ga_s, gb_s, ks, vs, kis, ss = _mixers(
            xs, bs, ts, pos_s, state_hgrn[l], caches, wts, l)

        wa = w_branch_a[l].astype(BF16)
        wb = w_branch_b[l].astype(BF16)
        wo = w_out[l].astype(BF16)
        wr = jnp.pad(w_router[l], ((0, 0), (0, LANES - N_EXPERTS))).astype(BF16)
        br = jnp.pad(b_router[l], (0, LANES - N_EXPERTS))[None]
        g2 = norm2_g[l][None]
        bgu = b_gate_up[l][:, None, :]
        bd = b_down[l][:, None, :]
        streams = []
        for (x2, oa, ob, ga, gb) in ((xp, oa_p, ob_p, ga_p, gb_p), (xs, oa_s, ob_s, ga_s, gb_s)):
            r = x2.shape[0]
            streams.append(_post(oa, ob, ga, gb, x2, wa, wb, wo, g2, wr, br, _row_tile(r, 512)))
        xp, xs = _moe(streams, w_gate_up[l], bgu, w_down[l], bd)
        for lst, v in zip(outs, (kp.reshape(bp, tp, B_KV_HEADS, HEAD_DIM),
                                 vp.reshape(bp, tp, B_KV_HEADS, HEAD_DIM),
                                 kip.reshape(bp, tp, IDX_DIM), sp,
                                 ks.reshape(bs, ts, B_KV_HEADS, HEAD_DIM),
                                 vs.reshape(bs, ts, B_KV_HEADS, HEAD_DIM),
                                 kis.reshape(bs, ts, IDX_DIM), ss)):
            lst.append(v)
    return (xp.reshape(bp, tp, d), xs.reshape(bs, ts, d)) + tuple(jnp.stack(o) for o in outs)
```

```python
import functools

import jax
import jax.numpy as jnp
from jax import lax
from jax.experimental import pallas as pl
from jax.experimental.pallas import tpu as pltpu

F32 = jnp.float32
BF16 = jnp.bfloat16
I32 = jnp.int32

CHUNK = 64
A_HEADS = 8
A_DK = 128
A_DV = 128
B_HEADS = 8
B_KV_HEADS = 4
HEAD_DIM = 128
IDX_HEADS = 8
IDX_DIM = 64
IDX_TOPK_MAX = 256
ROPE_THETA = 10000.0
N_EXPERTS = 32
TOP_K = 4
SWIGLU_LIMIT = 7.0
SWIGLU_ALPHA = 1.702
EPS = 1e-6

LANES = 128
INT_MIN = -(2 ** 31)
NEG = -1e30
VMEM_LIMIT = 56 * 1024 * 1024


def _rms(x):
    return x * lax.rsqrt(jnp.mean(x * x, axis=-1, keepdims=True) + EPS)


def _silu(x):
    return x * jax.nn.sigmoid(x)


def _dot(a, b):
    return jnp.dot(a, b, preferred_element_type=F32)


def _dot_nt(a, b):
    return lax.dot_general(a, b, (((1,), (1,)), ((), ())), preferred_element_type=F32)


def _dot_tn(a, b):
    return lax.dot_general(a, b, (((0,), (0,)), ((), ())), preferred_element_type=F32)


def _const_spec(shape):
    zeros = (0,) * len(shape)
    return pl.BlockSpec(shape, lambda *_: zeros, pipeline_mode=pl.Buffered(1))


def _rope_tables(pos, d, reps):
    inv = 1.0 / (ROPE_THETA ** (jnp.arange(0, d, 2, dtype=F32) / d))
    ang = pos.astype(F32)[:, None] * inv[None, :]
    cos = jnp.cos(ang)
    sin = jnp.sin(ang)
    cos_t = jnp.concatenate([cos, cos] * reps, axis=-1)
    sin_t = jnp.concatenate([-sin, sin] * reps, axis=-1)
    return cos_t, sin_t


def _inproj_body(x_ref, g1_ref, wm_ref, ws_ref, wg_ref, qg_ref, kg_ref, ca_ref, sa_ref, cb_ref,
                 sb_ref, aq_o, af_o, ai_o, ag_o, q_o, kf_o, kb_o, vf_o, vb_o, qim_o, kif_o, ki2_o,
                 wi_o, ga_o, gb_o, vt_o):
    x = x_ref[...]
    tm = x.shape[0]
    hb = (_rms(x) * g1_ref[...]).astype(BF16)
    a_qk = A_HEADS * A_DK
    a_v = A_HEADS * A_DV
    b_q = B_HEADS * HEAD_DIM
    b_kv = B_KV_HEADS * HEAD_DIM
    iq_w = IDX_HEADS * IDX_DIM
    o = 0
    aq_o[...] = _dot(hb, wm_ref[:, o:o + a_qk]).astype(BF16)
    o += a_qk
    af_o[...] = _dot(hb, wm_ref[:, o:o + a_qk])
    o += a_qk
    ai_o[...] = _dot(hb, wm_ref[:, o:o + a_v]).astype(BF16)
    o += a_v
    ag_o[...] = _dot(hb, wm_ref[:, o:o + a_v]).astype(BF16)
    o += a_v

    ca = ca_ref[...]
    sa = sa_ref[...]

    def rope_head(y):
        return y * ca + pltpu.roll(y, HEAD_DIM // 2, 1) * sa

    zq = _dot(hb, wm_ref[:, o:o + b_q])
    o += b_q
    for h in range(B_HEADS):
        sl = slice(h * HEAD_DIM, (h + 1) * HEAD_DIM)
        q_o[:, sl] = (rope_head(_rms(zq[:, sl]) * qg_ref[...]) * Q_SCALE).astype(BF16)
    zk = _dot(hb, wm_ref[:, o:o + b_kv])
    o += b_kv
    for h in range(B_KV_HEADS):
        sl = slice(h * HEAD_DIM, (h + 1) * HEAD_DIM)
        y = rope_head(_rms(zk[:, sl]) * kg_ref[...])
        kf_o[pl.ds(h, tm, stride=B_KV_HEADS), :] = y
        kb_o[:, sl] = y.astype(BF16)
    zv = _dot(hb, wm_ref[:, o:o + b_kv])
    o += b_kv
    for h in range(B_KV_HEADS):
        vf_o[pl.ds(h, tm, stride=B_KV_HEADS), :] = zv[:, h * HEAD_DIM:(h + 1) * HEAD_DIM]
    vb_o[...] = zv.astype(BF16)
    vt_o[0] = zv.T.astype(BF16)

    cb = cb_ref[...]
    sb = sb_ref[...]
    lane = lax.broadcasted_iota(I32, (tm, LANES), 1)
    first_half = (lane & (IDX_DIM - 1)) < (IDX_DIM // 2)

    def rope_idx(y):
        partner = jnp.where(first_half, pltpu.roll(y, LANES - IDX_DIM // 2, 1),
                            pltpu.roll(y, IDX_DIM // 2, 1))
        return y * cb + partner * sb

    zi = _dot(hb, wm_ref[:, o:o + iq_w])
    for p in range(iq_w // LANES):
        y = rope_idx(zi[:, p * LANES:(p + 1) * LANES])
        qim_o[:, (2 * p) * LANES:(2 * p + 1) * LANES] = jnp.where(lane < IDX_DIM, y, 0.0).astype(BF16)
        qim_o[:, (2 * p + 1) * LANES:(2 * p + 2) * LANES] = jnp.where(lane >= IDX_DIM, y, 0.0).astype(BF16)
    zs = _dot(hb, ws_ref[...])
    y = rope_idx(zs[:, :LANES])
    ki2_o[...] = y.astype(BF16)
    kif_o[...] = y[:, :IDX_DIM]
    wi_o[...] = zs[:, LANES:] * (IDX_HEADS ** -0.5 * IDX_DIM ** -0.5)
    zg = _dot(hb, wg_ref[...])
    d = zg.shape[1] // 2
    ga_o[...] = zg[:, :d].astype(BF16)
    gb_o[...] = zg[:, d:].astype(BF16)


def _inproj(x2, g1, wm, ws, wg, qg, kg, tabs, tm, vt_tile):
    r, d = x2.shape
    ca, sa, cb, sb = tabs
    npos = ca.shape[0] // tm
    per = vt_tile // tm
    row = lambda w: pl.BlockSpec((tm, w), lambda i: (i, 0))
    tab = pl.BlockSpec((tm, LANES), lambda i: (i % npos, 0))
    a_qk = A_HEADS * A_DK
    b_q = B_HEADS * HEAD_DIM
    b_kv = B_KV_HEADS * HEAD_DIM
    kvh = B_KV_HEADS
    outs = [
        (1, a_qk, BF16), (1, a_qk, F32), (1, a_qk, BF16), (1, a_qk, BF16),
        (1, b_q, BF16), (kvh, HEAD_DIM, F32), (1, b_kv, BF16),
        (kvh, HEAD_DIM, F32), (1, b_kv, BF16),
        (1, IDX_HEADS * LANES, BF16), (1, IDX_DIM, F32), (1, LANES, BF16),
        (1, LANES, F32), (1, d, BF16), (1, d, BF16),
    ]
    return pl.pallas_call(
        _inproj_body,
        grid=(r // tm,),
        in_specs=[row(d), _const_spec(g1.shape), _const_spec(wm.shape), _const_spec(ws.shape),
                  _const_spec(wg.shape), _const_spec(qg.shape), _const_spec(kg.shape),
                  tab, tab, tab, tab],
        out_specs=[pl.BlockSpec((tm * k, w), lambda i: (i, 0)) for k, w, _ in outs] + [
            pl.BlockSpec((1, b_kv, tm), lambda i: (i // per, 0, i % per))],
        out_shape=[jax.ShapeDtypeStruct((r * k, w), dt) for k, w, dt in outs] + [
            jax.ShapeDtypeStruct((r // vt_tile, b_kv, vt_tile), BF16)],
        compiler_params=pltpu.CompilerParams(dimension_semantics=("arbitrary",),
                                             vmem_limit_bytes=VMEM_LIMIT),
        name=f"inproj_{r // tm}",
    )(x2, g1, wm, ws, wg, qg, kg, ca, sa, cb, sb)


HGRN_BATCH = 4


def _hgrn_body(aq_ref, af_ref, ai_ref, ag_ref, s0_ref, lbp_ref, ng_ref, oa_ref, sn_ref, st_ref,
               *, layer):
    c = pl.program_id(1)
    nc = pl.num_programs(1)
    nb = aq_ref.shape[0]

    @pl.when(c == 0)
    def _():
        for b in range(nb):
            for h in range(A_HEADS):
                st_ref[b, h] = s0_ref[b, h].T

    lbp = lbp_ref[...]
    e = jnp.exp(lbp - jnp.max(lbp, axis=0, keepdims=True))
    sm = e / jnp.sum(e, axis=0, keepdims=True)
    lb = jnp.sum(sm[:layer + 1], axis=0, keepdims=True)

    n = aq_ref.shape[1]
    r_i = lax.broadcasted_iota(I32, (n, n), 0)
    c_i = lax.broadcasted_iota(I32, (n, n), 1)
    causal = r_i >= c_i
    tri = jnp.where(causal, 1.0, 0.0).astype(BF16)
    prep = []
    for b in range(nb):
        f = lb + (1.0 - lb) * jax.nn.sigmoid(af_ref[b])
        lf = jnp.log(f)
        hi = lf.astype(BF16)
        r1 = lf - hi.astype(F32)
        mid = r1.astype(BF16)
        lo = (r1 - mid.astype(F32)).astype(BF16)
        cum = _dot(tri, hi) + _dot(tri, mid) + _dot(tri, lo)
        last = cum[n - 1:n, :]
        qd = _silu(aq_ref[b].astype(F32)) * jnp.exp(cum)
        k = 1.0 - f
        prep.append((qd, k * jnp.exp(-cum), k * jnp.exp(last - cum), jnp.exp(last),
                     _silu(ag_ref[b].astype(F32))))
    for h in range(A_HEADS):
        sl = slice(h * A_DK, (h + 1) * A_DK)
        for b, (qd, kd, kt, el, gate) in enumerate(prep):
            qd_h = qd[:, sl].astype(BF16)
            v_h = ai_ref[b, :, sl]
            att = jnp.where(causal, _dot_nt(qd_h, kd[:, sl].astype(BF16)), 0.0)
            st = st_ref[b, h]
            o = _dot(att.astype(BF16), v_h) + _dot_nt(qd_h, st.astype(BF16))
            st_ref[b, h] = st * el[:, sl] + _dot_tn(v_h, kt[:, sl].astype(BF16))
            oa_ref[b, :, sl] = (_rms(o) * ng_ref[...] * gate[:, sl]).astype(BF16)

    @pl.when(c == nc - 1)
    def _():
        for b in range(nb):
            for h in range(A_HEADS):
                sn_ref[b, h] = st_ref[b, h].T


def _hgrn(aq, af, ai, ag, s0, lbp, ng, bsz, t, layer):
    d = aq.shape[1]
    nc = t // CHUNK
    nb = _row_tile(bsz, HGRN_BATCH)
    seq = lambda a: a.reshape(bsz, t, d)
    row = pl.BlockSpec((nb, CHUNK, d), lambda g, c: (g, c, 0))
    st = pl.BlockSpec((nb, A_HEADS, A_DK, A_DV), lambda g, c: (g, 0, 0, 0))
    oa, s_new = pl.pallas_call(
        functools.partial(_hgrn_body, layer=layer),
        grid=(bsz // nb, nc),
        in_specs=[row, row, row, row, st, _const_spec(lbp.shape), _const_spec(ng.shape)],
        out_specs=[row, st],
        out_shape=[jax.ShapeDtypeStruct((bsz, t, d), BF16),
                   jax.ShapeDtypeStruct((bsz, A_HEADS, A_DK, A_DV), F32)],
        scratch_shapes=[pltpu.VMEM((nb, A_HEADS, A_DV, A_DK), F32)],
        compiler_params=pltpu.CompilerParams(dimension_semantics=("arbitrary", "arbitrary"),
                                             vmem_limit_bytes=VMEM_LIMIT),
        name=f"hgrn_{bsz // nb}x{nc}",
    )(seq(aq), seq(af), seq(ai), seq(ag), s0, lbp, ng)
    return oa.reshape(bsz * t, d), s_new


def _chunk_of(pos):
    return jnp.right_shift(pos, CHUNK.bit_length() - 1)


def _transpose_wi(wi, tq):
    if tq % LANES:
        wi = jnp.concatenate([wi, jnp.zeros((LANES - tq % LANES, LANES), F32)], axis=0)
    return wi.T[:IDX_HEADS, :tq]


def _score_tile(ki2_tile, qim_ref, wt):
    acc = None
    for h in range(IDX_HEADS):
        s = _dot_nt(ki2_tile, qim_ref[:, h * LANES:(h + 1) * LANES])
        term = jnp.maximum(s, 0.0) * wt[h:h + 1, :]
        acc = term if acc is None else acc + term
    return acc


def _mask_scores(score, adm):
    score = jnp.where(score == 0.0, 0.0, score)
    return score if adm is None else jnp.where(adm, score, -jnp.inf)


def _sort_key(score):
    bits = lax.bitcast_convert_type(score, I32)
    return jnp.where(bits < 0, bits ^ 0x7FFFFFFF, bits)


def _key_to_score(key):
    return lax.bitcast_convert_type(jnp.where(key < 0, key ^ 0x7FFFFFFF, key), F32)


SUBLANES = 8
COUNT_ACCS = 4
KEY_BITS = 32
PLANE_KEYS = SUBLANES * KEY_BITS


def _store_bit_planes(planes_ref, key, blk0):
    u = key ^ INT_MIN
    tk = u.shape[0]
    for blk in range(tk // PLANE_KEYS):
        a = [u[blk * PLANE_KEYS + i * SUBLANES:blk * PLANE_KEYS + (i + 1) * SUBLANES, :]
             for i in range(KEY_BITS)]
        for j, m in ((16, 0x0000FFFF), (8, 0x00FF00FF), (4, 0x0F0F0F0F), (2, 0x33333333),
                     (1, 0x55555555)):
            k = 0
            while k < KEY_BITS:
                t = (a[k] ^ lax.shift_right_logical(a[k + j], j)) & m
                a[k] = a[k] ^ t
                a[k + j] = a[k + j] ^ lax.shift_left(t, j)
                k = (k + j + 1) & ~j
        row = pl.multiple_of((blk0 + blk) * SUBLANES, SUBLANES)
        for p in range(KEY_BITS):
            planes_ref[p, pl.ds(row, SUBLANES), :] = a[p]


def _select(sc_ref, planes_ref, bias_ref, cut_ref, stat_ref, nkt, krow, tk, tq, nbits):
    nrow = planes_ref.shape[1]
    nblk = nkt * (tk // PLANE_KEYS)
    blk_of_row = jnp.right_shift(lax.broadcasted_iota(I32, (nrow, tq), 0),
                                 SUBLANES.bit_length() - 1)

    def word_count(words):
        pc = lax.population_count(words).reshape(nrow // SUBLANES, SUBLANES, tq)
        return jnp.sum(jnp.sum(pc, axis=0).astype(F32), axis=0, keepdims=True)

    def bit_body(i, carry):
        alive, above, thr_u = carry
        ones = alive & planes_ref[i]
        c1 = word_count(ones)
        take = above + c1 >= krow
        alive = jnp.where(take, ones, alive ^ ones)
        above = jnp.where(take, above, above + c1)
        thr_u = jnp.where(take, thr_u | jnp.left_shift(jnp.int32(1), KEY_BITS - 1 - i), thr_u)
        return alive, above, thr_u

    alive0 = jnp.where(blk_of_row < nblk, -1, 0).astype(I32)
    _, _, thr_u = lax.fori_loop(
        0, KEY_BITS, bit_body, (alive0, jnp.zeros((1, tq), F32), jnp.zeros((1, tq), I32)),
        unroll=True)

    nchunk = tk // SUBLANES
    sub = lax.broadcasted_iota(I32, (SUBLANES, tq), 0)

    def count(pred_fn):
        def body(j, accs):
            t = sc_ref[j]
            accs = list(accs)
            for c in range(nchunk):
                a = c % COUNT_ACCS
                accs[a] = pred_fn(t[c * SUBLANES:(c + 1) * SUBLANES, :], j, c, accs[a])
            return tuple(accs)
        zero = jnp.zeros((SUBLANES, tq), F32)
        accs = lax.fori_loop(0, nkt, body, (zero,) * COUNT_ACCS)
        tot = accs[0]
        for a in accs[1:]:
            tot = tot + a
        return jnp.sum(tot, axis=0, keepdims=True)

    def count_ge(cand):
        c8 = jnp.broadcast_to(cand, (SUBLANES, tq))
        return count(lambda t, j, c, acc: jnp.where(t >= c8, acc + 1.0, acc))

    def count_gt(cand):
        c8 = jnp.broadcast_to(cand, (SUBLANES, tq))
        return count(lambda t, j, c, acc: jnp.where(t > c8, acc + 1.0, acc))

    def record(thr):
        stat_ref[0:1, :] = thr
        stat_ref[1:2, :] = count_ge(thr)
        stat_ref[2:3, :] = count_gt(thr)

    record(_key_to_score(thr_u ^ INT_MIN))
    good = (stat_ref[2:3, :] < krow) & (stat_ref[1:2, :] >= krow)

    @pl.when(jnp.max(jnp.where(good, 0.0, 1.0)) > 0.0)
    def _():
        def cmp_body(i, tu):
            cand_u = tu | jnp.left_shift(jnp.int32(1), KEY_BITS - 1 - i)
            return jnp.where(count_ge(_key_to_score(cand_u ^ INT_MIN)) >= krow, cand_u, tu)
        tu = lax.fori_loop(0, KEY_BITS, cmp_body, jnp.zeros((1, tq), I32))
        record(_key_to_score(tu ^ INT_MIN))

    thr = stat_ref[0:1, :]
    n_ge = stat_ref[1:2, :]
    need = krow - stat_ref[2:3, :]

    cut_ref[...] = jnp.full(cut_ref.shape, 2 ** 31 - 1, I32)

    @pl.when(jnp.max(jnp.where(n_ge > krow, 1.0, 0.0)) > 0.0)
    def _():
        thr8 = jnp.broadcast_to(thr, (SUBLANES, tq))

        def idx_body(i, cut):
            cand = cut | jnp.left_shift(jnp.int32(1), nbits - 1 - i)
            c8 = jnp.broadcast_to(cand, (SUBLANES, tq))
            below = count(lambda t, j, c, acc: jnp.where(
                t == thr8, jnp.where(j * tk + c * SUBLANES + sub < c8, acc + 1.0, acc), acc))
            return jnp.where(below < need, cand, cut)
        cut_ref[...] = lax.fori_loop(0, nbits, idx_body, jnp.zeros((1, tq), I32))

    cut = cut_ref[...]

    def write(j, carry):
        t = sc_ref[j]
        idx = j * tk + lax.broadcasted_iota(I32, (tk, tq), 0)
        tie = jnp.where(t == thr, jnp.where(idx <= cut, 0.0, NEG), NEG)
        bias = jnp.where(t > thr, 0.0, tie)
        if callable(bias_ref):
            bias_ref(j, bias)
        else:
            bias_ref[j] = bias
        return carry

    lax.fori_loop(0, nkt, write, 0)


def _stack_heads(q_ref, u, stack):
    heads = [q_ref[:, h * HEAD_DIM:(h + 1) * HEAD_DIM] for h in range(u * stack, (u + 1) * stack)]
    return heads[0] if stack == 1 else jnp.concatenate(heads, axis=0)


def _store_heads(o_ref, u, stack, o, tq):
    for r in range(stack):
        h = u * stack + r
        o_ref[:, h * HEAD_DIM:(h + 1) * HEAD_DIM] = o[r * tq:(r + 1) * tq].astype(o_ref.dtype)


Q_SCALE = HEAD_DIM ** -0.5 * 1.4426950408889634
SUM_ROWS = 16


def _attn_reset(m_ref, l_ref, acc_ref):
    m_ref[...] = jnp.full(m_ref.shape, NEG, F32)
    l_ref[...] = jnp.zeros(l_ref.shape, F32)
    acc_ref[...] = jnp.zeros(acc_ref.shape, F32)


def _attn_logits(q_ref, u, stack, kt, bias, m_ref, state=None):
    state = u if state is None else state
    qg = _stack_heads(q_ref, u, stack)
    if bias.shape[1] != qg.shape[0]:
        bias = jnp.concatenate([bias] * stack, axis=1)
    s = _dot_nt(kt, qg) + bias
    return s, jnp.maximum(m_ref[state], jnp.max(s, axis=0, keepdims=True))


def _attn_accumulate(u, s, m_new, m_ref, l_ref, acc_ref, pv):
    alpha = jnp.exp2(m_ref[u] - m_new)
    p = jnp.exp2(s - m_new)
    if l_ref is not None:
        l_ref[u] = alpha * l_ref[u] + jnp.sum(p, axis=0, keepdims=True)
    acc_ref[u] = alpha * acc_ref[u] + pv(p.astype(BF16))
    m_ref[u] = m_new


def _attn_step(q_ref, u, stack, kt, bias, m_ref, l_ref, acc_ref, pv, state=None):
    state = u if state is None else state
    s, m_new = _attn_logits(q_ref, u, stack, kt, bias, m_ref, state)
    _attn_accumulate(state, s, m_new, m_ref, l_ref, acc_ref, pv)


def _attn_finish(o_ref, u, stack, l_ref, acc_ref, tq, state=None):
    state = u if state is None else state
    acc = acc_ref[state]
    l = acc[HEAD_DIM:HEAD_DIM + 1] if l_ref is None else l_ref[state]
    _store_heads(o_ref, u, stack, (acc[:HEAD_DIM] / l).T, tq)


def _dsa_prompt_body(q_ref, qim_ref, wi_ref, kb_ref, vt_ref, ki2_ref, o_ref, sc_ref, planes_ref,
                     bias_ref, cut_ref, stat_ref, m_ref, l_ref, acc_ref,
                     *, tq, tk, topk, nbits):
    q0 = pl.program_id(1) * tq
    nkt = lax.div(q0 + tq + tk - 1, tk)
    wt = _transpose_wi(wi_ref[...], tq)
    qchunk = _chunk_of(q0 + lax.broadcasted_iota(I32, (1, tq), 1))

    @pl.when((pl.program_id(0) == 0) & (pl.program_id(1) == 0))
    def _():
        planes_ref[...] = jnp.zeros(planes_ref.shape, I32)

    def score_body(masked, j, carry):
        ks = pl.multiple_of(j * tk, tk)
        score = _score_tile(ki2_ref[pl.ds(ks, tk), :], qim_ref, wt)
        adm = None
        if masked:
            adm = _chunk_of(ks + lax.broadcasted_iota(I32, (tk, tq), 0)) <= qchunk
        score = _mask_scores(score, adm)
        sc_ref[j] = score
        _store_bit_planes(planes_ref, _sort_key(score), j * (tk // PLANE_KEYS))
        return carry

    n_past = lax.div(q0, tk)
    lax.fori_loop(0, n_past, functools.partial(score_body, False), 0)
    lax.fori_loop(n_past, nkt, functools.partial(score_body, True), 0)

    krow = jnp.minimum((qchunk + 1) * CHUNK, topk).astype(F32)
    _select(sc_ref, planes_ref, bias_ref, cut_ref, stat_ref, nkt, krow, tk, tq, nbits)

    _attn_reset(m_ref, l_ref, acc_ref)

    rep = B_HEADS // B_KV_HEADS
    ones = jnp.ones((SUM_ROWS, tk), BF16)

    def att_body(j, carry):
        ks = pl.multiple_of(j * tk, tk)
        bias = bias_ref[j]
        gsl = lambda g: slice(g * HEAD_DIM, (g + 1) * HEAD_DIM)
        logits = lambda g: _attn_logits(q_ref, g, rep, kb_ref[pl.ds(ks, tk), gsl(g)], bias, m_ref)
        pending = [logits(0), logits(1)]
        for g in range(B_KV_HEADS):
            if g + 2 < B_KV_HEADS:
                pending.append(logits(g + 2))
            vt = jnp.concatenate([vt_ref[j, gsl(g), :], ones], axis=0)
            _attn_accumulate(g, *pending[g], m_ref, None, acc_ref, lambda p, vt=vt: _dot(vt, p))
        return carry

    lax.fori_loop(0, nkt, att_body, 0)
    for g in range(B_KV_HEADS):
        _attn_finish(o_ref, g, rep, None, acc_ref, tq)


def _dsa_scratch(nt, tk, tq, stack, acc_rows):
    units = B_HEADS // stack
    return [pltpu.VMEM((nt, tk, tq), F32),
            pltpu.VMEM((KEY_BITS, nt * tk // KEY_BITS, tq), I32),
            pltpu.VMEM((nt, tk, tq), F32),
            pltpu.VMEM((1, tq), I32),
            pltpu.VMEM((SUBLANES, tq), F32),
            pltpu.VMEM((units, 1, stack * tq), F32),
            pltpu.VMEM((units, 1, stack * tq), F32),
            pltpu.VMEM((units, acc_rows, stack * tq), F32)]


def _dsa_prompt(q, qim, wi, kb, vt, ki2, bsz, t, tq, tk):
    nq = t // tq
    nt = t // tk
    topk = min(IDX_TOPK_MAX, t // 4)
    nbits = max(1, (t - 1).bit_length())
    qrow = lambda w: pl.BlockSpec((tq, w), lambda b, i: (b * nq + i, 0))
    seq = lambda w: pl.BlockSpec((t, w), lambda b, i: (b, 0))
    return pl.pallas_call(
        functools.partial(_dsa_prompt_body, tq=tq, tk=tk, topk=topk, nbits=nbits),
        grid=(bsz, nq),
        in_specs=[qrow(q.shape[1]), qrow(qim.shape[1]), qrow(LANES), seq(kb.shape[1]),
                  pl.BlockSpec((nt, vt.shape[1], tk), lambda b, i: (b, 0, 0)), seq(LANES)],
        out_specs=qrow(q.shape[1]),
        out_shape=jax.ShapeDtypeStruct(q.shape, BF16),
        scratch_shapes=_dsa_scratch(nt, tk, tq, B_HEADS // B_KV_HEADS, HEAD_DIM + SUM_ROWS),
        compiler_params=pltpu.CompilerParams(dimension_semantics=("arbitrary", "arbitrary"),
                                             vmem_limit_bytes=VMEM_LIMIT),
        name="dsa_prompt",
    )(q, qim, wi, kb, vt, ki2)


def _dsa_sample_body(q_ref, qim_ref, wi_ref, kn_ref, vn_ref, ki2n_ref, kidx2_ref, *refs,
                     t, tk, nsub, npt, past, topk, nbits, nbat):
    caches, refs = refs[:2 * nbat], refs[2 * nbat:]
    o_ref, sc_ref, planes_ref, bias_ref, cut_ref, stat_ref, m_ref, l_ref, acc_ref = refs
    j = pl.program_id(1)
    nj = pl.num_programs(1)
    tq = nbat * t
    rep = B_HEADS // B_KV_HEADS
    pad = jnp.zeros((tk - t, LANES), BF16)
    nb = tk // PLANE_KEYS
    rows_of = lambda b: pl.ds(b * t, t)

    @pl.when(j == 0)
    def _():
        wt = _transpose_wi(wi_ref[...], tq)
        row = lax.broadcasted_iota(I32, (tq, 1), 0)
        qims = [jnp.where((row >= b * t) & (row < (b + 1) * t), qim_ref[...].astype(F32),
                          0.0).astype(BF16) for b in range(nbat)]

        def scores(tile_of):
            total = None
            for b in range(nbat):
                s = _score_tile(tile_of(b), qims[b], wt)
                total = s if total is None else total + s
            return total

        def score_body(jt, carry):
            ks = pl.multiple_of(jt * tk, tk)
            score = _mask_scores(scores(lambda b: kidx2_ref[pl.ds(b * past + ks, tk), :]), None)
            sc_ref[jt] = score
            _store_bit_planes(planes_ref, _sort_key(score), jt * nb)
            return carry

        lax.fori_loop(0, npt, score_body, 0)
        krow_i = lax.broadcasted_iota(I32, (tk, tq), 0)
        qchunk = _chunk_of(past + (lax.broadcasted_iota(I32, (1, tq), 1) & (t - 1)))
        adm = jnp.where(krow_i < t, _chunk_of(past + krow_i), 2 ** 30) <= qchunk
        score = _mask_scores(
            scores(lambda b: jnp.concatenate([ki2n_ref[rows_of(b), :], pad], axis=0)), adm)
        sc_ref[npt] = score
        _store_bit_planes(planes_ref, _sort_key(score), npt * nb)
        krow = jnp.minimum((qchunk + 1) * CHUNK, topk).astype(F32)

        def write_bias(jt, bias):
            for b in range(nbat):
                bias_ref[b, jt] = jnp.concatenate([bias[:, b * t:(b + 1) * t]] * rep, axis=1)

        _select(sc_ref, planes_ref, write_bias, cut_ref, stat_ref, npt + 1, krow, tk, tq, nbits)
        _attn_reset(m_ref, l_ref, acc_ref)

    units = [(b, g) for b in range(nbat) for g in range(B_KV_HEADS)]
    state = lambda b, g: b * B_KV_HEADS + g

    for su in range(nsub):
        rows = lambda g: pl.ds(su * tk * B_KV_HEADS + g, tk, stride=B_KV_HEADS)
        pending = [_attn_logits(q_ref.at[rows_of(b)], g, rep,
                                caches[2 * b][rows(g), :].astype(BF16),
                                bias_ref[b, j * nsub + su], m_ref, state(b, g))
                   for b, g in units]
        for (b, g), (s, m_new) in zip(units, pending):
            vt = caches[2 * b + 1][rows(g), :].astype(BF16)
            _attn_accumulate(state(b, g), s, m_new, m_ref, l_ref, acc_ref,
                             lambda p, vt=vt: _dot_tn(vt, p))

    @pl.when(j == nj - 1)
    def _():
        for b, g in units:
            gs = slice(g * HEAD_DIM, (g + 1) * HEAD_DIM)
            kt = jnp.concatenate([kn_ref[rows_of(b), gs], pad], axis=0)
            vt = jnp.concatenate([vn_ref[rows_of(b), gs], pad], axis=0)
            _attn_step(q_ref.at[rows_of(b)], g, rep, kt, bias_ref[b, npt], m_ref, l_ref, acc_ref,
                       lambda p, vt=vt: _dot_tn(vt, p), state(b, g))
            _attn_finish(o_ref.at[rows_of(b)], g, rep, l_ref, acc_ref, t, state(b, g))


def _dsa_sample(q, qim, wi, kn, vn, ki2n, kidx2, ck, cv, bsz, t, past, tk, nsub, cache_off):
    fits = t < LANES and LANES % t == 0 and t & (t - 1) == 0 and bsz % (LANES // t) == 0
    nbat = LANES // t if fits else 1
    tq = nbat * t
    npt = past // tk
    nj = npt // nsub
    nt = npt + 1
    topk = min(IDX_TOPK_MAX, (past + t) // 4)
    nbits = max(1, (past + t - 1).bit_length())
    rep = B_HEADS // B_KV_HEADS
    units = nbat * B_KV_HEADS
    qrow = lambda w: pl.BlockSpec((tq, w), lambda g, j: (g, 0))
    kvw = kn.shape[1]
    cache = lambda b: pl.BlockSpec((nsub * tk * B_KV_HEADS, HEAD_DIM),
                                   lambda g, j: ((cache_off + g * nbat + b) * nj + j, 0))
    caches = [spec for b in range(nbat) for spec in (cache(b), cache(b))]
    scratch = [pltpu.VMEM((nt, tk, tq), F32),
               pltpu.VMEM((KEY_BITS, nt * tk // KEY_BITS, tq), I32),
               pltpu.VMEM((nbat, nt, tk, rep * t), F32),
               pltpu.VMEM((1, tq), I32),
               pltpu.VMEM((SUBLANES, tq), F32),
               pltpu.VMEM((units, 1, rep * t), F32),
               pltpu.VMEM((units, 1, rep * t), F32),
               pltpu.VMEM((units, HEAD_DIM, rep * t), F32)]
    return pl.pallas_call(
        functools.partial(_dsa_sample_body, t=t, tk=tk, nsub=nsub, npt=npt, past=past,
                          topk=topk, nbits=nbits, nbat=nbat),
        grid=(bsz // nbat, nj),
        in_specs=[qrow(q.shape[1]), qrow(qim.shape[1]), qrow(LANES), qrow(kvw), qrow(kvw),
                  qrow(LANES), pl.BlockSpec((nbat * past, LANES), lambda g, j: (g, 0))] + caches,
        out_specs=qrow(q.shape[1]),
        out_shape=jax.ShapeDtypeStruct(q.shape, BF16),
        scratch_shapes=scratch,
        compiler_params=pltpu.CompilerParams(dimension_semantics=("arbitrary", "arbitrary"),
                                             vmem_limit_bytes=VMEM_LIMIT),
        name="dsa_sample",
    )(q, qim, wi, kn, vn, ki2n, kidx2, *([ck, cv] * nbat))


def _post_body(oa_ref, ob_ref, ga_ref, gb_ref, x_ref, wa_ref, wb_ref, wo_ref, g2_ref, wr_ref,
               br_ref, x1_o, t_o, eidx_o, prob_o, cnt_o):
    ya = _dot(oa_ref[...], wa_ref[...])
    yb = _dot(ob_ref[...], wb_ref[...])
    merged = (jax.nn.sigmoid(ga_ref[...].astype(F32)) * ya
              + jax.nn.sigmoid(gb_ref[...].astype(F32)) * yb)
    x1 = x_ref[...] + _dot(merged.astype(BF16), wo_ref[...])
    x1_o[...] = x1
    tok = _rms(x1) * g2_ref[...]
    t_o[...] = tok
    logits = _dot(tok.astype(BF16), wr_ref[...]) + br_ref[...]
    tm = logits.shape[0]
    lane = lax.broadcasted_iota(I32, (tm, LANES), 1).astype(F32)
    cur = jnp.where(lane < N_EXPERTS, logits, -jnp.inf)
    top = None
    den = jnp.zeros((tm, 1), F32)
    eidx = jnp.zeros((tm, LANES), F32)
    prob = jnp.zeros((tm, LANES), F32)
    chosen = jnp.zeros((tm, LANES), F32)
    for k in range(TOP_K):
        mx = jnp.max(cur, axis=1, keepdims=True)
        first = jnp.min(jnp.where(cur == mx, lane, float(LANES)), axis=1, keepdims=True)
        if top is None:
            top = mx
        e = jnp.exp(mx - top)
        den = den + e
        eidx = jnp.where(lane == k, first, eidx)
        prob = jnp.where(lane == k, e, prob)
        hit = lane == first
        chosen = jnp.where(hit, 1.0, chosen)
        cur = jnp.where(hit, -jnp.inf, cur)
    eidx_o[...] = eidx.astype(I32)
    prob_o[...] = prob / den

    @pl.when(pl.program_id(0) == 0)
    def _():
        cnt_o[...] = jnp.zeros(cnt_o.shape, F32)

    cnt_o[...] += jnp.sum(chosen, axis=0, keepdims=True)


def _post(oa, ob, ga, gb, x2, wa, wb, wo, g2, wr, br, tm):
    r, d = x2.shape
    row = lambda w: pl.BlockSpec((tm, w), lambda i: (i, 0))
    return pl.pallas_call(
        _post_body,
        grid=(r // tm,),
        in_specs=[row(d), row(d), row(d), row(d), row(d), _const_spec(wa.shape),
                  _const_spec(wb.shape), _const_spec(wo.shape), _const_spec(g2.shape),
                  _const_spec(wr.shape), _const_spec(br.shape)],
        out_specs=[row(d), row(d), row(LANES), row(LANES),
                   pl.BlockSpec((1, LANES), lambda i: (0, 0))],
        out_shape=[jax.ShapeDtypeStruct((r, d), F32), jax.ShapeDtypeStruct((r, d), F32),
                   jax.ShapeDtypeStruct((r, LANES), I32), jax.ShapeDtypeStruct((r, LANES), F32),
                   jax.ShapeDtypeStruct((1, LANES), F32)],
        compiler_params=pltpu.CompilerParams(dimension_semantics=("arbitrary",),
                                             vmem_limit_bytes=VMEM_LIMIT),
        name=f"post_{r // tm}",
    )(oa, ob, ga, gb, x2, wa, wb, wo, g2, wr, br)


MOE_TILE = 512
DMA_PRIORITIES = 2
ISSUE_UNROLL = 8


WIN_ROWS = 64
FLAG_LANE = LANES - 1


def _rank_body(eidx_ref, cin_ref, pos_o, col_o, win_o, carry_ref, *, max_start):
    @pl.when(pl.program_id(0) == 0)
    def _():
        carry_ref[...] = cin_ref[...]

    eidx = eidx_ref[...]
    tm = eidx.shape[0]
    lane = lax.broadcasted_iota(I32, (tm, LANES), 1)
    hits = [lane == eidx[:, k:k + 1] for k in range(TOP_K)]
    onehot = jnp.zeros((tm, LANES), F32)
    for hit in hits:
        onehot = onehot + jnp.where(hit, 1.0, 0.0)
    r_i = lax.broadcasted_iota(I32, (tm, tm), 0)
    c_i = lax.broadcasted_iota(I32, (tm, tm), 1)
    before = jnp.where(c_i < r_i, 1.0, 0.0).astype(BF16)
    carry = carry_ref[...]
    start = jnp.minimum((carry.astype(I32) >> 3) << 3, max_start)
    base = carry + _dot(before, onehot.astype(BF16))
    rank = jnp.zeros((tm, LANES), F32)
    col = jnp.zeros((tm, LANES), F32)
    worst = jnp.zeros((tm, 1), F32)
    for k, hit in enumerate(hits):
        rk = jnp.sum(jnp.where(hit, base, 0.0), axis=1, keepdims=True)
        off = jnp.sum(jnp.where(hit, base - start.astype(F32), 0.0), axis=1, keepdims=True)
        rank = jnp.where(lane == k, rk, rank)
        col = jnp.where(lane == k, eidx[:, k:k + 1].astype(F32) * WIN_ROWS + off, col)
        worst = jnp.maximum(worst, off)
    pos_o[...] = rank.T[:SUBLANES, :].astype(I32)
    col_o[...] = col.astype(I32)
    flag = (jnp.max(worst) >= WIN_ROWS).astype(I32)
    win_o[...] = jnp.where(lane[:1] == FLAG_LANE, flag, start).reshape(win_o.shape)
    carry_ref[...] += jnp.sum(onehot, axis=0, keepdims=True)


def _rank(eidx, cin, tm, max_start):
    r = eidx.shape[0]
    return pl.pallas_call(
        functools.partial(_rank_body, max_start=max_start),
        grid=(r // tm,),
        in_specs=[pl.BlockSpec((tm, LANES), lambda i: (i, 0)), _const_spec(cin.shape)],
        out_specs=[pl.BlockSpec((SUBLANES, tm), lambda i: (0, i)),
                   pl.BlockSpec((tm, LANES), lambda i: (i, 0)),
                   pl.BlockSpec((1, 1, LANES), lambda i: (i, 0, 0))],
        out_shape=[jax.ShapeDtypeStruct((SUBLANES, r), I32),
                   jax.ShapeDtypeStruct((r, LANES), I32),
                   jax.ShapeDtypeStruct((r // tm, 1, LANES), I32)],
        scratch_shapes=[pltpu.VMEM((1, LANES), F32)],
        compiler_params=pltpu.CompilerParams(dimension_semantics=("arbitrary",)),
        name=f"moe_rank_{r // tm}",
    )(eidx, cin)


def _wait_rows(ref, sem, times):
    for _ in range(times):
        pltpu.make_async_copy(ref, ref, sem).wait()


def _zero_pad_rows(pad_ref, xs_out, zeros, sem):
    zeros[...] = jnp.zeros(zeros.shape, zeros.dtype)
    row = zeros.at[pl.ds(0, 1)]
    rows8 = zeros.at[pl.ds(0, SUBLANES)]
    tile = zeros.shape[0]

    def for_each_copy(fn):
        def per_group(g, carry):
            first, n = pad_ref[0, g], pad_ref[1, g]
            head = jnp.minimum(n, (SUBLANES - (first & (SUBLANES - 1))) & (SUBLANES - 1))

            def per_row(t, c):
                fn(pltpu.make_async_copy(row, xs_out.at[pl.ds(first + t, 1)], sem))
                return c
            lax.fori_loop(0, head, per_row, 0)

            def per_block(t, c):
                at = pl.multiple_of(first + head + t * SUBLANES, SUBLANES)
                fn(pltpu.make_async_copy(rows8, xs_out.at[pl.ds(at, SUBLANES)], sem))
                return c
            lax.fori_loop(0, lax.div(n - head, SUBLANES), per_block, 0)
            return carry
        lax.fori_loop(0, N_EXPERTS, per_group, 0)
        first = pad_ref[0, N_EXPERTS]

        def per_tile(t, c):
            fn(pltpu.make_async_copy(
                zeros, xs_out.at[pl.ds(pl.multiple_of(first + t * tile, tile), tile)], sem))
            return c
        lax.fori_loop(0, lax.div(pad_ref[1, N_EXPERTS], tile), per_tile, 0)

    for_each_copy(lambda copy: copy.start())
    for_each_copy(lambda copy: copy.wait())


def _dispatch_body(pad_ref, *refs, bounds):
    ns = len(bounds) - 1
    pos_refs, tok_refs = refs[:ns], refs[ns:2 * ns]
    xs_out, ring, sems, zeros, zsem = refs[2 * ns:]
    i = pl.program_id(0)
    tm = ring.shape[1]
    last = pl.num_programs(0) - 1

    pl.when(i == 0)(functools.partial(_zero_pad_rows, pad_ref, xs_out, zeros, zsem))

    for slot in range(2):
        mine = i % 2 == slot
        for s in range(ns):
            @pl.when(mine & (i >= bounds[s]) & (i < bounds[s + 1]))
            def _(slot=slot, s=s):
                ring[slot] = tok_refs[s][...]

                def issue(t, carry):
                    for k in range(TOP_K):
                        pltpu.make_async_copy(ring.at[slot, pl.ds(t, 1)],
                                              xs_out.at[pl.ds(pos_refs[s][t * TOP_K + k], 1)],
                                              sems.at[slot]).start(priority=k % DMA_PRIORITIES)
                    return carry

                lax.fori_loop(0, tm, issue, 0)

        @pl.when(mine & (i > 0))
        def _(slot=slot):
            _wait_rows(ring.at[1 - slot], sems.at[1 - slot], TOP_K)

        @pl.when(mine & (i == last))
        def _(slot=slot):
            _wait_rows(ring.at[slot], sems.at[slot], TOP_K)


def _dispatch(poss, pad, toks, rows, tm):
    d = toks[0].shape[1]
    bounds = [0]
    for tok in toks:
        bounds.append(bounds[-1] + tok.shape[0] // tm)

    def local(s):
        lo, n = bounds[s], bounds[s + 1] - bounds[s]
        return lambda i: jnp.clip(i - lo, 0, n - 1)

    in_specs = [pl.BlockSpec(memory_space=pltpu.SMEM)]
    in_specs += [pl.BlockSpec((tm * TOP_K,), lambda i, f=local(s): (f(i),),
                              memory_space=pltpu.SMEM) for s in range(len(toks))]
    in_specs += [pl.BlockSpec((tm, d), lambda i, f=local(s): (f(i), 0)) for s in range(len(toks))]
    return pl.pallas_call(
        functools.partial(_dispatch_body, bounds=tuple(bounds)),
        grid=(bounds[-1],),
        in_specs=in_specs,
        out_specs=pl.BlockSpec(memory_space=pl.ANY),
        out_shape=jax.ShapeDtypeStruct((rows, d), toks[0].dtype),
        scratch_shapes=[pltpu.VMEM((2, tm, d), toks[0].dtype), pltpu.SemaphoreType.DMA((2,)),
                        pltpu.VMEM((MOE_TILE, d), toks[0].dtype), pltpu.SemaphoreType.DMA],
        compiler_params=pltpu.CompilerParams(dimension_semantics=("arbitrary",),
                                             has_side_effects=True),
        name="moe_dispatch",
    )(pad, *poss, *toks)


def _experts_body(te_ref, na_ref, first_ref, next_ref, slot_ref, x_ref, wgu_hbm, bgu_ref, wd_hbm,
                  bd_ref, y_ref, wgu_f, wd_f, wgu_s, wd_s, sems):
    r = pl.program_id(0)

    def fetch(e, slot):
        return (pltpu.make_async_copy(wgu_hbm.at[e], wgu_f.at[slot], sems.at[0, slot]),
                pltpu.make_async_copy(wd_hbm.at[e], wd_f.at[slot], sems.at[1, slot]))

    @pl.when(r == 0)
    def _():
        for copy in fetch(te_ref[0], slot_ref[0]):
            copy.start()

    @pl.when(first_ref[r] != 0)
    def _():
        slot = slot_ref[r]
        for copy in fetch(te_ref[r], slot):
            copy.wait()
        rows = 128
        for c in range(wgu_s.shape[0] // rows):
            wgu_s[c * rows:(c + 1) * rows, :] = wgu_f[slot, c * rows:(c + 1) * rows, :].astype(BF16)
        for c in range(wd_s.shape[0] // rows):
            wd_s[c * rows:(c + 1) * rows, :] = wd_f[slot, c * rows:(c + 1) * rows, :].astype(BF16)

        @pl.when(next_ref[r] >= 0)
        def _():
            for copy in fetch(next_ref[r], 1 - slot):
                copy.start()

    @pl.when(r < na_ref[0])
    def _():
        gu = _dot(x_ref[...].astype(BF16), wgu_s[...]) + bgu_ref[0]
        dff = gu.shape[1] // 2
        gate = jnp.minimum(gu[:, :dff], SWIGLU_LIMIT)
        up = jnp.clip(gu[:, dff:], -SWIGLU_LIMIT, SWIGLU_LIMIT)
        act = (up + 1.0) * gate * jax.nn.sigmoid(SWIGLU_ALPHA * gate)
        y_ref[...] = _dot(act.astype(BF16), wd_s[...]) + bd_ref[0]

    @pl.when(r >= na_ref[0])
    def _():
        y_ref[...] = jnp.zeros(y_ref.shape, F32)


def _experts(tile_expert, n_active, xs, wgu, bgu, wd, bd):
    p, d = xs.shape
    _, _, dff2 = wgu.shape
    tm = MOE_TILE
    n = p // tm
    r = jnp.arange(n, dtype=I32)
    active = r < n_active[0]
    change = jnp.concatenate([jnp.ones((1,), bool), tile_expert[1:] != tile_expert[:-1]])
    first = (change & active).astype(I32)
    group = jnp.cumsum(first) - 1
    slot = jnp.maximum(group, 0) % 2
    opens = jnp.where(first > 0, r, n)
    next_open = jnp.min(jnp.where(opens[None, :] > r[:, None], opens[None, :], n), axis=1)
    nxt = jnp.where(next_open < n, tile_expert[jnp.minimum(next_open, n - 1)], -1).astype(I32)
    grid_spec = pltpu.PrefetchScalarGridSpec(
        num_scalar_prefetch=5,
        grid=(n,),
        in_specs=[pl.BlockSpec((tm, d), lambda r, te, na, *_: (jnp.minimum(r, na[0] - 1), 0)),
                  pl.BlockSpec(memory_space=pl.ANY),
                  pl.BlockSpec((1, 1, dff2), lambda r, te, *_: (te[r], 0, 0)),
                  pl.BlockSpec(memory_space=pl.ANY),
                  pl.BlockSpec((1, 1, d), lambda r, te, *_: (te[r], 0, 0))],
        out_specs=pl.BlockSpec((tm, d), lambda r, te, *_: (r, 0)),
        scratch_shapes=[pltpu.VMEM((2, d, dff2), F32), pltpu.VMEM((2, dff2 // 2, d), F32),
                        pltpu.VMEM((d, dff2), BF16), pltpu.VMEM((dff2 // 2, d), BF16),
                        pltpu.SemaphoreType.DMA((2, 2))],
    )
    return pl.pallas_call(
        _experts_body,
        grid_spec=grid_spec,
        out_shape=jax.ShapeDtypeStruct((p, d), F32),
        compiler_params=pltpu.CompilerParams(dimension_semantics=("arbitrary",),
                                             vmem_limit_bytes=VMEM_LIMIT),
        name="moe_experts",
    )(tile_expert, n_active, first, nxt, slot.astype(I32), xs, wgu, bgu, wd, bd)


def _combine_body(slow_ref, win_ref, pos_ref, col_ref, prob_ref, x1_ref, y_hbm, out_ref,
                  wbuf, rbuf, sems):
    i = pl.program_id(0)
    n = pl.num_programs(0) - 1
    tm = x1_ref.shape[0]
    slow = slow_ref[0] != 0
    nwin = wbuf.shape[1] // WIN_ROWS

    for slot in range(2):
        mine = i % 2 == slot

        @pl.when(mine & (i < n) & jnp.logical_not(slow))
        def _(slot=slot):
            for e in range(nwin):
                first = pl.multiple_of(win_ref[0, 0, e], SUBLANES)
                pltpu.make_async_copy(y_hbm.at[pl.ds(first, WIN_ROWS)],
                                      wbuf.at[slot, pl.ds(e * WIN_ROWS, WIN_ROWS)],
                                      sems.at[slot]).start()

        @pl.when(mine & (i > 0) & jnp.logical_not(slow))
        def _(slot=slot):
            _wait_rows(wbuf.at[1 - slot], sems.at[1 - slot], 1)
            col = col_ref[...]
            prob = prob_ref[...]
            lane = lax.broadcasted_iota(I32, (tm, LANES), 1)
            rel = [jnp.broadcast_to(col[:, k:k + 1], (tm, LANES)) - lane for k in range(TOP_K)]
            wgt = [jnp.broadcast_to(prob[:, k:k + 1], (tm, LANES)) for k in range(TOP_K)]
            pick = []
            for c in range(wbuf.shape[1] // LANES):
                g = jnp.zeros((tm, LANES), F32)
                for k in range(TOP_K):
                    g = jnp.where(rel[k] == c * LANES, wgt[k], g)
                pick.append(g.astype(BF16))
            ffn = _dot(jnp.concatenate(pick, axis=1), wbuf[1 - slot].astype(BF16))
            out_ref[...] = x1_ref[...] + ffn

        @pl.when(mine & (i < n) & slow)
        def _(slot=slot):
            def issue(t, carry):
                for k in range(TOP_K):
                    pltpu.make_async_copy(y_hbm.at[pl.ds(pos_ref[k, t], 1)],
                                          rbuf.at[slot, k, pl.ds(t, 1)],
                                          sems.at[slot]).start(priority=k % DMA_PRIORITIES)
                return carry
            lax.fori_loop(0, tm, issue, 0)

        @pl.when(mine & (i > 0) & slow)
        def _(slot=slot):
            _wait_rows(rbuf.at[1 - slot, 0], sems.at[1 - slot], TOP_K)
            prob = prob_ref[...]
            ffn = prob[:, 0:1] * rbuf[1 - slot, 0]
            for k in range(1, TOP_K):
                ffn = ffn + prob[:, k:k + 1] * rbuf[1 - slot, k]
            out_ref[...] = x1_ref[...] + ffn


def _combine(slow, win, pos, col, prob, x1, ys, tm):
    r, d = x1.shape
    n = r // tm
    prev = lambda w: pl.BlockSpec((tm, w), lambda i: (jnp.maximum(i - 1, 0), 0))
    cur = lambda i: jnp.minimum(i, n - 1)
    return pl.pallas_call(
        _combine_body,
        grid=(n + 1,),
        in_specs=[pl.BlockSpec(memory_space=pltpu.SMEM),
                  pl.BlockSpec((1, 1, LANES), lambda i: (cur(i), 0, 0), memory_space=pltpu.SMEM),
                  pl.BlockSpec((SUBLANES, tm), lambda i: (0, cur(i)), memory_space=pltpu.SMEM),
                  prev(LANES), prev(LANES), prev(d), pl.BlockSpec(memory_space=pl.ANY)],
        out_specs=prev(d),
        out_shape=jax.ShapeDtypeStruct((r, d), F32),
        scratch_shapes=[pltpu.VMEM((2, N_EXPERTS * WIN_ROWS, d), F32),
                        pltpu.VMEM((2, TOP_K, tm, d), F32), pltpu.SemaphoreType.DMA((2,))],
        compiler_params=pltpu.CompilerParams(dimension_semantics=("arbitrary",),
                                             vmem_limit_bytes=VMEM_LIMIT),
        name=f"moe_combine_{n}",
    )(slow, win, pos, col, prob, x1, ys)


def _moe(streams, wgu, bgu, wd, bd):
    tm = MOE_TILE
    counts = [s[4][0, :N_EXPERTS].astype(I32) for s in streams]
    cnt = sum(counts)
    padded = ((cnt + tm - 1) // tm) * tm
    ends = jnp.cumsum(padded)
    n_pairs = sum(s[0].shape[0] for s in streams) * TOP_K
    n_tiles = n_pairs // tm + N_EXPERTS
    tile_start = jnp.arange(n_tiles, dtype=I32) * tm
    tile_expert = jnp.minimum(jnp.sum((ends[None, :] <= tile_start[:, None]).astype(I32), axis=1),
                              N_EXPERTS - 1)
    n_active = (ends[-1:] // tm).astype(I32)
    start = ends - padded
    pad = jnp.stack([jnp.append(start + cnt, ends[-1]),
                     jnp.append(padded - cnt, n_tiles * tm - ends[-1])])
    routes = []
    for (x1, tok, eidx, prob, _), c in zip(streams, counts):
        cin = jnp.pad(start, (0, LANES - N_EXPERTS)).astype(F32)[None]
        routes.append(_rank(eidx, cin, _row_tile(eidx.shape[0], 256), n_tiles * tm - WIN_ROWS))
        start = start + c
    flat = [r[0][:TOP_K].T.reshape(-1) for r in routes]
    xs = _dispatch(flat, pad, [s[1] for s in streams], n_tiles * tm,
                   min(_row_tile(s[1].shape[0], 256) for s in streams))
    ys = _experts(tile_expert, n_active, xs, wgu, bgu, wd, bd)
    outs = []
    for s, (pos, col, win) in zip(streams, routes):
        slow = jnp.max(win[:, 0, FLAG_LANE]).reshape(1)
        outs.append(_combine(slow, win, pos, col, s[3], s[0], ys, _row_tile(s[0].shape[0], 256)))
    return outs


def _row_tile(r, want):
    tm = min(r, want)
    assert r % tm == 0, (r, tm)
    return tm


def _mixers(x2, bsz, t, pos, s0, caches, wts, layer):
    (g1, wm, ws, wg, qg, kg, lbp, ng) = wts
    r = x2.shape[0]
    tm = _row_tile(r, 256)
    assert t % tm == 0 or tm % t == 0
    tabs_a = _rope_tables(pos, HEAD_DIM, 1)
    tabs_b = _rope_tables(pos, IDX_DIM, LANES // IDX_DIM)
    tabs = tabs_a + tabs_b
    if tm > t:
        tabs = tuple(jnp.tile(tb, (tm // t, 1)) for tb in tabs)
    tk = max(tm, _row_tile(t, 512)) if caches is None else tm
    (aq, af, ai, ag, q, kf, kb, vf, vb, qim, kif, ki2, wi, ga, gb, vt) = _inproj(
        x2, g1, wm, ws, wg, qg, kg, tabs, tm, tk)
    oa, s_new = _hgrn(aq, af, ai, ag, s0, lbp, ng, bsz, t, layer)
    if caches is None:
        ob = _dsa_prompt(q, qim, wi, kb, vt, ki2, bsz, t, _row_tile(t, 256), tk)
    else:
        ck, cv, kidx2, past, cache_off = caches
        tk = _row_tile(past, 512)
        nsub = 2 if (past // tk) % 2 == 0 else 1
        ob = _dsa_sample(q, qim, wi, kb, vb, ki2, kidx2, ck, cv, bsz, t, past, tk, nsub, cache_off)
    return oa, ob, ga, gb, kf, vf, kif, s_new


def kernel(x_prompt, x_sample, cache_k, cache_v, cache_kidx, state_hgrn, norm1_g, w_in, lower_bounds, hgrn_norm_g, q_norm_g, k_norm_g, w_branch_a, w_branch_b, w_out, norm2_g, w_router, b_router, w_gate_up, b_gate_up, w_down, b_down):
    bp, tp, d = x_prompt.shape
    bs, ts, _ = x_sample.shape
    depth = w_in.shape[0]
    past = cache_k.shape[2]
    kvw = B_KV_HEADS * HEAD_DIM
    pos_p = jnp.arange(tp, dtype=I32)
    pos_s = past + jnp.arange(ts, dtype=I32)
    xp = x_prompt.reshape(bp * tp, d)
    xs = x_sample.reshape(bs * ts, d)
    n_main = 2 * A_HEADS * A_DK + 2 * A_HEADS * A_DV + B_HEADS * HEAD_DIM + 2 * kvw + IDX_HEADS * IDX_DIM
    n_small = n_main + IDX_DIM + IDX_HEADS
    outs = [[] for _ in range(8)]
    for l in range(depth):
        w = w_in[l]
        wm = w[:, :n_main].astype(BF16)
        w_ik = w[:, n_main:n_main + IDX_DIM]
        w_iw = w[:, n_main + IDX_DIM:n_small]
        ws = jnp.concatenate(
            [w_ik, w_ik, w_iw, jnp.zeros((d, LANES - IDX_HEADS), w.dtype)], axis=1).astype(BF16)
        wg = w[:, n_small:].astype(BF16)
        wts = (norm1_g[l][None], wm, ws, wg, q_norm_g[l][None], k_norm_g[l][None],
               lower_bounds, hgrn_norm_g[l][None])
        s0_p = jnp.zeros((bp, A_HEADS, A_DK, A_DV), F32)
        oa_p, ob_p, ga_p, gb_p, kp, vp, kip, sp = _mixers(xp, bp, tp, pos_p, s0_p, None, wts, l)
        kidx2 = jnp.concatenate([cache_kidx[l], cache_kidx[l]], axis=-1).astype(BF16)
        caches = (cache_k.reshape(-1, HEAD_DIM), cache_v.reshape(-1, HEAD_DIM),
                  kidx2.reshape(bs * past, LANES), past, l * bs)
        oa_s, ob_s, ga_s, gb_s, ks, vs, kis, ss = _mixers(
            xs, bs, ts, pos_s, state_hgrn[l], caches, wts, l)

        wa = w_branch_a[l].astype(BF16)
        wb = w_branch_b[l].astype(BF16)
        wo = w_out[l].astype(BF16)
        wr = jnp.pad(w_router[l], ((0, 0), (0, LANES - N_EXPERTS))).astype(BF16)
        br = jnp.pad(b_router[l], (0, LANES - N_EXPERTS))[None]
        g2 = norm2_g[l][None]
        bgu = b_gate_up[l][:, None, :]
        bd = b_down[l][:, None, :]
        streams = []
        for (x2, oa, ob, ga, gb) in ((xp, oa_p, ob_p, ga_p, gb_p), (xs, oa_s, ob_s, ga_s, gb_s)):
            r = x2.shape[0]
            streams.append(_post(oa, ob, ga, gb, x2, wa, wb, wo, g2, wr, br, _row_tile(r, 512)))
        xp, xs = _moe(streams, w_gate_up[l], bgu, w_down[l], bd)
        for lst, v in zip(outs, (kp.reshape(bp, tp, B_KV_HEADS, HEAD_DIM),
                                 vp.reshape(bp, tp, B_KV_HEADS, HEAD_DIM),
                                 kip.reshape(bp, tp, IDX_DIM), sp,
                                 ks.reshape(bs, ts, B_KV_HEADS, HEAD_DIM),
                                 vs.reshape(bs, ts, B_KV_HEADS, HEAD_DIM),
                                 kis.reshape(bs, ts, IDX_DIM), ss)):
            lst.append(v)
    return (xp.reshape(bp, tp, d), xs.reshape(bs, ts, d)) + tuple(jnp.stack(o) for o in outs)
```
